```python
import numpy as np
import jax
import jax.numpy as jnp
from jax import lax

D_MODEL = 2048
BATCH = 4
SEQ = 4096
DEPTH = 2

HEAD_DIM = 64
SWA_HEADS = D_MODEL // HEAD_DIM // 2
SWA_KV_HEADS = 2
SWA_WINDOW = 128
NSA_HEADS = D_MODEL // HEAD_DIM - SWA_HEADS
NSA_KV_GROUPS = 2
NSA_CMP_LEN = 32
NSA_CMP_STRIDE = 16
NSA_CMP_HIDDEN = 256
NSA_SLC_LEN = 64
NSA_TOPK = 16
NSA_WINDOW = 512
Q_BLOCK = 128
NSA_Q_CHUNK = 64
N_GROUPS = 8
EXPERTS_PER_GROUP = 8
N_EXPERTS = N_GROUPS * EXPERTS_PER_GROUP
EXPERT_TOPK = 2
EXPERT_HIDDEN = 512
EXPERT_ROWS = 128
SWA_Q_W = SWA_HEADS * HEAD_DIM
SWA_KV_W = SWA_KV_HEADS * HEAD_DIM
NSA_Q_W = NSA_HEADS * HEAD_DIM
NSA_KV_W = NSA_KV_GROUPS * HEAD_DIM
NSA_GATE_W = 3 * NSA_HEADS
IN_COLS = SWA_Q_W + 2 * SWA_KV_W + NSA_Q_W + 6 * NSA_KV_W + NSA_GATE_W
MIX_WIDTH = SWA_Q_W + NSA_Q_W
ALPHA = (2.0 * DEPTH) ** 0.25
BETA = (8.0 * DEPTH) ** -0.25
LN_EPS = 1e-5

kernel_name = 'hybrid_swa_nsa_hier_moe'


def layer_norm(x, g, b):
    xf = x.astype(jnp.float32)
    mu = jnp.mean(xf, axis=-1, keepdims=True)
    var = jnp.mean(jnp.square(xf - mu), axis=-1, keepdims=True)
    return ((xf - mu) * lax.rsqrt(var + LN_EPS) * g + b).astype(x.dtype)


def alibi_slopes():
    n = SWA_HEADS + NSA_HEADS
    s = jnp.exp2(-8.0 * jnp.arange(1, n + 1, dtype=jnp.float32) / n)
    return s[0::2], s[1::2]


def banded_attention(q, k, v, slopes, window, sinks=None):
    B, S, G, R, Dh = q.shape
    nb = S // Q_BLOCK
    span = window + Q_BLOCK
    kp = jnp.pad(k, ((0, 0), (window, 0), (0, 0), (0, 0)))
    vp = jnp.pad(v, ((0, 0), (window, 0), (0, 0), (0, 0)))
    qb = q.reshape(B, nb, Q_BLOCK, G, R, Dh).swapaxes(0, 1)
    scale = Dh ** -0.5
    slope = slopes.astype(jnp.float32)[None, :, :, None, None]

    def block(args):
        j, qj = args
        kj = lax.dynamic_slice_in_dim(kp, j * Q_BLOCK, span, axis=1)
        vj = lax.dynamic_slice_in_dim(vp, j * Q_BLOCK, span, axis=1)
        t = j * Q_BLOCK + jnp.arange(Q_BLOCK)
        s = j * Q_BLOCK - window + jnp.arange(span)
        dist = t[:, None] - s[None, :]
        mask = (dist >= 0) & (dist < window) & (s[None, :] >= 0)
        logits = jnp.einsum('bqgrd,bkgd->bgrqk', qj, kj).astype(jnp.float32) * scale
        logits = logits - slope * dist.astype(jnp.float32)
        logits = jnp.where(mask, logits, -jnp.inf)
        m = jnp.max(logits, axis=-1, keepdims=True)
        if sinks is None:
            p = jnp.exp(logits - m)
            denom = jnp.sum(p, axis=-1, keepdims=True)
        else:
            sink = sinks.astype(jnp.float32)[None, :, :, None, None]
            m = jnp.maximum(m, sink)
            p = jnp.exp(logits - m)
            denom = jnp.sum(p, axis=-1, keepdims=True) + jnp.exp(sink - m)
        p = (p / denom).astype(vj.dtype)
        return jnp.einsum('bgrqk,bkgd->bqgrd', p, vj)

    out = lax.map(block, (jnp.arange(nb), qb))
    return out.swapaxes(0, 1).reshape(B, S, G, R, Dh)


def compress_blocks(kv, pe, w1, w2):
    B, S, G, Dh = kv.shape
    n_cmp = (S - NSA_CMP_LEN) // NSA_CMP_STRIDE + 1
    idx = jnp.arange(n_cmp)[:, None] * NSA_CMP_STRIDE + jnp.arange(NSA_CMP_LEN)[None, :]
    blocks = kv[:, idx] + pe[None, None, :, None, :]
    flat = blocks.transpose(0, 1, 3, 2, 4).reshape(B, n_cmp, G, NSA_CMP_LEN * Dh)
    return jax.nn.gelu(flat @ w1) @ w2


def nsa_compressed_selected(q, k_cmp, v_cmp, k_slc, v_slc, slopes):
    B, S, G, R, Dh = q.shape
    n_cmp = k_cmp.shape[1]
    n_slc = S // NSA_SLC_LEN
    k_sel = min(NSA_TOPK, n_slc)
    nc = S // NSA_Q_CHUNK
    scale = Dh ** -0.5
    slope = slopes.astype(jnp.float32)
    cmp_start = jnp.arange(n_cmp) * NSA_CMP_STRIDE
    cmp_end = cmp_start + NSA_CMP_LEN - 1
    slc_start = jnp.arange(n_slc) * NSA_SLC_LEN
    overlap = ((cmp_start[:, None] < slc_start[None, :] + NSA_SLC_LEN)
               & (cmp_end[:, None] >= slc_start[None, :])).astype(jnp.float32)
    ks_blocks = k_slc.reshape(B, n_slc, NSA_SLC_LEN, G, Dh).transpose(0, 3, 1, 2, 4)
    vs_blocks = v_slc.reshape(B, n_slc, NSA_SLC_LEN, G, Dh).transpose(0, 3, 1, 2, 4)
    bi = jnp.arange(B)[:, None, None, None]
    gi = jnp.arange(G)[None, :, None, None]
    blk = jnp.arange(n_slc)
    qc = q.reshape(B, nc, NSA_Q_CHUNK, G, R, Dh).swapaxes(0, 1)

    def chunk(args):
        c, qj = args
        t = c * NSA_Q_CHUNK + jnp.arange(NSA_Q_CHUNK)
        dist_c = (t[:, None] - cmp_end[None, :]).astype(jnp.float32)
        lc = jnp.einsum('bqgrd,bngd->bgrqn', qj, k_cmp).astype(jnp.float32) * scale
        lc = lc - slope[None, :, :, None, None] * dist_c
        lc = jnp.where(dist_c >= 0, lc, -jnp.inf)
        mc = jnp.max(lc, axis=-1, keepdims=True)
        mc = jnp.where(jnp.isfinite(mc), mc, 0.0)
        pc = jnp.exp(lc - mc)
        dc = jnp.sum(pc, axis=-1, keepdims=True)
        pc = pc / jnp.where(dc > 0, dc, 1.0)
        o_cmp = jnp.einsum('bgrqn,bngd->bqgrd', pc.astype(v_cmp.dtype), v_cmp)
        imp = jnp.einsum('bgrqn,nj->bgqj', pc, overlap)
        cur = t // NSA_SLC_LEN
        valid = slc_start[None, :] <= t[:, None]
        forced = (blk[None, :] == 0) | (blk[None, :] == cur[:, None]) | (blk[None, :] == cur[:, None] - 1)
        score = jnp.where(forced, jnp.inf, imp)
        score = jnp.where(valid, score, -jnp.inf)
        top_s, top_i = lax.top_k(score, k_sel)
        kg = ks_blocks[bi, gi, top_i]
        vg = vs_blocks[bi, gi, top_i]
        pos = top_i[..., None] * NSA_SLC_LEN + jnp.arange(NSA_SLC_LEN)
        dist_s = t[None, None, :, None, None] - pos
        ok = (top_s > -jnp.inf)[..., None] & (dist_s >= 0)
        ls = jnp.einsum('bqgrd,bgqkld->bgrqkl', qj, kg).astype(jnp.float32) * scale
        ls = ls - slope[None, :, :, None, None, None] * dist_s.astype(jnp.float32)[:, :, None]
        ls = jnp.where(ok[:, :, None], ls, -jnp.inf)
        ps = jax.nn.softmax(ls.reshape(B, G, R, NSA_Q_CHUNK, k_sel * NSA_SLC_LEN), axis=-1)
        ps = ps.reshape(B, G, R, NSA_Q_CHUNK, k_sel, NSA_SLC_LEN).astype(vg.dtype)
        o_slc = jnp.einsum('bgrqkl,bgqkld->bqgrd', ps, vg)
        return o_cmp, o_slc

    o_cmp, o_slc = lax.map(chunk, (jnp.arange(nc), qc))
    o_cmp = o_cmp.swapaxes(0, 1).reshape(B, S, G, R, Dh)
    o_slc = o_slc.swapaxes(0, 1).reshape(B, S, G, R, Dh)
    return o_cmp, o_slc


def hybrid_mixer(x, w_in, b_in, sinks, pe_k, w1_k, w2_k, pe_v, w1_v, w2_v, w_out, b_out):
    B, S, _ = x.shape
    Dh = HEAD_DIM
    proj = x @ w_in + b_in
    sizes = [SWA_Q_W, SWA_KV_W, SWA_KV_W, NSA_Q_W, 6 * NSA_KV_W, NSA_GATE_W]
    offs = []
    acc = 0
    for n in sizes[:-1]:
        acc += n
        offs.append(acc)
    qa, ka, va, qn, kvn, gn = jnp.split(proj, offs, axis=-1)
    slopes_a, slopes_n = alibi_slopes()
    ra = SWA_HEADS // SWA_KV_HEADS
    qa = qa.reshape(B, S, SWA_KV_HEADS, ra, Dh)
    ka = ka.reshape(B, S, SWA_KV_HEADS, Dh)
    va = va.reshape(B, S, SWA_KV_HEADS, Dh)
    o_a = banded_attention(qa, ka, va, slopes_a.reshape(SWA_KV_HEADS, ra), SWA_WINDOW,
                           sinks.reshape(SWA_KV_HEADS, ra))
    rn = NSA_HEADS // NSA_KV_GROUPS
    qn = qn.reshape(B, S, NSA_KV_GROUPS, rn, Dh)
    kvn = kvn.reshape(B, S, 6, NSA_KV_GROUPS, Dh)
    k_cmp = compress_blocks(kvn[:, :, 0], pe_k, w1_k, w2_k)
    v_cmp = compress_blocks(kvn[:, :, 1], pe_v, w1_v, w2_v)
    sl_n = slopes_n.reshape(NSA_KV_GROUPS, rn)
    o_cmp, o_slc = nsa_compressed_selected(qn, k_cmp, v_cmp, kvn[:, :, 2], kvn[:, :, 3], sl_n)
    o_win = banded_attention(qn, kvn[:, :, 4], kvn[:, :, 5], sl_n, NSA_WINDOW)
    g = jax.nn.sigmoid(gn).reshape(B, S, NSA_KV_GROUPS, rn, 3, 1)
    o_n = g[..., 0, :] * o_cmp + g[..., 1, :] * o_slc + g[..., 2, :] * o_win
    o = jnp.concatenate([o_a.reshape(B, S, SWA_Q_W), o_n.reshape(B, S, NSA_Q_W)], axis=-1)
    return o @ w_out + b_out


def hier_moe(x, w_group, b_group, w_expert, b_expert, we_gate, we_up, we_down):
    B, S, D = x.shape
    N = B * S
    T = EXPERT_ROWS
    xf = x.reshape(N, D)
    g_logits = (xf @ w_group + b_group).astype(jnp.float32)
    g_prob = jax.nn.softmax(g_logits, axis=-1)
    g_sel = jnp.argmax(g_logits, axis=-1)
    g_w = jnp.take_along_axis(g_prob, g_sel[:, None], axis=-1)
    e_logits = (xf @ w_expert + b_expert).astype(jnp.float32).reshape(N, N_GROUPS, EXPERTS_PER_GROUP)
    e_logits = jnp.take_along_axis(e_logits, g_sel[:, None, None], axis=1)[:, 0]
    top_v, top_local = lax.top_k(e_logits, EXPERT_TOPK)
    e_w = jax.nn.softmax(top_v, axis=-1) * g_w
    eid = (g_sel[:, None] * EXPERTS_PER_GROUP + top_local).reshape(-1)
    wts = e_w.reshape(-1)
    tok = jnp.repeat(jnp.arange(N, dtype=jnp.int32), EXPERT_TOPK)
    A = N * EXPERT_TOPK
    order = jnp.argsort(eid)
    eid_s, tok_s, w_s = eid[order], tok[order], wts[order]
    counts = jnp.bincount(eid, length=N_EXPERTS)
    padded = (counts + T - 1) // T * T
    start = jnp.cumsum(counts) - counts
    pend = jnp.cumsum(padded)
    pstart = pend - padded
    dest = pstart[eid_s] + (jnp.arange(A) - start[eid_s])
    P = A + N_EXPERTS * T
    nblk = P // T
    row_tok = jnp.full((P,), N, dtype=jnp.int32).at[dest].set(tok_s)
    x_pad = jnp.concatenate([xf, jnp.zeros((1, D), xf.dtype)], axis=0)
    xb = x_pad[row_tok].reshape(nblk, T, D)
    blk_e = jnp.minimum(jnp.searchsorted(pend, jnp.arange(nblk) * T, side='right'), N_EXPERTS - 1)

    def run(args):
        e, xr = args
        h = jax.nn.silu(xr @ we_gate[e]) * (xr @ we_up[e])
        return h @ we_down[e]

    yb = lax.map(run, (blk_e, xb)).reshape(P, D)
    y = jax.ops.segment_sum(yb[dest] * w_s[:, None].astype(yb.dtype), tok_s, num_segments=N)
    return y.reshape(B, S, D)


def setup_inputs(seed: int = 0) -> dict:
    key = jax.random.key(seed)
    ks = jax.random.split(key, 24)
    D, L, Dh = D_MODEL, DEPTH, HEAD_DIM
    nrm = jax.random.normal
    segs = [(SWA_Q_W, 1.0), (SWA_KV_W, 1.0), (SWA_KV_W, BETA), (NSA_Q_W, 1.0)]
    segs += [(NSA_KV_W, 1.0 if i % 2 == 0 else BETA) for i in range(6)]
    segs += [(NSA_GATE_W, 1.0)]
    col_scale = jnp.asarray(np.concatenate([np.full(n, s, np.float32) for n, s in segs]))
    flat_in = NSA_CMP_LEN * Dh
    return {
        'x': nrm(ks[0], (BATCH, SEQ, D), jnp.float32),
        'w_in': nrm(ks[1], (L, D, IN_COLS), jnp.float32) * D ** -0.5 * col_scale,
        'b_in': nrm(ks[2], (L, IN_COLS), jnp.float32) * 0.02,
        'swa_sinks': nrm(ks[3], (L, SWA_HEADS), jnp.float32) * 0.5,
        'cmp_pe_k': nrm(ks[4], (L, NSA_CMP_LEN, Dh), jnp.float32) * 0.5,
        'cmp_w1_k': nrm(ks[5], (L, flat_in, NSA_CMP_HIDDEN), jnp.float32) * flat_in ** -0.5,
        'cmp_w2_k': nrm(ks[6], (L, NSA_CMP_HIDDEN, Dh), jnp.float32) * NSA_CMP_HIDDEN ** -0.5,
        'cmp_pe_v': nrm(ks[7], (L, NSA_CMP_LEN, Dh), jnp.float32) * 0.5,
        'cmp_w1_v': nrm(ks[8], (L, flat_in, NSA_CMP_HIDDEN), jnp.float32) * flat_in ** -0.5,
        'cmp_w2_v': nrm(ks[9], (L, NSA_CMP_HIDDEN, Dh), jnp.float32) * NSA_CMP_HIDDEN ** -0.5,
        'w_out': nrm(ks[10], (L, MIX_WIDTH, D), jnp.float32) * MIX_WIDTH ** -0.5 * BETA,
        'b_out': nrm(ks[11], (L, D), jnp.float32) * 0.02,
        'ln1_g': 1.0 + 0.05 * nrm(ks[12], (L, D), jnp.float32),
        'ln1_b': 0.02 * nrm(ks[13], (L, D), jnp.float32),
        'w_group': nrm(ks[14], (L, D, N_GROUPS), jnp.float32) * D ** -0.5,
        'b_group': nrm(ks[15], (L, N_GROUPS), jnp.float32) * 0.01,
        'w_expert': nrm(ks[16], (L, D, N_EXPERTS), jnp.float32) * D ** -0.5,
        'b_expert': nrm(ks[17], (L, N_EXPERTS), jnp.float32) * 0.01,
        'we_gate': nrm(ks[18], (L, N_EXPERTS, D, EXPERT_HIDDEN), jnp.float32) * D ** -0.5,
        'we_up': nrm(ks[19], (L, N_EXPERTS, D, EXPERT_HIDDEN), jnp.float32) * D ** -0.5,
        'we_down': nrm(ks[20], (L, N_EXPERTS, EXPERT_HIDDEN, D), jnp.float32) * EXPERT_HIDDEN ** -0.5 * BETA,
        'ln2_g': 1.0 + 0.05 * nrm(ks[21], (L, D), jnp.float32),
        'ln2_b': 0.02 * nrm(ks[22], (L, D), jnp.float32),
    }


def reference(x, w_in, b_in, swa_sinks, cmp_pe_k, cmp_w1_k, cmp_w2_k, cmp_pe_v, cmp_w1_v,
              cmp_w2_v, w_out, b_out, ln1_g, ln1_b, w_group, b_group, w_expert, b_expert,
              we_gate, we_up, we_down, ln2_g, ln2_b):
    for l in range(DEPTH):
        mix = hybrid_mixer(x, w_in[l], b_in[l], swa_sinks[l], cmp_pe_k[l], cmp_w1_k[l], cmp_w2_k[l],
                           cmp_pe_v[l], cmp_w1_v[l], cmp_w2_v[l], w_out[l], b_out[l])
        x = layer_norm(ALPHA * x + mix, ln1_g[l], ln1_b[l])
        ffn = hier_moe(x, w_group[l], b_group[l], w_expert[l], b_expert[l],
                       we_gate[l], we_up[l], we_down[l])
        x = layer_norm(ALPHA * x + ffn, ln2_g[l], ln2_b[l])
    return x
```

```python
import functools
import math

import numpy as np
import jax
import jax.numpy as jnp
from jax import lax
from jax.experimental import pallas as pl
from jax.experimental.pallas import tpu as pltpu

F32 = jnp.float32
MXU_DT = jnp.bfloat16

HEAD_DIM = 64
HEADS_PER_GROUP = 8
PAIRS = HEADS_PER_GROUP // 2
N_KV_GROUPS = 2
SWA_WINDOW = 128
NSA_WINDOW = 512
NSA_CMP_LEN = 32
NSA_CMP_STRIDE = 16
NSA_CMP_HIDDEN = 256
NSA_SLC_LEN = 64
NSA_TOPK = 16
N_GROUPS = 8
EXPERTS_PER_GROUP = 8
N_EXPERTS = N_GROUPS * EXPERTS_PER_GROUP
EXPERT_HIDDEN = 512
LN_EPS = 1e-5

LANES = 128
Q_BLOCK = 128
BAND_CHUNK = 128
SLC_CHUNK = 256
EXPERT_ROWS = 256
ROW_TILE = 256
PROJ_TILE = 512
GATHER_ROWS = 256
NEG = -1e30
M_INIT = -5e29
VMEM_LIMIT = 56 * 1024 * 1024

FEAT_EVEN = 64
FEAT_ODD = 72


def _alibi_slopes(n_heads_total):
    n = n_heads_total
    s = np.exp2(-8.0 * np.arange(1, n + 1, dtype=np.float32) / np.float32(n)).astype(np.float32)
    return s[0::2], s[1::2]


def _bf16_parts(v):
    v = np.asarray(v, np.float32)
    a = v.astype(jnp.bfloat16).astype(np.float32)
    r = (v - a).astype(np.float32)
    b = r.astype(jnp.bfloat16).astype(np.float32)
    c = (r - b).astype(np.float32)
    return a, b, c


def _query_feats(slopes):
    out = np.zeros((N_KV_GROUPS, PAIRS, LANES), np.float32)
    a, b, c = _bf16_parts(slopes)
    for g in range(N_KV_GROUPS):
        for p in range(PAIRS):
            for h, base in ((0, FEAT_EVEN), (1, FEAT_ODD)):
                idx = g * HEADS_PER_GROUP + 2 * p + h
                out[g, p, base:base + 6] = [a[idx], b[idx], c[idx], a[idx], b[idx], c[idx]]
    return out


def _key_feats(positions, n_pad_chunks, chunk, with_blocks):
    pos = np.asarray(positions, np.int64)
    n = pos.shape[0]
    assert n % chunk == 0
    hi = (pos // 64 * 64).astype(np.float32)
    lo = (pos % 64).astype(np.float32)
    f = np.zeros((n, 2, LANES), np.float32)
    for v, base in ((0, FEAT_EVEN), (1, FEAT_ODD)):
        f[:, v, base:base + 3] = hi[:, None]
        f[:, v, base + 3:base + 6] = lo[:, None]
    if with_blocks:
        blk = pos // NSA_SLC_LEN
        for v in range(2):
            f[np.arange(n), v, blk] = 1.0
    f = f.reshape(n // chunk, chunk, 2, LANES).transpose(0, 2, 1, 3)
    if n_pad_chunks:
        f = np.concatenate([np.zeros((n_pad_chunks,) + f.shape[1:], np.float32), f], axis=0)
    return f


def _params(sem, vmem=VMEM_LIMIT):
    return pltpu.CompilerParams(dimension_semantics=sem, vmem_limit_bytes=vmem)


def _const_spec(shape):
    nd = len(shape)
    return pl.BlockSpec(shape, lambda *_: (0,) * nd, pipeline_mode=pl.Buffered(1))


def _dot(a, b):
    return jnp.dot(a, b, preferred_element_type=F32)


def _dot_nt(a, b):
    return lax.dot_general(a, b, (((1,), (1,)), ((), ())), preferred_element_type=F32)


def _layer_norm(h, g, b):
    mu = jnp.mean(h, axis=-1, keepdims=True)
    d = h - mu
    var = jnp.mean(d * d, axis=-1, keepdims=True)
    return d * lax.rsqrt(var + LN_EPS) * g + b


def _inproj_kernel(x_ref, w_ref, b_ref, qa_ref, qn_ref, kv_ref, gt_ref):
    xb = x_ref[...].astype(MXU_DT)
    width = qa_ref.shape[1]
    for j, o_ref in enumerate((qa_ref, qn_ref, kv_ref)):
        acc = _dot(xb, w_ref[:, j * width:(j + 1) * width]) + b_ref[:, j * width:(j + 1) * width]
        o_ref[...] = acc.astype(o_ref.dtype)
    gt_ref[...] = _dot(xb, w_ref[:, 3 * width:]) + b_ref[:, 3 * width:]


def _input_projection(x2d, w, b):
    n, d = x2d.shape
    width = (w.shape[1] - 2 * LANES) // 3
    tm = min(PROJ_TILE, n)
    row = lambda i: (i, 0)
    return pl.pallas_call(
        _inproj_kernel,
        grid=(n // tm,),
        in_specs=[pl.BlockSpec((tm, d), row), _const_spec(w.shape), _const_spec(b.shape)],
        out_specs=[pl.BlockSpec((tm, width), row)] * 3 + [pl.BlockSpec((tm, 2 * LANES), row)],
        out_shape=[jax.ShapeDtypeStruct((n, width), MXU_DT)] * 3
        + [jax.ShapeDtypeStruct((n, 2 * LANES), F32)],
        compiler_params=_params(("arbitrary",)),
        name="input_projection",
    )(x2d, w, b)


def _gelu_tanh(x):
    c = math.sqrt(2.0 / math.pi)
    return 0.5 * x * (1.0 + jnp.tanh(c * (x + 0.044715 * (x * x * x))))


def _compress_kernel(c_ref, pe_ref, w1_ref, w2_ref, o_ref):
    half = c_ref.shape[-1]
    w1 = w1_ref[0]
    pe = pe_ref[0]
    pe_hi = pe.astype(MXU_DT)
    pe_lo = (pe - pe_hi.astype(F32)).astype(MXU_DT)
    base = (_dot(pe_hi, w1) + _dot(pe_lo, w1))[0:1]
    for g in range(N_KV_GROUPS):
        c = c_ref[0, 0, g]
        top = _dot(c, w1[:half])
        bot = _dot(c, w1[half:])
        n_rows = bot.shape[0]
        hidden = top + pltpu.roll(bot, n_rows - 1, 0) + base
        act = _gelu_tanh(hidden)
        o_ref[0, 0, :, g * HEAD_DIM:(g + 1) * HEAD_DIM] = _dot(act.astype(MXU_DT), w2_ref[0])


def _compress(chunks, pe, w1, w2):
    b, two, g, nch, half = chunks.shape
    return pl.pallas_call(
        _compress_kernel,
        grid=(b, two),
        in_specs=[pl.BlockSpec((1, 1, g, nch, half), lambda i, j: (i, j, 0, 0, 0)),
                  pl.BlockSpec((1, 8, 2 * half), lambda i, j: (j, 0, 0)),
                  pl.BlockSpec((1, 2 * half, NSA_CMP_HIDDEN), lambda i, j: (j, 0, 0)),
                  pl.BlockSpec((1, NSA_CMP_HIDDEN, HEAD_DIM), lambda i, j: (j, 0, 0))],
        out_specs=pl.BlockSpec((1, 1, nch, LANES), lambda i, j: (i, j, 0, 0)),
        out_shape=jax.ShapeDtypeStruct((b, two, nch, LANES), F32),
        compiler_params=_params(("arbitrary", "arbitrary")),
        name="nsa_compress",
    )(chunks, pe, w1, w2)


def _place_group(x, g_is_zero):
    lane = lax.broadcasted_iota(jnp.int32, x.shape, 1)
    rolled = pltpu.roll(x, HEAD_DIM, 1)
    lo_src, hi_src = (x, rolled) if g_is_zero else (rolled, x)
    lo = jnp.where(lane < HEAD_DIM, lo_src, 0.0)
    hi = jnp.where(lane >= HEAD_DIM, hi_src, 0.0)
    return lo, hi


def _build_keys(g, k, v, kf_ref, kaug, ve, vo, n_pad, chunk, ones_lanes):
    n_keys = k.shape[0]
    n_chunks = n_keys // chunk
    lane = lax.broadcasted_iota(jnp.int32, (n_keys, LANES), 1)
    fill = 1.0 if ones_lanes else 0.0
    for gval in range(N_KV_GROUPS):
        @pl.when(g == gval)
        def _():
            klo, khi = _place_group(k, gval == 0)
            vlo, vhi = _place_group(v, gval == 0)
            kaug[n_pad:, 0, :, 0:LANES] = klo.astype(kaug.dtype).reshape(n_chunks, chunk, LANES)
            kaug[n_pad:, 1, :, 0:LANES] = khi.astype(kaug.dtype).reshape(n_chunks, chunk, LANES)
            ve[n_pad * chunk:, :] = jnp.where(lane < HEAD_DIM, vlo, fill).astype(ve.dtype)
            vo[n_pad * chunk:, :] = jnp.where(lane >= HEAD_DIM, vhi, fill).astype(vo.dtype)
    kaug[:, :, :, LANES:2 * LANES] = kf_ref[...]
    if n_pad:
        kaug[0:n_pad, :, :, 0:LANES] = jnp.zeros((n_pad, 2, chunk, LANES), kaug.dtype)
        ve[0:n_pad * chunk, :] = jnp.zeros((n_pad * chunk, LANES), ve.dtype)
        vo[0:n_pad * chunk, :] = jnp.zeros((n_pad * chunk, LANES), vo.dtype)


def _fill_queries(qaug, q_ref, feats):
    tq = q_ref.shape[1]
    for p in range(PAIRS):
        qaug[p * tq:(p + 1) * tq, 0:LANES] = q_ref[0, :, p * LANES:(p + 1) * LANES]
        qaug[p * tq:(p + 1) * tq, LANES:2 * LANES] = feats[p].astype(qaug.dtype)


def _add_shared(x, bias):
    tq, w = bias.shape
    return (x.reshape(PAIRS, tq, w) + bias[None]).reshape(PAIRS * tq, w)


def _emit_output(o_ref, acc_e, acc_o, den_e, den_o, gate_ref, gate_col):
    tq = o_ref.shape[1]
    lane = lax.broadcasted_iota(jnp.int32, (tq, LANES), 1)
    if gate_ref is not None:
        gsig = jax.nn.sigmoid(gate_ref[0])
    for p in range(PAIRS):
        rows = slice(p * tq, (p + 1) * tq)
        num = jnp.where(lane < HEAD_DIM, acc_e[rows], acc_o[rows])
        den = jnp.where(lane < HEAD_DIM, den_e[rows], den_o[rows])
        out = num / den
        if gate_ref is not None:
            ce = 3 * (2 * p) + gate_col
            co = 3 * (2 * p + 1) + gate_col
            out = out * jnp.where(lane < HEAD_DIM, gsig[:, ce:ce + 1], gsig[:, co:co + 1])
        o_ref[0, :, p * LANES:(p + 1) * LANES] = out.astype(o_ref.dtype)


def _banded_kernel(*refs, n_chunks, window, has_sink, gate_col):
    refs = list(refs)
    if has_sink:
        sink_ref, slope_ref = refs[0], refs[1]
        refs = refs[2:]
    q_ref, k_ref, v_ref, qf_ref, kf_ref = refs[:5]
    refs = refs[5:]
    gate_ref = None
    if gate_col is not None:
        gate_ref = refs[0]
        refs = refs[1:]
    o_ref, kaug, ve, vo, qaug = refs
    g = pl.program_id(1)
    i = pl.program_id(2)
    tq = q_ref.shape[1]
    ch = BAND_CHUNK
    n_pad = n_chunks - 1

    @pl.when(i == 0)
    def _():
        _build_keys(g, k_ref[0].astype(F32), v_ref[0].astype(F32), kf_ref, kaug, ve, vo,
                    n_pad, ch, ones_lanes=True)

    _fill_queries(qaug, q_ref, [jnp.broadcast_to(qf_ref[0, p:p + 1, :], (tq, LANES)) for p in range(PAIRS)])
    kc = kaug[pl.ds(i, n_chunks)].reshape(n_chunks * 2 * ch, 2 * LANES)
    s = _dot_nt(qaug[...], kc)

    row = lax.broadcasted_iota(jnp.int32, (tq, ch), 0)
    col = lax.broadcasted_iota(jnp.int32, (tq, ch), 1)
    t = i * tq + row
    ev, od = [], []
    for u in range(n_chunks):
        spos = (i - n_pad + u) * ch + col
        dist = t - spos
        vis = jnp.where(dist >= 0, jnp.where(dist < window, jnp.where(spos >= 0, 1.0, 0.0), 0.0), 0.0)
        bias = jnp.where(vis > 0.5, 0.0, NEG)
        ev.append(_add_shared(s[:, (2 * u) * ch:(2 * u + 1) * ch], bias))
        od.append(_add_shared(s[:, (2 * u + 1) * ch:(2 * u + 2) * ch], bias))

    tcol = (i * tq + lax.broadcasted_iota(jnp.int32, (tq, 1), 0)).astype(F32)
    res = []
    for h, slabs in enumerate((ev, od)):
        m = slabs[0].max(axis=1, keepdims=True)
        for x in slabs[1:]:
            m = jnp.maximum(m, x.max(axis=1, keepdims=True))
        if has_sink:
            sk = jnp.concatenate(
                [sink_ref[g * HEADS_PER_GROUP + 2 * p + h] + slope_ref[g * HEADS_PER_GROUP + 2 * p + h] * tcol
                 for p in range(PAIRS)], axis=0)
            m = jnp.maximum(m, sk)
        pmat = jnp.concatenate([jnp.exp(x - m) for x in slabs], axis=1).astype(ve.dtype)
        vsrc = ve if h == 0 else vo
        acc = _dot(pmat, vsrc[pl.ds(pl.multiple_of(i * ch, ch), n_chunks * ch), :])
        den = acc[:, HEAD_DIM:HEAD_DIM + 1] if h == 0 else acc[:, 0:1]
        if has_sink:
            den = den + jnp.exp(sk - m)
        res.append((acc, den))
    _emit_output(o_ref, res[0][0], res[1][0], res[0][1], res[1][1], gate_ref, gate_col)


def _banded_attention(q, kv, k_col, v_col, qfeat, kfeat, window, sinks=None, slopes=None,
                      gates=None, gate_col=None):
    b, s, _ = q.shape
    tq = Q_BLOCK
    n_chunks = window // BAND_CHUNK + 1
    n_tot = kfeat.shape[0]
    has_sink = sinks is not None
    kern = functools.partial(_banded_kernel, n_chunks=n_chunks, window=window,
                             has_sink=has_sink, gate_col=gate_col)
    in_specs, args = [], []
    if has_sink:
        in_specs += [pl.BlockSpec(memory_space=pltpu.SMEM)] * 2
        args += [sinks, slopes]
    in_specs += [pl.BlockSpec((1, tq, PAIRS * LANES), lambda bi, g, i: (bi, i, g)),
                 pl.BlockSpec((1, s, LANES), lambda bi, g, i: (bi, 0, k_col)),
                 pl.BlockSpec((1, s, LANES), lambda bi, g, i: (bi, 0, v_col)),
                 pl.BlockSpec((1, PAIRS, LANES), lambda bi, g, i: (g, 0, 0)),
                 _const_spec(kfeat.shape)]
    args += [q, kv, kv, qfeat, kfeat]
    if gate_col is not None:
        in_specs.append(pl.BlockSpec((1, tq, LANES), lambda bi, g, i: (bi, i, g)))
        args.append(gates)
    return pl.pallas_call(
        kern,
        grid=(b, N_KV_GROUPS, s // tq),
        in_specs=in_specs,
        out_specs=pl.BlockSpec((1, tq, PAIRS * LANES), lambda bi, g, i: (bi, i, g)),
        out_shape=jax.ShapeDtypeStruct(q.shape, MXU_DT),
        scratch_shapes=[pltpu.VMEM((n_tot, 2, BAND_CHUNK, 2 * LANES), MXU_DT),
                        pltpu.VMEM((n_tot * BAND_CHUNK, LANES), MXU_DT),
                        pltpu.VMEM((n_tot * BAND_CHUNK, LANES), MXU_DT),
                        pltpu.VMEM((PAIRS * tq, 2 * LANES), MXU_DT)],
        compiler_params=_params(("arbitrary", "arbitrary", "arbitrary")),
        name="banded_attention_w%d" % window,
    )(*args)


def _cmpsel_kernel(q_ref, k_ref, v_ref, qf_ref, kf_ref, ovt_ref, gate_ref, o_ref, mf_ref,
                   kaug, ve, vo, qaug):
    g = pl.program_id(1)
    i = pl.program_id(2)
    tq = q_ref.shape[1]
    n_cmp_pad = k_ref.shape[2]
    n_slc = NSA_SLC_LEN

    @pl.when(i == 0)
    def _():
        _build_keys(g, k_ref[0, 0], v_ref[0, 0], kf_ref, kaug, ve, vo, 0, n_cmp_pad, ones_lanes=False)

    _fill_queries(qaug, q_ref, [jnp.broadcast_to(qf_ref[0, p:p + 1, :], (tq, LANES)) for p in range(PAIRS)])
    s = _dot_nt(qaug[...], kaug[...].reshape(2 * n_cmp_pad, 2 * LANES))

    row = lax.broadcasted_iota(jnp.int32, (tq, n_cmp_pad), 0)
    col = lax.broadcasted_iota(jnp.int32, (tq, n_cmp_pad), 1)
    t = i * tq + row
    cmp_end = col * NSA_CMP_STRIDE + (NSA_CMP_LEN - 1)
    vis = jnp.where(t >= cmp_end, jnp.where(col < n_cmp_pad - 1, 1.0, 0.0), 0.0)
    bias = jnp.where(vis > 0.5, 0.0, NEG)

    pn = []
    for h in range(2):
        x = _add_shared(s[:, h * n_cmp_pad:(h + 1) * n_cmp_pad], bias)
        m = x.max(axis=1, keepdims=True)
        m = jnp.where(m > 0.5 * NEG, m, 0.0)
        e = jnp.exp(x - m)
        d = e.sum(axis=1, keepdims=True)
        pn.append(e * (1.0 / jnp.where(d > 0.0, d, 1.0)))
    acc = _dot(pn[0].astype(ve.dtype), ve[...]) + _dot(pn[1].astype(vo.dtype), vo[...])
    ones = jnp.ones((PAIRS * tq, 1), F32)
    _emit_output(o_ref, acc, acc, ones, ones, gate_ref, 0)

    psum = (pn[0] + pn[1]).reshape(PAIRS, tq, n_cmp_pad).sum(axis=0)
    p_hi = psum.astype(MXU_DT)
    p_lo = (psum - p_hi.astype(F32)).astype(MXU_DT)
    imp = _dot_nt(ovt_ref[...], p_hi) + _dot_nt(ovt_ref[...], p_lo)
    imp = imp[0:n_slc]
    blk = lax.broadcasted_iota(jnp.int32, (n_slc, tq), 0)
    tq_pos = i * tq + lax.broadcasted_iota(jnp.int32, (n_slc, tq), 1)
    cur = lax.shift_right_logical(tq_pos, int(math.log2(NSA_SLC_LEN)))
    forced = jnp.where(blk == 0, 1.0, jnp.where(blk == cur, 1.0, jnp.where(blk == cur - 1, 1.0, 0.0)))
    valid = blk <= cur
    score = jnp.where(valid, jnp.where(forced > 0.5, jnp.inf, imp), -jnp.inf)
    groups = [score[8 * v:8 * v + 8] for v in range(n_slc // 8)]
    ranks = [jnp.zeros((8, tq), F32) for _ in groups]
    sub = lax.broadcasted_iota(jnp.int32, (8, tq), 0)
    for j in range(n_slc):
        rj = score[j:j + 1, :]
        for v in range(n_slc // 8):
            if 8 * v > j:
                beats = jnp.where(rj >= groups[v], 1.0, 0.0)
            elif 8 * v + 7 < j:
                beats = jnp.where(rj > groups[v], 1.0, 0.0)
            else:
                beats = jnp.where(sub > (j - 8 * v), jnp.where(rj >= groups[v], 1.0, 0.0),
                                  jnp.where(rj > groups[v], 1.0, 0.0))
            ranks[v] = ranks[v] + beats
    rank = jnp.concatenate(ranks, axis=0)
    keep = jnp.where(valid, jnp.where(rank < float(NSA_TOPK), 1.0, 0.0), 0.0)
    feat = jnp.where(keep > 0.5, 0.0, NEG)
    feat = jnp.concatenate([feat, jnp.zeros((LANES - n_slc, tq), F32)], axis=0)
    mf_ref[0, 0] = feat.T.astype(mf_ref.dtype)


def _compressed_and_select(q, kvcmp, qfeat, kfeat, ovt, gates):
    b, s, _ = q.shape
    tq = Q_BLOCK
    n_cmp_pad = kvcmp.shape[2]
    return pl.pallas_call(
        _cmpsel_kernel,
        grid=(b, N_KV_GROUPS, s // tq),
        in_specs=[pl.BlockSpec((1, tq, PAIRS * LANES), lambda bi, g, i: (bi, i, g)),
                  pl.BlockSpec((1, 1, n_cmp_pad, LANES), lambda bi, g, i: (bi, 0, 0, 0)),
                  pl.BlockSpec((1, 1, n_cmp_pad, LANES), lambda bi, g, i: (bi, 1, 0, 0)),
                  pl.BlockSpec((1, PAIRS, LANES), lambda bi, g, i: (g, 0, 0)),
                  _const_spec(kfeat.shape), _const_spec(ovt.shape),
                  pl.BlockSpec((1, tq, LANES), lambda bi, g, i: (bi, i, g))],
        out_specs=[pl.BlockSpec((1, tq, PAIRS * LANES), lambda bi, g, i: (bi, i, g)),
                   pl.BlockSpec((1, 1, tq, LANES), lambda bi, g, i: (bi, g, i, 0))],
        out_shape=[jax.ShapeDtypeStruct(q.shape, MXU_DT),
                   jax.ShapeDtypeStruct((b, N_KV_GROUPS, s, LANES), MXU_DT)],
        scratch_shapes=[pltpu.VMEM((1, 2, n_cmp_pad, 2 * LANES), MXU_DT),
                        pltpu.VMEM((n_cmp_pad, LANES), MXU_DT),
                        pltpu.VMEM((n_cmp_pad, LANES), MXU_DT),
                        pltpu.VMEM((PAIRS * tq, 2 * LANES), MXU_DT)],
        compiler_params=_params(("arbitrary", "arbitrary", "arbitrary")),
        name="nsa_compressed_select",
    )(q, kvcmp, kvcmp, qfeat, kfeat, ovt, gates)


def _selected_kernel(q_ref, k_ref, v_ref, qf_ref, mf_ref, kf_ref, gate_ref, o_ref,
                     kaug, ve, vo, qaug, m_e, m_o, acc_e, acc_o):
    g = pl.program_id(1)
    i = pl.program_id(2)
    tq = q_ref.shape[1]
    ch = SLC_CHUNK
    rows = PAIRS * tq

    @pl.when(i == 0)
    def _():
        _build_keys(g, k_ref[0].astype(F32), v_ref[0].astype(F32), kf_ref, kaug, ve, vo,
                    0, ch, ones_lanes=True)

    mask_feat = mf_ref[0, 0].astype(F32)
    _fill_queries(qaug, q_ref, [mask_feat + qf_ref[0, p:p + 1, :] for p in range(PAIRS)])
    m_e[...] = jnp.full((rows, 1), M_INIT, F32)
    m_o[...] = jnp.full((rows, 1), M_INIT, F32)
    acc_e[...] = jnp.zeros((rows, LANES), F32)
    acc_o[...] = jnp.zeros((rows, LANES), F32)

    def update(x, m_ref, acc_ref, v):
        m_old = m_ref[...]
        m_new = jnp.maximum(m_old, x.max(axis=1, keepdims=True))
        alpha = jnp.exp(m_old - m_new)
        p = jnp.exp(x - m_new).astype(v.dtype)
        acc_ref[...] = alpha * acc_ref[...] + _dot(p, v)
        m_ref[...] = m_new

    def step(c, causal):
        s = _dot_nt(qaug[...], kaug[c].reshape(2 * ch, 2 * LANES))
        xe, xo = s[:, :ch], s[:, ch:]
        if causal:
            row = lax.broadcasted_iota(jnp.int32, (tq, ch), 0)
            col = lax.broadcasted_iota(jnp.int32, (tq, ch), 1)
            bias = jnp.where(c * ch + col <= i * tq + row, 0.0, NEG)
            xe, xo = _add_shared(xe, bias), _add_shared(xo, bias)
        start = pl.multiple_of(c * ch, ch)
        update(xe, m_e, acc_e, ve[pl.ds(start, ch), :])
        update(xo, m_o, acc_o, vo[pl.ds(start, ch), :])

    c_diag = (i * tq) // ch

    def body(c, carry):
        step(c, False)
        return carry

    lax.fori_loop(0, c_diag, body, 0)
    step(c_diag, True)
    ae, ao = acc_e[...], acc_o[...]
    _emit_output(o_ref, ae, ao, ae[:, HEAD_DIM:HEAD_DIM + 1], ao[:, 0:1], gate_ref, 1)


def _selected_attention(q, kv, k_col, v_col, qfeat, maskfeat, kfeat, gates):
    b, s, _ = q.shape
    tq = Q_BLOCK
    n_tot = kfeat.shape[0]
    return pl.pallas_call(
        _selected_kernel,
        grid=(b, N_KV_GROUPS, s // tq),
        in_specs=[pl.BlockSpec((1, tq, PAIRS * LANES), lambda bi, g, i: (bi, i, g)),
                  pl.BlockSpec((1, s, LANES), lambda bi, g, i: (bi, 0, k_col)),
                  pl.BlockSpec((1, s, LANES), lambda bi, g, i: (bi, 0, v_col)),
                  pl.BlockSpec((1, PAIRS, LANES), lambda bi, g, i: (g, 0, 0)),
                  pl.BlockSpec((1, 1, tq, LANES), lambda bi, g, i: (bi, g, i, 0)),
                  _const_spec(kfeat.shape),
                  pl.BlockSpec((1, tq, LANES), lambda bi, g, i: (bi, i, g))],
        out_specs=pl.BlockSpec((1, tq, PAIRS * LANES), lambda bi, g, i: (bi, i, g)),
        out_shape=jax.ShapeDtypeStruct(q.shape, MXU_DT),
        scratch_shapes=[pltpu.VMEM((n_tot, 2, SLC_CHUNK, 2 * LANES), MXU_DT),
                        pltpu.VMEM((n_tot * SLC_CHUNK, LANES), MXU_DT),
                        pltpu.VMEM((n_tot * SLC_CHUNK, LANES), MXU_DT),
                        pltpu.VMEM((PAIRS * tq, 2 * LANES), MXU_DT),
                        pltpu.VMEM((PAIRS * tq, 1), F32),
                        pltpu.VMEM((PAIRS * tq, 1), F32),
                        pltpu.VMEM((PAIRS * tq, LANES), F32),
                        pltpu.VMEM((PAIRS * tq, LANES), F32)],
        compiler_params=_params(("arbitrary", "arbitrary", "arbitrary")),
        name="nsa_selected",
    )(q, kv, kv, qfeat, maskfeat, kfeat, gates)


def _outproj_kernel(oa_ref, oc_ref, os_ref, ow_ref, x_ref, w_ref, b_ref, g_ref, beta_ref,
                    wr_ref, br_ref, x1_ref, route_ref, *, alpha):
    half = oa_ref.shape[1]
    on = (oc_ref[...].astype(F32) + os_ref[...].astype(F32) + ow_ref[...].astype(F32)).astype(MXU_DT)
    mix = _dot(oa_ref[...], w_ref[0:half]) + _dot(on, w_ref[half:]) + b_ref[...]
    x1 = _layer_norm(alpha * x_ref[...] + mix, g_ref[...], beta_ref[...])
    x1_ref[...] = x1

    logits = jnp.dot(x1, wr_ref[...], preferred_element_type=F32,
                     precision=lax.Precision.HIGHEST) + br_ref[...]
    lane = lax.broadcasted_iota(jnp.int32, logits.shape, 1)
    lanef = lane.astype(F32)
    big = float(4 * LANES)
    gl = jnp.where(lane < N_GROUPS, logits, -jnp.inf)
    gmax = gl.max(axis=1, keepdims=True)
    gsel = jnp.where(gl == gmax, lanef, big).min(axis=1, keepdims=True)
    g_w = 1.0 / jnp.exp(gl - gmax).sum(axis=1, keepdims=True)
    lo = N_GROUPS + gsel * EXPERTS_PER_GROUP
    el = jnp.where(lanef >= lo, jnp.where(lanef < lo + EXPERTS_PER_GROUP, logits, -jnp.inf), -jnp.inf)
    v1 = el.max(axis=1, keepdims=True)
    i1 = jnp.where(el == v1, lanef, big).min(axis=1, keepdims=True)
    el2 = jnp.where(lanef == i1, -jnp.inf, el)
    v2 = el2.max(axis=1, keepdims=True)
    i2 = jnp.where(el2 == v2, lanef, big).min(axis=1, keepdims=True)
    e2 = jnp.exp(v2 - v1)
    w1 = g_w / (1.0 + e2)
    w2 = g_w * e2 / (1.0 + e2)
    route_ref[...] = jnp.where(lane == 0, i1 - N_GROUPS,
                               jnp.where(lane == 1, i2 - N_GROUPS,
                                         jnp.where(lane == 2, w1, jnp.where(lane == 3, w2, 0.0))))


def _output_projection(oa, oc, os_, ow, x2d, w, b, ln_g, ln_b, wr, br, alpha):
    n, d = x2d.shape
    half = oa.shape[1]
    tm = min(ROW_TILE, n)
    row = lambda i: (i, 0)
    return pl.pallas_call(
        functools.partial(_outproj_kernel, alpha=alpha),
        grid=(n // tm,),
        in_specs=[pl.BlockSpec((tm, half), row)] * 4 + [pl.BlockSpec((tm, d), row)]
        + [_const_spec(a.shape) for a in (w, b, ln_g, ln_b, wr, br)],
        out_specs=[pl.BlockSpec((tm, d), row), pl.BlockSpec((tm, LANES), row)],
        out_shape=[jax.ShapeDtypeStruct((n, d), F32), jax.ShapeDtypeStruct((n, LANES), F32)],
        compiler_params=_params(("arbitrary",)),
        name="output_projection_ln_route",
    )(oa, oc, os_, ow, x2d, w, b, ln_g, ln_b, wr, br)


def _gather_kernel(idx_ref, src_ref, o_ref, sem):
    rows = o_ref.shape[0]
    base = pl.program_id(0) * rows

    def copy(r, src_row):
        return pltpu.make_async_copy(src_ref.at[pl.ds(src_row, 1)], o_ref.at[pl.ds(r, 1)], sem)

    def start(r, carry):
        copy(r, idx_ref[base + r]).start()
        return carry

    def wait(r, carry):
        copy(r, 0).wait()
        return carry

    lax.fori_loop(0, rows, start, 0)
    lax.fori_loop(0, rows, wait, 0)


def _gather_rows(src, idx):
    n_out = idx.shape[0]
    d = src.shape[1]
    rows = min(GATHER_ROWS, n_out)
    return pl.pallas_call(
        _gather_kernel,
        grid_spec=pltpu.PrefetchScalarGridSpec(
            num_scalar_prefetch=1,
            grid=(n_out // rows,),
            in_specs=[pl.BlockSpec(memory_space=pl.ANY)],
            out_specs=pl.BlockSpec((rows, d), lambda i, idx_ref: (i, 0)),
            scratch_shapes=[pltpu.SemaphoreType.DMA(())]),
        out_shape=jax.ShapeDtypeStruct((n_out, d), src.dtype),
        compiler_params=_params(("arbitrary",)),
        name="gather_rows",
    )(idx, src)


def _expert_kernel(blk_e_ref, n_used_ref, x_ref, wg_ref, wu_ref, wd_ref, o_ref):
    i = pl.program_id(0)

    @pl.when(i < n_used_ref[0])
    def _():
        xb = x_ref[...].astype(MXU_DT)
        gate = _dot(xb, wg_ref[0])
        up = _dot(xb, wu_ref[0])
        hidden = (gate * jax.nn.sigmoid(gate) * up).astype(MXU_DT)
        o_ref[...] = _dot(hidden, wd_ref[0])

    @pl.when(i >= n_used_ref[0])
    def _():
        o_ref[...] = jnp.zeros(o_ref.shape, o_ref.dtype)


def _expert_ffn(xs, blk_e, n_used, wg, wu, wd):
    p, d = xs.shape
    t = EXPERT_ROWS
    hid = wg.shape[2]
    wmap = lambda i, be, nu: (be[i], 0, 0)
    return pl.pallas_call(
        _expert_kernel,
        grid_spec=pltpu.PrefetchScalarGridSpec(
            num_scalar_prefetch=2,
            grid=(p // t,),
            in_specs=[pl.BlockSpec((t, d), lambda i, be, nu: (i, 0)),
                      pl.BlockSpec((1, d, hid), wmap),
                      pl.BlockSpec((1, d, hid), wmap),
                      pl.BlockSpec((1, hid, d), wmap)],
            out_specs=pl.BlockSpec((t, d), lambda i, be, nu: (i, 0))),
        out_shape=jax.ShapeDtypeStruct((p, d), F32),
        compiler_params=_params(("arbitrary",)),
        name="expert_ffn",
    )(blk_e, n_used, xs, wg, wu, wd)


def _combine_kernel(y_ref, route_ref, x_ref, g_ref, b_ref, o_ref, *, alpha):
    d = x_ref.shape[1]
    r = route_ref[...]
    y = r[:, 2:3] * y_ref[:, 0:d] + r[:, 3:4] * y_ref[:, d:2 * d]
    o_ref[...] = _layer_norm(alpha * x_ref[...] + y, g_ref[...], b_ref[...])


def _combine(y_pairs, route, x1, ln_g, ln_b, alpha):
    n, d = x1.shape
    tm = min(ROW_TILE, n)
    row = lambda i: (i, 0)
    return pl.pallas_call(
        functools.partial(_combine_kernel, alpha=alpha),
        grid=(n // tm,),
        in_specs=[pl.BlockSpec((tm, 2 * d), row), pl.BlockSpec((tm, LANES), row),
                  pl.BlockSpec((tm, d), row), _const_spec(ln_g.shape), _const_spec(ln_b.shape)],
        out_specs=pl.BlockSpec((tm, d), row),
        out_shape=jax.ShapeDtypeStruct((n, d), F32),
        compiler_params=_params(("arbitrary",)),
        name="combine_ln",
    )(y_pairs, route, x1, ln_g, ln_b)


def _dispatch_plan(route, n_tokens):
    t = EXPERT_ROWS
    a = 2 * n_tokens
    eid = route[:, 0:2].astype(jnp.int32).reshape(a)
    onehot = (eid[:, None] == jnp.arange(N_EXPERTS, dtype=jnp.int32)[None, :]).astype(jnp.int32)
    counts = onehot.sum(axis=0)
    before = jnp.cumsum(onehot, axis=0) - onehot
    rank = jnp.take_along_axis(before, eid[:, None], axis=1)[:, 0]
    padded = (counts + t - 1) // t * t
    pend = jnp.cumsum(padded)
    pstart = pend - padded
    dest = pstart[eid] + rank
    p_rows = a + N_EXPERTS * t
    row_tok = jnp.zeros((p_rows,), jnp.int32).at[dest].set(jnp.arange(a, dtype=jnp.int32) // 2)
    blk_start = jnp.arange(p_rows // t, dtype=jnp.int32) * t
    blk_e = jnp.minimum(jnp.searchsorted(pend, blk_start, side="right"), N_EXPERTS - 1).astype(jnp.int32)
    n_used = (pend[-1] // t).astype(jnp.int32).reshape(1)
    return dest.astype(jnp.int32), row_tok, blk_e, n_used


def kernel(x, w_in, b_in, swa_sinks, cmp_pe_k, cmp_w1_k, cmp_w2_k, cmp_pe_v, cmp_w1_v, cmp_w2_v,
           w_out, b_out, ln1_g, ln1_b, w_group, b_group, w_expert, b_expert, we_gate, we_up,
           we_down, ln2_g, ln2_b):
    bsz, seq, d_model = x.shape
    depth = w_in.shape[0]
    n_tok = bsz * seq
    n_heads = d_model // HEAD_DIM
    q_w = n_heads // 2 * HEAD_DIM
    kv_w = N_KV_GROUPS * HEAD_DIM
    alpha = (2.0 * depth) ** 0.25
    scale = HEAD_DIM ** -0.5
    assert q_w == N_KV_GROUPS * PAIRS * LANES and seq % SLC_CHUNK == 0

    slopes_a, slopes_n = _alibi_slopes(n_heads)
    qfeat_a = jnp.asarray(_query_feats(slopes_a), F32)
    qfeat_n = jnp.asarray(_query_feats(slopes_n), F32)
    pos = np.arange(seq)
    kfeat_swa = jnp.asarray(_key_feats(pos, SWA_WINDOW // BAND_CHUNK, BAND_CHUNK, False), MXU_DT)
    kfeat_win = jnp.asarray(_key_feats(pos, NSA_WINDOW // BAND_CHUNK, BAND_CHUNK, False), MXU_DT)
    kfeat_slc = jnp.asarray(_key_feats(pos, 0, SLC_CHUNK, True), MXU_DT)
    n_cmp = (seq - NSA_CMP_LEN) // NSA_CMP_STRIDE + 1
    n_cmp_pad = seq // NSA_CMP_STRIDE
    cmp_end = np.arange(n_cmp_pad) * NSA_CMP_STRIDE + NSA_CMP_LEN - 1
    kfeat_cmp = jnp.asarray(_key_feats(cmp_end, 0, n_cmp_pad, False), MXU_DT)
    n_slc = seq // NSA_SLC_LEN
    cs = np.arange(n_cmp_pad)[None, :] * NSA_CMP_STRIDE
    ss = np.arange(LANES)[:, None] * NSA_SLC_LEN
    ovt = ((cs < ss + NSA_SLC_LEN) & (cs + NSA_CMP_LEN - 1 >= ss)
           & (np.arange(n_cmp_pad)[None, :] < n_cmp) & (np.arange(LANES)[:, None] < n_slc))
    ovt = jnp.asarray(ovt.astype(np.float32), MXU_DT)
    slopes_a_s = jnp.asarray(slopes_a, F32)

    o_qa, o_ka, o_va = 0, q_w, q_w + kv_w
    o_qn = q_w + 2 * kv_w
    o_kvn = o_qn + q_w
    o_gn = o_kvn + 6 * kv_w
    gate_w = 3 * HEADS_PER_GROUP

    h = x.reshape(n_tok, d_model)
    for l in range(depth):
        wl, bl = w_in[l], b_in[l]

        def seg(a, lo, hi):
            return a[..., lo:hi]

        def gate_cols(a):
            pad = [(0, 0)] * (a.ndim - 1) + [(0, LANES - gate_w)]
            return jnp.concatenate([jnp.pad(seg(a, o_gn + g * gate_w, o_gn + (g + 1) * gate_w), pad)
                                    for g in range(N_KV_GROUPS)], axis=-1)

        def regroup(a):
            return jnp.concatenate([seg(a, o_qa, o_qa + q_w) * scale, seg(a, o_qn, o_qn + q_w) * scale,
                                    seg(a, o_ka, o_ka + 2 * kv_w), seg(a, o_kvn, o_kvn + 6 * kv_w),
                                    gate_cols(a)], axis=-1)

        w_r = regroup(wl).astype(MXU_DT)
        b_r = regroup(bl)[None, :]
        qa, qn, kv, gates = _input_projection(h, w_r, b_r)
        qa = qa.reshape(bsz, seq, q_w)
        qn = qn.reshape(bsz, seq, q_w)
        kv = kv.reshape(bsz, seq, 8 * kv_w)
        gates = gates.reshape(bsz, seq, 2 * LANES)

        o_a = _banded_attention(qa, kv, 0, 1, qfeat_a, kfeat_swa, SWA_WINDOW,
                                sinks=swa_sinks[l], slopes=slopes_a_s)

        kvc = kv[:, :, 2 * kv_w:4 * kv_w].reshape(bsz, n_cmp_pad, NSA_CMP_STRIDE, 2, N_KV_GROUPS, HEAD_DIM)
        kvc = kvc.transpose(0, 3, 4, 1, 2, 5).reshape(bsz, 2, N_KV_GROUPS, n_cmp_pad, NSA_CMP_STRIDE * HEAD_DIM)
        pe = jnp.stack([cmp_pe_k[l], cmp_pe_v[l]]).reshape(2, 1, NSA_CMP_LEN * HEAD_DIM)
        pe = jnp.broadcast_to(pe, (2, 8, NSA_CMP_LEN * HEAD_DIM))
        w1 = jnp.stack([cmp_w1_k[l], cmp_w1_v[l]]).astype(MXU_DT)
        w2 = jnp.stack([cmp_w2_k[l], cmp_w2_v[l]]).astype(MXU_DT)
        kvcmp = _compress(kvc, pe, w1, w2)

        o_cmp, maskfeat = _compressed_and_select(qn, kvcmp, qfeat_n, kfeat_cmp, ovt, gates)
        o_slc = _selected_attention(qn, kv, 4, 5, qfeat_n, maskfeat, kfeat_slc, gates)
        o_win = _banded_attention(qn, kv, 6, 7, qfeat_n, kfeat_win, NSA_WINDOW,
                                  gates=gates, gate_col=2)

        wr = jnp.concatenate([w_group[l], w_expert[l],
                              jnp.zeros((d_model, LANES - N_GROUPS - N_EXPERTS), F32)], axis=1)
        br = jnp.concatenate([b_group[l], b_expert[l],
                              jnp.zeros((LANES - N_GROUPS - N_EXPERTS,), F32)])[None, :]
        x1, route = _output_projection(
            o_a.reshape(n_tok, q_w), o_cmp.reshape(n_tok, q_w), o_slc.reshape(n_tok, q_w),
            o_win.reshape(n_tok, q_w), h, w_out[l].astype(MXU_DT), b_out[l][None, :],
            ln1_g[l][None, :], ln1_b[l][None, :], wr, br, alpha)

        dest, row_tok, blk_e, n_used = _dispatch_plan(route, n_tok)
        xs = _gather_rows(x1, row_tok)
        ys = _expert_ffn(xs, blk_e, n_used, we_gate[l].astype(MXU_DT), we_up[l].astype(MXU_DT),
                         we_down[l].astype(MXU_DT))
        y_pairs = _gather_rows(ys, dest).reshape(n_tok, 2 * d_model)
        h = _combine(y_pairs, route, x1, ln2_g[l][None, :], ln2_b[l][None, :], alpha)
    return h.reshape(bsz, seq, d_model)
```

```python
import functools
import math

import numpy as np
import jax
import jax.numpy as jnp
from jax import lax
from jax.experimental import pallas as pl
from jax.experimental.pallas import tpu as pltpu

F32 = jnp.float32
MXU_DT = jnp.bfloat16

HEAD_DIM = 64
HEADS_PER_GROUP = 8
PAIRS = HEADS_PER_GROUP // 2
N_KV_GROUPS = 2
SWA_WINDOW = 128
NSA_WINDOW = 512
NSA_CMP_LEN = 32
NSA_CMP_STRIDE = 16
NSA_CMP_HIDDEN = 256
NSA_SLC_LEN = 64
NSA_TOPK = 16
N_GROUPS = 8
EXPERTS_PER_GROUP = 8
N_EXPERTS = N_GROUPS * EXPERTS_PER_GROUP
EXPERT_HIDDEN = 512
LN_EPS = 1e-5

LANES = 128
Q_BLOCK = 128
BAND_CHUNK = 128
SLC_CHUNK = 256
EXPERT_ROWS = 256
ROW_TILE = 256
PROJ_TILE = 512
GATHER_ROWS = 256
GATHER_UNROLL = 8
NEG = -1e30
M_INIT = -5e29
VMEM_LIMIT = 56 * 1024 * 1024

FEAT_EVEN = 64
FEAT_ODD = 72


def _alibi_slopes(n_heads_total):
    n = n_heads_total
    s = np.exp2(-8.0 * np.arange(1, n + 1, dtype=np.float32) / np.float32(n)).astype(np.float32)
    return s[0::2], s[1::2]


def _bf16_parts(v):
    v = np.asarray(v, np.float32)
    a = v.astype(jnp.bfloat16).astype(np.float32)
    r = (v - a).astype(np.float32)
    b = r.astype(jnp.bfloat16).astype(np.float32)
    c = (r - b).astype(np.float32)
    return a, b, c


def _query_feats(slopes):
    out = np.zeros((N_KV_GROUPS, PAIRS, LANES), np.float32)
    a, b, c = _bf16_parts(slopes)
    for g in range(N_KV_GROUPS):
        for p in range(PAIRS):
            for h, base in ((0, FEAT_EVEN), (1, FEAT_ODD)):
                idx = g * HEADS_PER_GROUP + 2 * p + h
                out[g, p, base:base + 6] = [a[idx], b[idx], c[idx], a[idx], b[idx], c[idx]]
    return out


def _key_feats(positions, n_pad_chunks, chunk, with_blocks):
    pos = np.asarray(positions, np.int64)
    n = pos.shape[0]
    assert n % chunk == 0
    hi = (pos // 64 * 64).astype(np.float32)
    lo = (pos % 64).astype(np.float32)
    f = np.zeros((n, 2, LANES), np.float32)
    for v, base in ((0, FEAT_EVEN), (1, FEAT_ODD)):
        f[:, v, base:base + 3] = hi[:, None]
        f[:, v, base + 3:base + 6] = lo[:, None]
    if with_blocks:
        blk = pos // NSA_SLC_LEN
        for v in range(2):
            f[np.arange(n), v, blk] = 1.0
    f = f.reshape(n // chunk, chunk, 2, LANES).transpose(0, 2, 1, 3)
    if n_pad_chunks:
        f = np.concatenate([np.zeros((n_pad_chunks,) + f.shape[1:], np.float32), f], axis=0)
    return f


def _params(sem, vmem=VMEM_LIMIT):
    return pltpu.CompilerParams(dimension_semantics=sem, vmem_limit_bytes=vmem)


def _const_spec(shape):
    nd = len(shape)
    return pl.BlockSpec(shape, lambda *_: (0,) * nd, pipeline_mode=pl.Buffered(1))


def _dot(a, b):
    return jnp.dot(a, b, preferred_element_type=F32)


def _dot_nt(a, b):
    return lax.dot_general(a, b, (((1,), (1,)), ((), ())), preferred_element_type=F32)


def _layer_norm(h, g, b):
    mu = jnp.mean(h, axis=-1, keepdims=True)
    d = h - mu
    var = jnp.mean(d * d, axis=-1, keepdims=True)
    return d * lax.rsqrt(var + LN_EPS) * g + b


def _inproj_kernel(x_ref, w_ref, b_ref, qa_ref, qn_ref, kv_ref, gt_ref):
    xb = x_ref[...].astype(MXU_DT)
    width = qa_ref.shape[1]
    for j, o_ref in enumerate((qa_ref, qn_ref, kv_ref)):
        acc = _dot(xb, w_ref[:, j * width:(j + 1) * width]) + b_ref[:, j * width:(j + 1) * width]
        o_ref[...] = acc.astype(o_ref.dtype)
    gt_ref[...] = _dot(xb, w_ref[:, 3 * width:]) + b_ref[:, 3 * width:]


def _input_projection(x2d, w, b):
    n, d = x2d.shape
    width = (w.shape[1] - 2 * LANES) // 3
    tm = min(PROJ_TILE, n)
    row = lambda i: (i, 0)
    return pl.pallas_call(
        _inproj_kernel,
        grid=(n // tm,),
        in_specs=[pl.BlockSpec((tm, d), row), _const_spec(w.shape), _const_spec(b.shape)],
        out_specs=[pl.BlockSpec((tm, width), row)] * 3 + [pl.BlockSpec((tm, 2 * LANES), row)],
        out_shape=[jax.ShapeDtypeStruct((n, width), MXU_DT)] * 3
        + [jax.ShapeDtypeStruct((n, 2 * LANES), F32)],
        compiler_params=_params(("arbitrary",)),
        name="input_projection",
    )(x2d, w, b)


def _gelu_tanh(x):
    c = math.sqrt(2.0 / math.pi)
    return 0.5 * x * (1.0 + jnp.tanh(c * (x + 0.044715 * (x * x * x))))


def _compress_kernel(c_ref, pe_ref, w1_ref, w2_ref, o_ref):
    half = c_ref.shape[-1]
    w1 = w1_ref[0]
    pe = pe_ref[0]
    pe_hi = pe.astype(MXU_DT)
    pe_lo = (pe - pe_hi.astype(F32)).astype(MXU_DT)
    base = (_dot(pe_hi, w1) + _dot(pe_lo, w1))[0:1]
    for g in range(N_KV_GROUPS):
        c = c_ref[0, 0, g]
        top = _dot(c, w1[:half])
        bot = _dot(c, w1[half:])
        n_rows = bot.shape[0]
        hidden = top + pltpu.roll(bot, n_rows - 1, 0) + base
        act = _gelu_tanh(hidden)
        o_ref[0, 0, :, g * HEAD_DIM:(g + 1) * HEAD_DIM] = _dot(act.astype(MXU_DT), w2_ref[0])


def _compress(chunks, pe, w1, w2):
    b, two, g, nch, half = chunks.shape
    return pl.pallas_call(
        _compress_kernel,
        grid=(b, two),
        in_specs=[pl.BlockSpec((1, 1, g, nch, half), lambda i, j: (i, j, 0, 0, 0)),
                  pl.BlockSpec((1, 8, 2 * half), lambda i, j: (j, 0, 0)),
                  pl.BlockSpec((1, 2 * half, NSA_CMP_HIDDEN), lambda i, j: (j, 0, 0)),
                  pl.BlockSpec((1, NSA_CMP_HIDDEN, HEAD_DIM), lambda i, j: (j, 0, 0))],
        out_specs=pl.BlockSpec((1, 1, nch, LANES), lambda i, j: (i, j, 0, 0)),
        out_shape=jax.ShapeDtypeStruct((b, two, nch, LANES), F32),
        compiler_params=_params(("arbitrary", "arbitrary")),
        name="nsa_compress",
    )(chunks, pe, w1, w2)


def _place_group(x, g_is_zero):
    lane = lax.broadcasted_iota(jnp.int32, x.shape, 1)
    rolled = pltpu.roll(x, HEAD_DIM, 1)
    lo_src, hi_src = (x, rolled) if g_is_zero else (rolled, x)
    lo = jnp.where(lane < HEAD_DIM, lo_src, 0.0)
    hi = jnp.where(lane >= HEAD_DIM, hi_src, 0.0)
    return lo, hi


def _build_keys(g, load_k, load_v, n_chunks, kf_ref, kaug, vte, vto, n_pad, chunk, ones_rows):
    fill = 1.0 if ones_rows else 0.0
    lane = lax.broadcasted_iota(jnp.int32, (chunk, LANES), 1)
    for gval in range(N_KV_GROUPS):
        @pl.when(g == gval)
        def _():
            def body(c, carry):
                start = pl.multiple_of(c * chunk, chunk)
                klo, khi = _place_group(load_k(start), gval == 0)
                vlo, vhi = _place_group(load_v(start), gval == 0)
                kaug[n_pad + c, 0, :, 0:LANES] = klo.astype(kaug.dtype)
                kaug[n_pad + c, 1, :, 0:LANES] = khi.astype(kaug.dtype)
                vte[n_pad + c] = jnp.where(lane < HEAD_DIM, vlo, fill).T.astype(vte.dtype)
                vto[n_pad + c] = jnp.where(lane >= HEAD_DIM, vhi, fill).T.astype(vto.dtype)
                return carry
            lax.fori_loop(0, n_chunks, body, 0)
    kaug[:, :, :, LANES:2 * LANES] = kf_ref[...]
    if n_pad:
        kaug[0:n_pad, :, :, 0:LANES] = jnp.zeros((n_pad, 2, chunk, LANES), kaug.dtype)
        vte[0:n_pad] = jnp.zeros((n_pad, LANES, chunk), vte.dtype)
        vto[0:n_pad] = jnp.zeros((n_pad, LANES, chunk), vto.dtype)


def _fill_queries(qaug, q_ref, feats):
    tq = q_ref.shape[1]
    for p in range(PAIRS):
        qaug[p * tq:(p + 1) * tq, 0:LANES] = q_ref[0, :, p * LANES:(p + 1) * LANES]
        qaug[p * tq:(p + 1) * tq, LANES:2 * LANES] = feats[p].astype(qaug.dtype)


def _tile_pairs(x):
    return jnp.concatenate([x] * PAIRS, axis=1)


def _finish(o_ref, acct_e, acct_o, gate_ref, gate_col):
    tq = o_ref.shape[1]
    lane = lax.broadcasted_iota(jnp.int32, (tq, LANES), 1)
    if gate_ref is not None:
        gsig = jax.nn.sigmoid(gate_ref[0])
    for p in range(PAIRS):
        cols = slice(p * tq, (p + 1) * tq)
        ae = acct_e[:, cols].T
        if acct_o is None:
            out = ae
        else:
            ao = acct_o[:, cols].T
            num = jnp.where(lane < HEAD_DIM, ae, ao)
            den = jnp.where(lane < HEAD_DIM, ae[:, HEAD_DIM:HEAD_DIM + 1], ao[:, 0:1])
            out = num / den
        if gate_ref is not None:
            ce = 3 * (2 * p) + gate_col
            co = 3 * (2 * p + 1) + gate_col
            out = out * jnp.where(lane < HEAD_DIM, gsig[:, ce:ce + 1], gsig[:, co:co + 1])
        o_ref[0, :, p * LANES:(p + 1) * LANES] = out.astype(o_ref.dtype)


def _banded_kernel(*refs, n_chunks, window, has_sink, gate_col):
    refs = list(refs)
    if has_sink:
        sink_ref, slope_ref = refs[0], refs[1]
        refs = refs[2:]
    q_ref, k_ref, v_ref, qf_ref, kf_ref = refs[:5]
    refs = refs[5:]
    gate_ref = None
    if gate_col is not None:
        gate_ref = refs[0]
        refs = refs[1:]
    o_ref, kaug, vte, vto, qaug = refs
    g = pl.program_id(1)
    i = pl.program_id(2)
    tq = q_ref.shape[1]
    rows = PAIRS * tq
    ch = BAND_CHUNK
    n_pad = n_chunks - 1

    @pl.when(i == 0)
    def _():
        _build_keys(g, lambda s0: k_ref[0, pl.ds(s0, ch), :].astype(F32),
                    lambda s0: v_ref[0, pl.ds(s0, ch), :].astype(F32),
                    k_ref.shape[1] // ch, kf_ref, kaug, vte, vto, n_pad, ch, ones_rows=True)

    _fill_queries(qaug, q_ref, [jnp.broadcast_to(qf_ref[0, p:p + 1, :], (tq, LANES)) for p in range(PAIRS)])
    kc = kaug[pl.ds(i, n_chunks)].reshape(n_chunks * 2 * ch, 2 * LANES)
    st = _dot_nt(kc, qaug[...])

    krow = lax.broadcasted_iota(jnp.int32, (ch, tq), 0)
    qcol = lax.broadcasted_iota(jnp.int32, (ch, tq), 1)
    t = i * tq + qcol
    ev, od = [], []
    for u in range(n_chunks):
        spos = (i - n_pad + u) * ch + krow
        dist = t - spos
        vis = jnp.where(dist >= 0, jnp.where(dist < window, jnp.where(spos >= 0, 1.0, 0.0), 0.0), 0.0)
        bias = _tile_pairs(jnp.where(vis > 0.5, 0.0, NEG))
        ev.append(st[(2 * u) * ch:(2 * u + 1) * ch] + bias)
        od.append(st[(2 * u + 1) * ch:(2 * u + 2) * ch] + bias)

    tpos = (i * tq + lax.broadcasted_iota(jnp.int32, (1, tq), 1)).astype(F32)
    row128 = lax.broadcasted_iota(jnp.int32, (LANES, rows), 0)
    accs = []
    for h, slabs in enumerate((ev, od)):
        m = slabs[0].max(axis=0, keepdims=True)
        for x in slabs[1:]:
            m = jnp.maximum(m, x.max(axis=0, keepdims=True))
        if has_sink:
            sk = jnp.concatenate(
                [sink_ref[g * HEADS_PER_GROUP + 2 * p + h] + slope_ref[g * HEADS_PER_GROUP + 2 * p + h] * tpos
                 for p in range(PAIRS)], axis=1)
            m = jnp.maximum(m, sk)
        vt = vte if h == 0 else vto
        acc = None
        for u, x in enumerate(slabs):
            d = _dot(vt[i + u], jnp.exp(x - m).astype(vt.dtype))
            acc = d if acc is None else acc + d
        if has_sink:
            sum_rows = (row128 >= HEAD_DIM) if h == 0 else (row128 < HEAD_DIM)
            acc = acc + jnp.where(sum_rows, jnp.exp(sk - m), 0.0)
        accs.append(acc)
    _finish(o_ref, accs[0], accs[1], gate_ref, gate_col)


def _banded_attention(q, kv, k_col, v_col, qfeat, kfeat, window, sinks=None, slopes=None,
                      gates=None, gate_col=None):
    b, s, _ = q.shape
    tq = Q_BLOCK
    n_chunks = window // BAND_CHUNK + 1
    n_tot = kfeat.shape[0]
    has_sink = sinks is not None
    kern = functools.partial(_banded_kernel, n_chunks=n_chunks, window=window,
                             has_sink=has_sink, gate_col=gate_col)
    in_specs, args = [], []
    if has_sink:
        in_specs += [pl.BlockSpec(memory_space=pltpu.SMEM)] * 2
        args += [sinks, slopes]
    in_specs += [pl.BlockSpec((1, tq, PAIRS * LANES), lambda bi, g, i: (bi, i, g)),
                 pl.BlockSpec((1, s, LANES), lambda bi, g, i: (bi, 0, k_col)),
                 pl.BlockSpec((1, s, LANES), lambda bi, g, i: (bi, 0, v_col)),
                 pl.BlockSpec((1, PAIRS, LANES), lambda bi, g, i: (g, 0, 0)),
                 _const_spec(kfeat.shape)]
    args += [q, kv, kv, qfeat, kfeat]
    if gate_col is not None:
        in_specs.append(pl.BlockSpec((1, tq, LANES), lambda bi, g, i: (bi, i, g)))
        args.append(gates)
    return pl.pallas_call(
        kern,
        grid=(b, N_KV_GROUPS, s // tq),
        in_specs=in_specs,
        out_specs=pl.BlockSpec((1, tq, PAIRS * LANES), lambda bi, g, i: (bi, i, g)),
        out_shape=jax.ShapeDtypeStruct(q.shape, MXU_DT),
        scratch_shapes=[pltpu.VMEM((n_tot, 2, BAND_CHUNK, 2 * LANES), MXU_DT),
                        pltpu.VMEM((n_tot, LANES, BAND_CHUNK), MXU_DT),
                        pltpu.VMEM((n_tot, LANES, BAND_CHUNK), MXU_DT),
                        pltpu.VMEM((PAIRS * tq, 2 * LANES), MXU_DT)],
        compiler_params=_params(("arbitrary", "arbitrary", "arbitrary")),
        name="banded_attention_w%d" % window,
    )(*args)


def _cmpsel_kernel(q_ref, k_ref, v_ref, qf_ref, kf_ref, ovt_ref, gate_ref, o_ref, mf_ref,
                   kaug, vte, vto, qaug):
    g = pl.program_id(1)
    i = pl.program_id(2)
    tq = q_ref.shape[1]
    ncp = k_ref.shape[2]
    n_slc = NSA_SLC_LEN

    @pl.when(i == 0)
    def _():
        _build_keys(g, lambda s0: k_ref[0, 0], lambda s0: v_ref[0, 0], 1, kf_ref, kaug, vte, vto,
                    0, ncp, ones_rows=False)

    _fill_queries(qaug, q_ref, [jnp.broadcast_to(qf_ref[0, p:p + 1, :], (tq, LANES)) for p in range(PAIRS)])
    st = _dot_nt(kaug[0].reshape(2 * ncp, 2 * LANES), qaug[...])

    krow = lax.broadcasted_iota(jnp.int32, (ncp, tq), 0)
    qcol = lax.broadcasted_iota(jnp.int32, (ncp, tq), 1)
    t = i * tq + qcol
    cmp_end = krow * NSA_CMP_STRIDE + (NSA_CMP_LEN - 1)
    vis = jnp.where(t >= cmp_end, jnp.where(krow < ncp - 1, 1.0, 0.0), 0.0)
    bias = _tile_pairs(jnp.where(vis > 0.5, 0.0, NEG))

    pn = []
    for h in range(2):
        x = st[h * ncp:(h + 1) * ncp] + bias
        m = x.max(axis=0, keepdims=True)
        m = jnp.where(m > 0.5 * NEG, m, 0.0)
        e = jnp.exp(x - m)
        d = e.sum(axis=0, keepdims=True)
        pn.append(e * (1.0 / jnp.where(d > 0.0, d, 1.0)))
    acct = _dot(vte[0], pn[0].astype(vte.dtype)) + _dot(vto[0], pn[1].astype(vto.dtype))
    _finish(o_ref, acct, None, gate_ref, 0)

    ps = pn[0] + pn[1]
    psum = ps[:, 0:tq]
    for p in range(1, PAIRS):
        psum = psum + ps[:, p * tq:(p + 1) * tq]
    p_hi = psum.astype(MXU_DT)
    p_lo = (psum - p_hi.astype(F32)).astype(MXU_DT)
    imp = (_dot(ovt_ref[...], p_hi) + _dot(ovt_ref[...], p_lo))[0:n_slc]
    blk = lax.broadcasted_iota(jnp.int32, (n_slc, tq), 0)
    tq_pos = i * tq + lax.broadcasted_iota(jnp.int32, (n_slc, tq), 1)
    cur = lax.shift_right_logical(tq_pos, int(math.log2(NSA_SLC_LEN)))
    forced = jnp.where(blk == 0, 1.0, jnp.where(blk == cur, 1.0, jnp.where(blk == cur - 1, 1.0, 0.0)))
    valid = blk <= cur
    score = jnp.where(valid, jnp.where(forced > 0.5, jnp.inf, imp), -jnp.inf)
    groups = [score[8 * v:8 * v + 8] for v in range(n_slc // 8)]
    ranks = [jnp.zeros((8, tq), F32) for _ in groups]
    sub = lax.broadcasted_iota(jnp.int32, (8, tq), 0)
    for j in range(n_slc):
        rj = score[j:j + 1, :]
        for v in range(n_slc // 8):
            if 8 * v > j:
                beats = jnp.where(rj >= groups[v], 1.0, 0.0)
            elif 8 * v + 7 < j:
                beats = jnp.where(rj > groups[v], 1.0, 0.0)
            else:
                beats = jnp.where(sub > (j - 8 * v), jnp.where(rj >= groups[v], 1.0, 0.0),
                                  jnp.where(rj > groups[v], 1.0, 0.0))
            ranks[v] = ranks[v] + beats
    rank = jnp.concatenate(ranks, axis=0)
    keep = jnp.where(valid, jnp.where(rank < float(NSA_TOPK), 1.0, 0.0), 0.0)
    feat = jnp.where(keep > 0.5, 0.0, NEG)
    feat = jnp.concatenate([feat, jnp.zeros((LANES - n_slc, tq), F32)], axis=0)
    mf_ref[0, 0] = feat.T.astype(mf_ref.dtype)


def _compressed_and_select(q, kvcmp, qfeat, kfeat, ovt, gates):
    b, s, _ = q.shape
    tq = Q_BLOCK
    ncp = kvcmp.shape[2]
    return pl.pallas_call(
        _cmpsel_kernel,
        grid=(b, N_KV_GROUPS, s // tq),
        in_specs=[pl.BlockSpec((1, tq, PAIRS * LANES), lambda bi, g, i: (bi, i, g)),
                  pl.BlockSpec((1, 1, ncp, LANES), lambda bi, g, i: (bi, 0, 0, 0)),
                  pl.BlockSpec((1, 1, ncp, LANES), lambda bi, g, i: (bi, 1, 0, 0)),
                  pl.BlockSpec((1, PAIRS, LANES), lambda bi, g, i: (g, 0, 0)),
                  _const_spec(kfeat.shape), _const_spec(ovt.shape),
                  pl.BlockSpec((1, tq, LANES), lambda bi, g, i: (bi, i, g))],
        out_specs=[pl.BlockSpec((1, tq, PAIRS * LANES), lambda bi, g, i: (bi, i, g)),
                   pl.BlockSpec((1, 1, tq, LANES), lambda bi, g, i: (bi, g, i, 0))],
        out_shape=[jax.ShapeDtypeStruct(q.shape, MXU_DT),
                   jax.ShapeDtypeStruct((b, N_KV_GROUPS, s, LANES), MXU_DT)],
        scratch_shapes=[pltpu.VMEM((1, 2, ncp, 2 * LANES), MXU_DT),
                        pltpu.VMEM((1, LANES, ncp), MXU_DT),
                        pltpu.VMEM((1, LANES, ncp), MXU_DT),
                        pltpu.VMEM((PAIRS * tq, 2 * LANES), MXU_DT)],
        compiler_params=_params(("arbitrary", "arbitrary", "arbitrary")),
        name="nsa_compressed_select",
    )(q, kvcmp, kvcmp, qfeat, kfeat, ovt, gates)


def _selected_kernel(q_ref, k_ref, v_ref, qf_ref, mf_ref, kf_ref, gate_ref, o_ref,
                     kaug, vte, vto, qaug, acc_e, acc_o):
    g = pl.program_id(1)
    i = pl.program_id(2)
    tq = q_ref.shape[1]
    ch = SLC_CHUNK
    rows = PAIRS * tq

    @pl.when(i == 0)
    def _():
        _build_keys(g, lambda s0: k_ref[0, pl.ds(s0, ch), :].astype(F32),
                    lambda s0: v_ref[0, pl.ds(s0, ch), :].astype(F32),
                    k_ref.shape[1] // ch, kf_ref, kaug, vte, vto, 0, ch, ones_rows=True)

    mask_feat = mf_ref[0, 0].astype(F32)
    _fill_queries(qaug, q_ref, [mask_feat + qf_ref[0, p:p + 1, :] for p in range(PAIRS)])
    acc_e[...] = jnp.zeros((LANES, rows), F32)
    acc_o[...] = jnp.zeros((LANES, rows), F32)

    def update(x, m_old, acc_ref, vt_c):
        m_new = jnp.maximum(m_old, x.max(axis=0, keepdims=True))
        alpha = jnp.exp(m_old - m_new)
        p = jnp.exp(x - m_new).astype(vt_c.dtype)
        acc_ref[...] = alpha * acc_ref[...] + _dot(vt_c, p)
        return m_new

    def scores(c):
        return _dot_nt(kaug[c].reshape(2 * ch, 2 * LANES), qaug[...])

    def consume(st, c, m_e, m_o, causal):
        xe, xo = st[:ch], st[ch:]
        if causal:
            krow = lax.broadcasted_iota(jnp.int32, (ch, tq), 0)
            qcol = lax.broadcasted_iota(jnp.int32, (ch, tq), 1)
            bias = _tile_pairs(jnp.where(c * ch + krow <= i * tq + qcol, 0.0, NEG))
            xe, xo = xe + bias, xo + bias
        return update(xe, m_e, acc_e, vte[c]), update(xo, m_o, acc_o, vto[c])

    def body(c, carry):
        st, m_e, m_o = carry
        st_next = scores(c + 1)
        m_e, m_o = consume(st, c, m_e, m_o, False)
        return st_next, m_e, m_o

    c_diag = (i * tq) // ch
    m0 = jnp.full((1, rows), M_INIT, F32)
    st, m_e, m_o = lax.fori_loop(0, c_diag, body, (scores(0), m0, m0))
    consume(st, c_diag, m_e, m_o, True)
    _finish(o_ref, acc_e[...], acc_o[...], gate_ref, 1)


def _selected_attention(q, kv, k_col, v_col, qfeat, maskfeat, kfeat, gates):
    b, s, _ = q.shape
    tq = Q_BLOCK
    n_tot = kfeat.shape[0]
    return pl.pallas_call(
        _selected_kernel,
        grid=(b, N_KV_GROUPS, s // tq),
        in_specs=[pl.BlockSpec((1, tq, PAIRS * LANES), lambda bi, g, i: (bi, i, g)),
                  pl.BlockSpec((1, s, LANES), lambda bi, g, i: (bi, 0, k_col)),
                  pl.BlockSpec((1, s, LANES), lambda bi, g, i: (bi, 0, v_col)),
                  pl.BlockSpec((1, PAIRS, LANES), lambda bi, g, i: (g, 0, 0)),
                  pl.BlockSpec((1, 1, tq, LANES), lambda bi, g, i: (bi, g, i, 0)),
                  _const_spec(kfeat.shape),
                  pl.BlockSpec((1, tq, LANES), lambda bi, g, i: (bi, i, g))],
        out_specs=pl.BlockSpec((1, tq, PAIRS * LANES), lambda bi, g, i: (bi, i, g)),
        out_shape=jax.ShapeDtypeStruct(q.shape, MXU_DT),
        scratch_shapes=[pltpu.VMEM((n_tot, 2, SLC_CHUNK, 2 * LANES), MXU_DT),
                        pltpu.VMEM((n_tot, LANES, SLC_CHUNK), MXU_DT),
                        pltpu.VMEM((n_tot, LANES, SLC_CHUNK), MXU_DT),
                        pltpu.VMEM((PAIRS * tq, 2 * LANES), MXU_DT),
                        pltpu.VMEM((LANES, PAIRS * tq), F32),
                        pltpu.VMEM((LANES, PAIRS * tq), F32)],
        compiler_params=_params(("arbitrary", "arbitrary", "arbitrary")),
        name="nsa_selected",
    )(q, kv, kv, qfeat, maskfeat, kfeat, gates)


def _outproj_kernel(oa_ref, oc_ref, os_ref, ow_ref, x_ref, w_ref, b_ref, g_ref, beta_ref,
                    wr_ref, br_ref, x1_ref, route_ref, *, alpha):
    half = oa_ref.shape[1]
    on = (oc_ref[...].astype(F32) + os_ref[...].astype(F32) + ow_ref[...].astype(F32)).astype(MXU_DT)
    mix = _dot(oa_ref[...], w_ref[0:half]) + _dot(on, w_ref[half:]) + b_ref[...]
    x1 = _layer_norm(alpha * x_ref[...] + mix, g_ref[...], beta_ref[...])
    x1_ref[...] = x1

    logits = jnp.dot(x1, wr_ref[...], preferred_element_type=F32,
                     precision=lax.Precision.HIGHEST) + br_ref[...]
    lane = lax.broadcasted_iota(jnp.int32, logits.shape, 1)
    lanef = lane.astype(F32)
    big = float(4 * LANES)
    gl = jnp.where(lane < N_GROUPS, logits, -jnp.inf)
    gmax = gl.max(axis=1, keepdims=True)
    gsel = jnp.where(gl == gmax, lanef, big).min(axis=1, keepdims=True)
    g_w = 1.0 / jnp.exp(gl - gmax).sum(axis=1, keepdims=True)
    lo = N_GROUPS + gsel * EXPERTS_PER_GROUP
    el = jnp.where(lanef >= lo, jnp.where(lanef < lo + EXPERTS_PER_GROUP, logits, -jnp.inf), -jnp.inf)
    v1 = el.max(axis=1, keepdims=True)
    i1 = jnp.where(el == v1, lanef, big).min(axis=1, keepdims=True)
    el2 = jnp.where(lanef == i1, -jnp.inf, el)
    v2 = el2.max(axis=1, keepdims=True)
    i2 = jnp.where(el2 == v2, lanef, big).min(axis=1, keepdims=True)
    e2 = jnp.exp(v2 - v1)
    w1 = g_w / (1.0 + e2)
    w2 = g_w * e2 / (1.0 + e2)
    route_ref[...] = jnp.where(lane == 0, i1 - N_GROUPS,
                               jnp.where(lane == 1, i2 - N_GROUPS,
                                         jnp.where(lane == 2, w1, jnp.where(lane == 3, w2, 0.0))))


def _output_projection(oa, oc, os_, ow, x2d, w, b, ln_g, ln_b, wr, br, alpha):
    n, d = x2d.shape
    half = oa.shape[1]
    tm = min(ROW_TILE, n)
    row = lambda i: (i, 0)
    return pl.pallas_call(
        functools.partial(_outproj_kernel, alpha=alpha),
        grid=(n // tm,),
        in_specs=[pl.BlockSpec((tm, half), row)] * 4 + [pl.BlockSpec((tm, d), row)]
        + [_const_spec(a.shape) for a in (w, b, ln_g, ln_b, wr, br)],
        out_specs=[pl.BlockSpec((tm, d), row), pl.BlockSpec((tm, LANES), row)],
        out_shape=[jax.ShapeDtypeStruct((n, d), F32), jax.ShapeDtypeStruct((n, LANES), F32)],
        compiler_params=_params(("arbitrary",)),
        name="output_projection_ln_route",
    )(oa, oc, os_, ow, x2d, w, b, ln_g, ln_b, wr, br)


def _gather_kernel(idx_ref, src_ref, o_ref, sem):
    rows = o_ref.shape[0]
    base = pl.program_id(0) * rows

    def copy(r, src_row):
        return pltpu.make_async_copy(src_ref.at[pl.ds(src_row, 1)], o_ref.at[pl.ds(r, 1)], sem)

    def start(r, carry):
        copy(r, idx_ref[base + r]).start()
        return carry

    def wait(r, carry):
        copy(r, 0).wait()
        return carry

    lax.fori_loop(0, rows, start, 0, unroll=GATHER_UNROLL)
    lax.fori_loop(0, rows, wait, 0, unroll=GATHER_UNROLL)


def _gather_rows(src, idx):
    n_out = idx.shape[0]
    d = src.shape[1]
    rows = min(GATHER_ROWS, n_out)
    return pl.pallas_call(
        _gather_kernel,
        grid_spec=pltpu.PrefetchScalarGridSpec(
            num_scalar_prefetch=1,
            grid=(n_out // rows,),
            in_specs=[pl.BlockSpec(memory_space=pl.ANY)],
            out_specs=pl.BlockSpec((rows, d), lambda i, idx_ref: (i, 0)),
            scratch_shapes=[pltpu.SemaphoreType.DMA(())]),
        out_shape=jax.ShapeDtypeStruct((n_out, d), src.dtype),
        compiler_params=_params(("arbitrary",)),
        name="gather_rows",
    )(idx, src)


def _expert_kernel(blk_e_ref, n_used_ref, x_ref, wg_ref, wu_ref, wd_ref, o_ref, wg_b, wu_b, wd_b):
    i = pl.program_id(0)
    used = i < n_used_ref[0]
    new_expert = jnp.logical_or(i == 0, blk_e_ref[i] != blk_e_ref[jnp.maximum(i - 1, 0)])

    @pl.when(jnp.logical_and(used, new_expert))
    def _():
        wg_b[...] = wg_ref[0, 0].astype(wg_b.dtype)
        wu_b[...] = wu_ref[0, 0].astype(wu_b.dtype)
        wd_b[...] = wd_ref[0, 0].astype(wd_b.dtype)

    @pl.when(used)
    def _():
        xb = x_ref[...].astype(MXU_DT)
        gate = _dot(xb, wg_b[...])
        up = _dot(xb, wu_b[...])
        hidden = (gate * jax.nn.sigmoid(gate) * up).astype(MXU_DT)
        o_ref[...] = _dot(hidden, wd_b[...])

    @pl.when(jnp.logical_not(used))
    def _():
        o_ref[...] = jnp.zeros(o_ref.shape, o_ref.dtype)


def _expert_ffn(xs, blk_e, n_used, wg, wu, wd, layer):
    p, d = xs.shape
    t = EXPERT_ROWS
    hid = wg.shape[3]
    wmap = lambda i, be, nu: (layer, be[i], 0, 0)
    return pl.pallas_call(
        _expert_kernel,
        grid_spec=pltpu.PrefetchScalarGridSpec(
            num_scalar_prefetch=2,
            grid=(p // t,),
            in_specs=[pl.BlockSpec((t, d), lambda i, be, nu: (i, 0)),
                      pl.BlockSpec((1, 1, d, hid), wmap),
                      pl.BlockSpec((1, 1, d, hid), wmap),
                      pl.BlockSpec((1, 1, hid, d), wmap)],
            out_specs=pl.BlockSpec((t, d), lambda i, be, nu: (i, 0)),
            scratch_shapes=[pltpu.VMEM((d, hid), MXU_DT), pltpu.VMEM((d, hid), MXU_DT),
                            pltpu.VMEM((hid, d), MXU_DT)]),
        out_shape=jax.ShapeDtypeStruct((p, d), F32),
        compiler_params=_params(("arbitrary",)),
        name="expert_ffn",
    )(blk_e, n_used, xs, wg, wu, wd)


def _combine_kernel(y0_ref, y1_ref, route_ref, x_ref, g_ref, b_ref, o_ref, *, alpha):
    r = route_ref[...]
    y = r[:, 2:3] * y0_ref[...] + r[:, 3:4] * y1_ref[...]
    o_ref[...] = _layer_norm(alpha * x_ref[...] + y, g_ref[...], b_ref[...])


def _combine(y_all, route, x1, ln_g, ln_b, alpha):
    n, d = x1.shape
    tm = min(ROW_TILE, n)
    nb = n // tm
    row = lambda i: (i, 0)
    return pl.pallas_call(
        functools.partial(_combine_kernel, alpha=alpha),
        grid=(nb,),
        in_specs=[pl.BlockSpec((tm, d), row), pl.BlockSpec((tm, d), lambda i: (i + nb, 0)),
                  pl.BlockSpec((tm, LANES), row), pl.BlockSpec((tm, d), row),
                  _const_spec(ln_g.shape), _const_spec(ln_b.shape)],
        out_specs=pl.BlockSpec((tm, d), row),
        out_shape=jax.ShapeDtypeStruct((n, d), F32),
        compiler_params=_params(("arbitrary",)),
        name="combine_ln",
    )(y_all, y_all, route, x1, ln_g, ln_b)


def _dispatch_plan(route, n_tokens):
    t = EXPERT_ROWS
    a = 2 * n_tokens
    eid = route[:, 0:2].astype(jnp.int32).reshape(a)
    onehot = (eid[:, None] == jnp.arange(N_EXPERTS, dtype=jnp.int32)[None, :]).astype(jnp.int32)
    counts = onehot.sum(axis=0)
    before = jnp.cumsum(onehot, axis=0) - onehot
    rank = jnp.take_along_axis(before, eid[:, None], axis=1)[:, 0]
    padded = (counts + t - 1) // t * t
    pend = jnp.cumsum(padded)
    pstart = pend - padded
    dest = (pstart[eid] + rank).astype(jnp.int32)
    p_rows = a + N_EXPERTS * t
    filler = jnp.arange(p_rows, dtype=jnp.int32) % n_tokens
    row_tok = filler.at[dest].set(jnp.arange(a, dtype=jnp.int32) // 2)
    blk_start = jnp.arange(p_rows // t, dtype=jnp.int32) * t
    blk_e = jnp.minimum((pend[None, :] <= blk_start[:, None]).astype(jnp.int32).sum(axis=1), N_EXPERTS - 1)
    n_used = (pend[-1] // t).astype(jnp.int32).reshape(1)
    dest_kmajor = dest.reshape(n_tokens, 2).T.reshape(a)
    return dest_kmajor, row_tok, blk_e.astype(jnp.int32), n_used


def kernel(x, w_in, b_in, swa_sinks, cmp_pe_k, cmp_w1_k, cmp_w2_k, cmp_pe_v, cmp_w1_v, cmp_w2_v,
           w_out, b_out, ln1_g, ln1_b, w_group, b_group, w_expert, b_expert, we_gate, we_up,
           we_down, ln2_g, ln2_b):
    bsz, seq, d_model = x.shape
    depth = w_in.shape[0]
    n_tok = bsz * seq
    n_heads = d_model // HEAD_DIM
    q_w = n_heads // 2 * HEAD_DIM
    kv_w = N_KV_GROUPS * HEAD_DIM
    alpha = (2.0 * depth) ** 0.25
    scale = HEAD_DIM ** -0.5
    assert q_w == N_KV_GROUPS * PAIRS * LANES and seq % SLC_CHUNK == 0

    slopes_a, slopes_n = _alibi_slopes(n_heads)
    qfeat_a = jnp.asarray(_query_feats(slopes_a), F32)
    qfeat_n = jnp.asarray(_query_feats(slopes_n), F32)
    pos = np.arange(seq)
    kfeat_swa = jnp.asarray(_key_feats(pos, SWA_WINDOW // BAND_CHUNK, BAND_CHUNK, False), MXU_DT)
    kfeat_win = jnp.asarray(_key_feats(pos, NSA_WINDOW // BAND_CHUNK, BAND_CHUNK, False), MXU_DT)
    kfeat_slc = jnp.asarray(_key_feats(pos, 0, SLC_CHUNK, True), MXU_DT)
    n_cmp = (seq - NSA_CMP_LEN) // NSA_CMP_STRIDE + 1
    n_cmp_pad = seq // NSA_CMP_STRIDE
    cmp_end = np.arange(n_cmp_pad) * NSA_CMP_STRIDE + NSA_CMP_LEN - 1
    kfeat_cmp = jnp.asarray(_key_feats(cmp_end, 0, n_cmp_pad, False), MXU_DT)
    n_slc = seq // NSA_SLC_LEN
    cs = np.arange(n_cmp_pad)[None, :] * NSA_CMP_STRIDE
    ss = np.arange(LANES)[:, None] * NSA_SLC_LEN
    ovt = ((cs < ss + NSA_SLC_LEN) & (cs + NSA_CMP_LEN - 1 >= ss)
           & (np.arange(n_cmp_pad)[None, :] < n_cmp) & (np.arange(LANES)[:, None] < n_slc))
    ovt = jnp.asarray(ovt.astype(np.float32), MXU_DT)
    slopes_a_s = jnp.asarray(slopes_a, F32)

    o_qa, o_ka = 0, q_w
    o_qn = q_w + 2 * kv_w
    o_kvn = o_qn + q_w
    o_gn = o_kvn + 6 * kv_w
    gate_w = 3 * HEADS_PER_GROUP

    h = x.reshape(n_tok, d_model)
    for l in range(depth):
        wl, bl = w_in[l], b_in[l]

        def seg(a, lo, hi):
            return a[..., lo:hi]

        def gate_cols(a):
            pad = [(0, 0)] * (a.ndim - 1) + [(0, LANES - gate_w)]
            return jnp.concatenate([jnp.pad(seg(a, o_gn + g * gate_w, o_gn + (g + 1) * gate_w), pad)
                                    for g in range(N_KV_GROUPS)], axis=-1)

        def regroup(a):
            return jnp.concatenate([seg(a, o_qa, o_qa + q_w) * scale, seg(a, o_qn, o_qn + q_w) * scale,
                                    seg(a, o_ka, o_ka + 2 * kv_w), seg(a, o_kvn, o_kvn + 6 * kv_w),
                                    gate_cols(a)], axis=-1)

        w_r = regroup(wl).astype(MXU_DT)
        b_r = regroup(bl)[None, :]
        qa, qn, kv, gates = _input_projection(h, w_r, b_r)
        qa = qa.reshape(bsz, seq, q_w)
        qn = qn.reshape(bsz, seq, q_w)
        kv = kv.reshape(bsz, seq, 8 * kv_w)
        gates = gates.reshape(bsz, seq, 2 * LANES)

        o_a = _banded_attention(qa, kv, 0, 1, qfeat_a, kfeat_swa, SWA_WINDOW,
                                sinks=swa_sinks[l], slopes=slopes_a_s)

        kvc = kv[:, :, 2 * kv_w:4 * kv_w].reshape(bsz, n_cmp_pad, NSA_CMP_STRIDE, 2, N_KV_GROUPS, HEAD_DIM)
        kvc = kvc.transpose(0, 3, 4, 1, 2, 5).reshape(bsz, 2, N_KV_GROUPS, n_cmp_pad, NSA_CMP_STRIDE * HEAD_DIM)
        pe = jnp.stack([cmp_pe_k[l], cmp_pe_v[l]]).reshape(2, 1, NSA_CMP_LEN * HEAD_DIM)
        pe = jnp.broadcast_to(pe, (2, 8, NSA_CMP_LEN * HEAD_DIM))
        w1 = jnp.stack([cmp_w1_k[l], cmp_w1_v[l]]).astype(MXU_DT)
        w2 = jnp.stack([cmp_w2_k[l], cmp_w2_v[l]]).astype(MXU_DT)
        kvcmp = _compress(kvc, pe, w1, w2)

        o_cmp, maskfeat = _compressed_and_select(qn, kvcmp, qfeat_n, kfeat_cmp, ovt, gates)
        o_slc = _selected_attention(qn, kv, 4, 5, qfeat_n, maskfeat, kfeat_slc, gates)
        o_win = _banded_attention(qn, kv, 6, 7, qfeat_n, kfeat_win, NSA_WINDOW,
                                  gates=gates, gate_col=2)

        wr = jnp.concatenate([w_group[l], w_expert[l],
                              jnp.zeros((d_model, LANES - N_GROUPS - N_EXPERTS), F32)], axis=1)
        br = jnp.concatenate([b_group[l], b_expert[l],
                              jnp.zeros((LANES - N_GROUPS - N_EXPERTS,), F32)])[None, :]
        x1, route = _output_projection(
            o_a.reshape(n_tok, q_w), o_cmp.reshape(n_tok, q_w), o_slc.reshape(n_tok, q_w),
            o_win.reshape(n_tok, q_w), h, w_out[l].astype(MXU_DT), b_out[l][None, :],
            ln1_g[l][None, :], ln1_b[l][None, :], wr, br, alpha)

        dest_kmajor, row_tok, blk_e, n_used = _dispatch_plan(route, n_tok)
        xs = _gather_rows(x1, row_tok)
        ys = _expert_ffn(xs, blk_e, n_used, we_gate, we_up, we_down, l)
        y_all = _gather_rows(ys, dest_kmajor)
        h = _combine(y_all, route, x1, ln2_g[l][None, :], ln2_b[l][None, :], alpha)
    return h.reshape(bsz, seq, d_model)
```

```python
import functools
import math

import numpy as np
import jax
import jax.numpy as jnp
from jax import lax
from jax.experimental import pallas as pl
from jax.experimental.pallas import tpu as pltpu

F32 = jnp.float32
MXU_DT = jnp.bfloat16

HEAD_DIM = 64
HEADS_PER_GROUP = 8
PAIRS = HEADS_PER_GROUP // 2
N_KV_GROUPS = 2
SWA_WINDOW = 128
NSA_WINDOW = 512
NSA_CMP_LEN = 32
NSA_CMP_STRIDE = 16
NSA_CMP_HIDDEN = 256
NSA_SLC_LEN = 64
NSA_TOPK = 16
N_GROUPS = 8
EXPERTS_PER_GROUP = 8
N_EXPERTS = N_GROUPS * EXPERTS_PER_GROUP
EXPERT_HIDDEN = 512
LN_EPS = 1e-5

LANES = 128
Q_BLOCK = 256
Q_SUB = 128
BAND_CHUNK = 128
SLC_CHUNK = 256
EXPERT_ROWS = 256
ROW_TILE = 256
PROJ_TILE = 512
GATHER_ROWS = 256
GATHER_UNROLL = 8
NEG = -1e30
M_INIT = -5e29
VMEM_LIMIT = 56 * 1024 * 1024

FEAT_EVEN = 64
FEAT_ODD = 72


def _alibi_slopes(n_heads_total):
    n = n_heads_total
    s = np.exp2(-8.0 * np.arange(1, n + 1, dtype=np.float32) / np.float32(n)).astype(np.float32)
    return s[0::2], s[1::2]


def _bf16_parts(v):
    v = np.asarray(v, np.float32)
    a = v.astype(jnp.bfloat16).astype(np.float32)
    r = (v - a).astype(np.float32)
    b = r.astype(jnp.bfloat16).astype(np.float32)
    c = (r - b).astype(np.float32)
    return a, b, c


def _query_feats(slopes):
    out = np.zeros((N_KV_GROUPS, PAIRS, LANES), np.float32)
    a, b, c = _bf16_parts(slopes)
    for g in range(N_KV_GROUPS):
        for p in range(PAIRS):
            for h, base in ((0, FEAT_EVEN), (1, FEAT_ODD)):
                idx = g * HEADS_PER_GROUP + 2 * p + h
                out[g, p, base:base + 6] = [a[idx], b[idx], c[idx], a[idx], b[idx], c[idx]]
    return out


def _key_feats(positions, n_pad_chunks, chunk, with_blocks):
    pos = np.asarray(positions, np.int64)
    n = pos.shape[0]
    assert n % chunk == 0
    hi = (pos // 64 * 64).astype(np.float32)
    lo = (pos % 64).astype(np.float32)
    f = np.zeros((n, 2, LANES), np.float32)
    for v, base in ((0, FEAT_EVEN), (1, FEAT_ODD)):
        f[:, v, base:base + 3] = hi[:, None]
        f[:, v, base + 3:base + 6] = lo[:, None]
    if with_blocks:
        blk = pos // NSA_SLC_LEN
        for v in range(2):
            f[np.arange(n), v, blk] = 1.0
    f = f.reshape(n // chunk, chunk, 2, LANES).transpose(0, 2, 1, 3)
    if n_pad_chunks:
        f = np.concatenate([np.zeros((n_pad_chunks,) + f.shape[1:], np.float32), f], axis=0)
    return f


def _params(sem, vmem=VMEM_LIMIT):
    return pltpu.CompilerParams(dimension_semantics=sem, vmem_limit_bytes=vmem)


def _const_spec(shape):
    nd = len(shape)
    return pl.BlockSpec(shape, lambda *_: (0,) * nd, pipeline_mode=pl.Buffered(1))


def _dot(a, b):
    return jnp.dot(a, b, preferred_element_type=F32)


def _dot_nt(a, b):
    return lax.dot_general(a, b, (((1,), (1,)), ((), ())), preferred_element_type=F32)


def _layer_norm(h, g, b):
    mu = jnp.mean(h, axis=-1, keepdims=True)
    d = h - mu
    var = jnp.mean(d * d, axis=-1, keepdims=True)
    return d * lax.rsqrt(var + LN_EPS) * g + b


def _inproj_kernel(x_ref, w_ref, b_ref, qa_ref, qn_ref, kv_ref, gt_ref):
    xb = x_ref[...].astype(MXU_DT)
    width = qa_ref.shape[1]
    for j, o_ref in enumerate((qa_ref, qn_ref, kv_ref)):
        acc = _dot(xb, w_ref[:, j * width:(j + 1) * width]) + b_ref[:, j * width:(j + 1) * width]
        o_ref[...] = acc.astype(o_ref.dtype)
    gt_ref[...] = _dot(xb, w_ref[:, 3 * width:]) + b_ref[:, 3 * width:]


def _input_projection(x2d, w, b):
    n, d = x2d.shape
    width = (w.shape[1] - 2 * LANES) // 3
    tm = min(PROJ_TILE, n)
    row = lambda i: (i, 0)
    return pl.pallas_call(
        _inproj_kernel,
        grid=(n // tm,),
        in_specs=[pl.BlockSpec((tm, d), row), _const_spec(w.shape), _const_spec(b.shape)],
        out_specs=[pl.BlockSpec((tm, width), row)] * 3 + [pl.BlockSpec((tm, 2 * LANES), row)],
        out_shape=[jax.ShapeDtypeStruct((n, width), MXU_DT)] * 3
        + [jax.ShapeDtypeStruct((n, 2 * LANES), F32)],
        compiler_params=_params(("arbitrary",)),
        name="input_projection",
    )(x2d, w, b)


def _gelu_tanh(x):
    c = math.sqrt(2.0 / math.pi)
    return 0.5 * x * (1.0 + jnp.tanh(c * (x + 0.044715 * (x * x * x))))


def _compress_kernel(c_ref, pe_ref, w1_ref, w2_ref, o_ref):
    half = c_ref.shape[-1]
    w1 = w1_ref[0]
    pe = pe_ref[0]
    pe_hi = pe.astype(MXU_DT)
    pe_lo = (pe - pe_hi.astype(F32)).astype(MXU_DT)
    base = (_dot(pe_hi, w1) + _dot(pe_lo, w1))[0:1]
    for g in range(N_KV_GROUPS):
        c = c_ref[0, 0, g]
        top = _dot(c, w1[:half])
        bot = _dot(c, w1[half:])
        n_rows = bot.shape[0]
        hidden = top + pltpu.roll(bot, n_rows - 1, 0) + base
        act = _gelu_tanh(hidden)
        o_ref[0, 0, :, g * HEAD_DIM:(g + 1) * HEAD_DIM] = _dot(act.astype(MXU_DT), w2_ref[0])


def _compress(chunks, pe, w1, w2):
    b, two, g, nch, half = chunks.shape
    return pl.pallas_call(
        _compress_kernel,
        grid=(b, two),
        in_specs=[pl.BlockSpec((1, 1, g, nch, half), lambda i, j: (i, j, 0, 0, 0)),
                  pl.BlockSpec((1, 8, 2 * half), lambda i, j: (j, 0, 0)),
                  pl.BlockSpec((1, 2 * half, NSA_CMP_HIDDEN), lambda i, j: (j, 0, 0)),
                  pl.BlockSpec((1, NSA_CMP_HIDDEN, HEAD_DIM), lambda i, j: (j, 0, 0))],
        out_specs=pl.BlockSpec((1, 1, nch, LANES), lambda i, j: (i, j, 0, 0)),
        out_shape=jax.ShapeDtypeStruct((b, two, nch, LANES), F32),
        compiler_params=_params(("arbitrary", "arbitrary")),
        name="nsa_compress",
    )(chunks, pe, w1, w2)


def _place_group(x, g_is_zero):
    lane = lax.broadcasted_iota(jnp.int32, x.shape, 1)
    rolled = pltpu.roll(x, HEAD_DIM, 1)
    lo_src, hi_src = (x, rolled) if g_is_zero else (rolled, x)
    lo = jnp.where(lane < HEAD_DIM, lo_src, 0.0)
    hi = jnp.where(lane >= HEAD_DIM, hi_src, 0.0)
    return lo, hi


def _build_keys(g, load_k, load_v, n_chunks, kf_ref, kaug, vte, vto, n_pad, chunk, ones_rows):
    fill = 1.0 if ones_rows else 0.0
    lane = lax.broadcasted_iota(jnp.int32, (chunk, LANES), 1)
    for gval in range(N_KV_GROUPS):
        @pl.when(g == gval)
        def _():
            def body(c, carry):
                start = pl.multiple_of(c * chunk, chunk)
                klo, khi = _place_group(load_k(start), gval == 0)
                vlo, vhi = _place_group(load_v(start), gval == 0)
                kaug[n_pad + c, 0, :, 0:LANES] = klo.astype(kaug.dtype)
                kaug[n_pad + c, 1, :, 0:LANES] = khi.astype(kaug.dtype)
                vte[n_pad + c] = jnp.where(lane < HEAD_DIM, vlo, fill).T.astype(vte.dtype)
                vto[n_pad + c] = jnp.where(lane >= HEAD_DIM, vhi, fill).T.astype(vto.dtype)
                return carry
            lax.fori_loop(0, n_chunks, body, 0)
    kaug[:, :, :, LANES:2 * LANES] = kf_ref[...]
    if n_pad:
        kaug[0:n_pad, :, :, 0:LANES] = jnp.zeros((n_pad, 2, chunk, LANES), kaug.dtype)
        vte[0:n_pad] = jnp.zeros((n_pad, LANES, chunk), vte.dtype)
        vto[0:n_pad] = jnp.zeros((n_pad, LANES, chunk), vto.dtype)


def _fill_queries(qaug, q_ref, r0, tq, feats):
    for p in range(PAIRS):
        qaug[p * tq:(p + 1) * tq, 0:LANES] = q_ref[0, r0:r0 + tq, p * LANES:(p + 1) * LANES]
        qaug[p * tq:(p + 1) * tq, LANES:2 * LANES] = feats[p].astype(qaug.dtype)


def _tile_pairs(x):
    return jnp.concatenate([x] * PAIRS, axis=1)


def _finish(o_ref, r0, tq, acct_e, acct_o, gate_ref, gate_col):
    lane = lax.broadcasted_iota(jnp.int32, (tq, LANES), 1)
    if gate_ref is not None:
        gsig = jax.nn.sigmoid(gate_ref[0, r0:r0 + tq, :])
    for p in range(PAIRS):
        cols = slice(p * tq, (p + 1) * tq)
        ae = acct_e[:, cols].T
        if acct_o is None:
            out = ae
        else:
            ao = acct_o[:, cols].T
            num = jnp.where(lane < HEAD_DIM, ae, ao)
            den = jnp.where(lane < HEAD_DIM, ae[:, HEAD_DIM:HEAD_DIM + 1], ao[:, 0:1])
            out = num / den
        if gate_ref is not None:
            ce = 3 * (2 * p) + gate_col
            co = 3 * (2 * p + 1) + gate_col
            out = out * jnp.where(lane < HEAD_DIM, gsig[:, ce:ce + 1], gsig[:, co:co + 1])
        o_ref[0, r0:r0 + tq, p * LANES:(p + 1) * LANES] = out.astype(o_ref.dtype)


def _banded_kernel(*refs, n_chunks, window, has_sink, gate_col):
    refs = list(refs)
    if has_sink:
        sink_ref, slope_ref = refs[0], refs[1]
        refs = refs[2:]
    q_ref, k_ref, v_ref, qf_ref, kf_ref = refs[:5]
    refs = refs[5:]
    gate_ref = None
    if gate_col is not None:
        gate_ref = refs[0]
        refs = refs[1:]
    o_ref, kaug, vte, vto, qaug = refs
    g = pl.program_id(1)
    tq = Q_SUB
    rows = PAIRS * tq
    ch = BAND_CHUNK
    n_pad = n_chunks - 1

    @pl.when(pl.program_id(2) == 0)
    def _():
        _build_keys(g, lambda s0: k_ref[0, pl.ds(s0, ch), :].astype(F32),
                    lambda s0: v_ref[0, pl.ds(s0, ch), :].astype(F32),
                    k_ref.shape[1] // ch, kf_ref, kaug, vte, vto, n_pad, ch, ones_rows=True)

    krow = lax.broadcasted_iota(jnp.int32, (ch, tq), 0)
    qcol = lax.broadcasted_iota(jnp.int32, (ch, tq), 1)
    row128 = lax.broadcasted_iota(jnp.int32, (LANES, rows), 0)
    for sub in range(q_ref.shape[1] // tq):
        i = pl.program_id(2) * (q_ref.shape[1] // tq) + sub
        _fill_queries(qaug.at[sub], q_ref, sub * tq, tq,
                      [jnp.broadcast_to(qf_ref[0, p:p + 1, :], (tq, LANES)) for p in range(PAIRS)])
        kc = kaug[pl.ds(i, n_chunks)].reshape(n_chunks * 2 * ch, 2 * LANES)
        st = _dot_nt(kc, qaug[sub])

        t = i * tq + qcol
        ev, od = [], []
        for u in range(n_chunks):
            spos = (i - n_pad + u) * ch + krow
            dist = t - spos
            vis = jnp.where(dist >= 0, jnp.where(dist < window, jnp.where(spos >= 0, 1.0, 0.0), 0.0), 0.0)
            bias = _tile_pairs(jnp.where(vis > 0.5, 0.0, NEG))
            ev.append(st[(2 * u) * ch:(2 * u + 1) * ch] + bias)
            od.append(st[(2 * u + 1) * ch:(2 * u + 2) * ch] + bias)

        tpos = (i * tq + lax.broadcasted_iota(jnp.int32, (1, tq), 1)).astype(F32)
        accs = []
        for h, slabs in enumerate((ev, od)):
            m = slabs[0].max(axis=0, keepdims=True)
            for x in slabs[1:]:
                m = jnp.maximum(m, x.max(axis=0, keepdims=True))
            if has_sink:
                sk = jnp.concatenate(
                    [sink_ref[g * HEADS_PER_GROUP + 2 * p + h]
                     + slope_ref[g * HEADS_PER_GROUP + 2 * p + h] * tpos for p in range(PAIRS)], axis=1)
                m = jnp.maximum(m, sk)
            vt = vte if h == 0 else vto
            acc = None
            for u, x in enumerate(slabs):
                d = _dot(vt[i + u], jnp.exp(x - m).astype(vt.dtype))
                acc = d if acc is None else acc + d
            if has_sink:
                sum_rows = (row128 >= HEAD_DIM) if h == 0 else (row128 < HEAD_DIM)
                acc = acc + jnp.where(sum_rows, jnp.exp(sk - m), 0.0)
            accs.append(acc)
        _finish(o_ref, sub * tq, tq, accs[0], accs[1], gate_ref, gate_col)


def _banded_attention(q, kv, k_col, v_col, qfeat, kfeat, window, sinks=None, slopes=None,
                      gates=None, gate_col=None):
    b, s, _ = q.shape
    tq = Q_BLOCK
    n_sub = tq // Q_SUB
    n_chunks = window // BAND_CHUNK + 1
    n_tot = kfeat.shape[0]
    has_sink = sinks is not None
    kern = functools.partial(_banded_kernel, n_chunks=n_chunks, window=window,
                             has_sink=has_sink, gate_col=gate_col)
    in_specs, args = [], []
    if has_sink:
        in_specs += [pl.BlockSpec(memory_space=pltpu.SMEM)] * 2
        args += [sinks, slopes]
    in_specs += [pl.BlockSpec((1, tq, PAIRS * LANES), lambda bi, g, i: (bi, i, g)),
                 pl.BlockSpec((1, s, LANES), lambda bi, g, i: (bi, 0, k_col)),
                 pl.BlockSpec((1, s, LANES), lambda bi, g, i: (bi, 0, v_col)),
                 pl.BlockSpec((1, PAIRS, LANES), lambda bi, g, i: (g, 0, 0)),
                 _const_spec(kfeat.shape)]
    args += [q, kv, kv, qfeat, kfeat]
    if gate_col is not None:
        in_specs.append(pl.BlockSpec((1, tq, LANES), lambda bi, g, i: (bi, i, g)))
        args.append(gates)
    return pl.pallas_call(
        kern,
        grid=(b, N_KV_GROUPS, s // tq),
        in_specs=in_specs,
        out_specs=pl.BlockSpec((1, tq, PAIRS * LANES), lambda bi, g, i: (bi, i, g)),
        out_shape=jax.ShapeDtypeStruct(q.shape, MXU_DT),
        scratch_shapes=[pltpu.VMEM((n_tot, 2, BAND_CHUNK, 2 * LANES), MXU_DT),
                        pltpu.VMEM((n_tot, LANES, BAND_CHUNK), MXU_DT),
                        pltpu.VMEM((n_tot, LANES, BAND_CHUNK), MXU_DT),
                        pltpu.VMEM((n_sub, PAIRS * Q_SUB, 2 * LANES), MXU_DT)],
        compiler_params=_params(("arbitrary", "arbitrary", "arbitrary")),
        name="banded_attention_w%d" % window,
    )(*args)


def _cmpsel_kernel(q_ref, k_ref, v_ref, qf_ref, kf_ref, ovt_ref, gate_ref, o_ref, mf_ref,
                   kaug, vte, vto, qaug):
    g = pl.program_id(1)
    ncp = k_ref.shape[2]

    @pl.when(pl.program_id(2) == 0)
    def _():
        _build_keys(g, lambda s0: k_ref[0, 0], lambda s0: v_ref[0, 0], 1, kf_ref, kaug, vte, vto,
                    0, ncp, ones_rows=False)

    n_sub = q_ref.shape[1] // Q_SUB
    for sub in range(n_sub):
        _cmpsel_block(pl.program_id(2) * n_sub + sub, sub, ncp, q_ref, qf_ref, ovt_ref, gate_ref,
                      o_ref, mf_ref, kaug, vte, vto, qaug)


def _cmpsel_block(i, sub, ncp, q_ref, qf_ref, ovt_ref, gate_ref, o_ref, mf_ref, kaug, vte, vto, qaug):
    tq = Q_SUB
    n_slc = NSA_SLC_LEN
    _fill_queries(qaug.at[sub], q_ref, sub * tq, tq,
                  [jnp.broadcast_to(qf_ref[0, p:p + 1, :], (tq, LANES)) for p in range(PAIRS)])
    st = _dot_nt(kaug[0].reshape(2 * ncp, 2 * LANES), qaug[sub])

    krow = lax.broadcasted_iota(jnp.int32, (ncp, tq), 0)
    qcol = lax.broadcasted_iota(jnp.int32, (ncp, tq), 1)
    t = i * tq + qcol
    cmp_end = krow * NSA_CMP_STRIDE + (NSA_CMP_LEN - 1)
    vis = jnp.where(t >= cmp_end, jnp.where(krow < ncp - 1, 1.0, 0.0), 0.0)
    bias = _tile_pairs(jnp.where(vis > 0.5, 0.0, NEG))

    pn = []
    for h in range(2):
        x = st[h * ncp:(h + 1) * ncp] + bias
        m = x.max(axis=0, keepdims=True)
        m = jnp.where(m > 0.5 * NEG, m, 0.0)
        e = jnp.exp(x - m)
        d = e.sum(axis=0, keepdims=True)
        pn.append(e * (1.0 / jnp.where(d > 0.0, d, 1.0)))
    acct = _dot(vte[0], pn[0].astype(vte.dtype)) + _dot(vto[0], pn[1].astype(vto.dtype))
    _finish(o_ref, sub * tq, tq, acct, None, gate_ref, 0)

    ps = pn[0] + pn[1]
    psum = ps[:, 0:tq]
    for p in range(1, PAIRS):
        psum = psum + ps[:, p * tq:(p + 1) * tq]
    p_hi = psum.astype(MXU_DT)
    p_lo = (psum - p_hi.astype(F32)).astype(MXU_DT)
    imp = (_dot(ovt_ref[...], p_hi) + _dot(ovt_ref[...], p_lo))[0:n_slc]
    blk = lax.broadcasted_iota(jnp.int32, (n_slc, tq), 0)
    tq_pos = i * tq + lax.broadcasted_iota(jnp.int32, (n_slc, tq), 1)
    cur = lax.shift_right_logical(tq_pos, int(math.log2(NSA_SLC_LEN)))
    forced = jnp.where(blk == 0, 1.0, jnp.where(blk == cur, 1.0, jnp.where(blk == cur - 1, 1.0, 0.0)))
    valid = blk <= cur
    score = jnp.where(valid, jnp.where(forced > 0.5, jnp.inf, imp), -jnp.inf)
    groups = [score[8 * v:8 * v + 8] for v in range(n_slc // 8)]
    ranks = [jnp.zeros((8, tq), F32) for _ in groups]
    sublane = lax.broadcasted_iota(jnp.int32, (8, tq), 0)
    for j in range(n_slc):
        rj = score[j:j + 1, :]
        for v in range(n_slc // 8):
            if 8 * v > j:
                beats = jnp.where(rj >= groups[v], 1.0, 0.0)
            elif 8 * v + 7 < j:
                beats = jnp.where(rj > groups[v], 1.0, 0.0)
            else:
                beats = jnp.where(sublane > (j - 8 * v), jnp.where(rj >= groups[v], 1.0, 0.0),
                                  jnp.where(rj > groups[v], 1.0, 0.0))
            ranks[v] = ranks[v] + beats
    rank = jnp.concatenate(ranks, axis=0)
    keep = jnp.where(valid, jnp.where(rank < float(NSA_TOPK), 1.0, 0.0), 0.0)
    feat = jnp.where(keep > 0.5, 0.0, NEG)
    feat = jnp.concatenate([feat, jnp.zeros((LANES - n_slc, tq), F32)], axis=0)
    mf_ref[0, 0, sub * tq:(sub + 1) * tq, :] = feat.T.astype(mf_ref.dtype)


def _compressed_and_select(q, kvcmp, qfeat, kfeat, ovt, gates):
    b, s, _ = q.shape
    tq = Q_BLOCK
    ncp = kvcmp.shape[2]
    return pl.pallas_call(
        _cmpsel_kernel,
        grid=(b, N_KV_GROUPS, s // tq),
        in_specs=[pl.BlockSpec((1, tq, PAIRS * LANES), lambda bi, g, i: (bi, i, g)),
                  pl.BlockSpec((1, 1, ncp, LANES), lambda bi, g, i: (bi, 0, 0, 0)),
                  pl.BlockSpec((1, 1, ncp, LANES), lambda bi, g, i: (bi, 1, 0, 0)),
                  pl.BlockSpec((1, PAIRS, LANES), lambda bi, g, i: (g, 0, 0)),
                  _const_spec(kfeat.shape), _const_spec(ovt.shape),
                  pl.BlockSpec((1, tq, LANES), lambda bi, g, i: (bi, i, g))],
        out_specs=[pl.BlockSpec((1, tq, PAIRS * LANES), lambda bi, g, i: (bi, i, g)),
                   pl.BlockSpec((1, 1, tq, LANES), lambda bi, g, i: (bi, g, i, 0))],
        out_shape=[jax.ShapeDtypeStruct(q.shape, MXU_DT),
                   jax.ShapeDtypeStruct((b, N_KV_GROUPS, s, LANES), MXU_DT)],
        scratch_shapes=[pltpu.VMEM((1, 2, ncp, 2 * LANES), MXU_DT),
                        pltpu.VMEM((1, LANES, ncp), MXU_DT),
                        pltpu.VMEM((1, LANES, ncp), MXU_DT),
                        pltpu.VMEM((tq // Q_SUB, PAIRS * Q_SUB, 2 * LANES), MXU_DT)],
        compiler_params=_params(("arbitrary", "arbitrary", "arbitrary")),
        name="nsa_compressed_select",
    )(q, kvcmp, kvcmp, qfeat, kfeat, ovt, gates)


def _selected_kernel(q_ref, k_ref, v_ref, qf_ref, mf_ref, kf_ref, gate_ref, o_ref,
                     kaug, vte, vto, qaug, acc_e, acc_o, st_a, st_b):
    g = pl.program_id(1)
    i = pl.program_id(2)
    tq = q_ref.shape[1]
    ch = SLC_CHUNK
    rows = PAIRS * tq

    @pl.when(i == 0)
    def _():
        _build_keys(g, lambda s0: k_ref[0, pl.ds(s0, ch), :].astype(F32),
                    lambda s0: v_ref[0, pl.ds(s0, ch), :].astype(F32),
                    k_ref.shape[1] // ch, kf_ref, kaug, vte, vto, 0, ch, ones_rows=True)

    mask_feat = mf_ref[0, 0].astype(F32)
    _fill_queries(qaug, q_ref, 0, tq, [mask_feat + qf_ref[0, p:p + 1, :] for p in range(PAIRS)])
    acc_e[...] = jnp.zeros((LANES, rows), F32)
    acc_o[...] = jnp.zeros((LANES, rows), F32)

    def update(x, m_old, acc_ref, vt_c):
        m_new = jnp.maximum(m_old, x.max(axis=0, keepdims=True))
        alpha = jnp.exp(m_old - m_new)
        p = jnp.exp(x - m_new).astype(vt_c.dtype)
        acc_ref[...] = alpha * acc_ref[...] + _dot(vt_c, p)
        return m_new

    def scores_into(buf, c):
        buf[...] = _dot_nt(kaug[c].reshape(2 * ch, 2 * LANES), qaug[...])

    def consume(buf, c, ms, causal):
        xe, xo = buf[0:ch, :], buf[ch:2 * ch, :]
        if causal:
            krow = lax.broadcasted_iota(jnp.int32, (ch, tq), 0)
            qcol = lax.broadcasted_iota(jnp.int32, (ch, tq), 1)
            bias = _tile_pairs(jnp.where(c * ch + krow <= i * tq + qcol, 0.0, NEG))
            xe, xo = xe + bias, xo + bias
        return update(xe, ms[0], acc_e, vte[c]), update(xo, ms[1], acc_o, vto[c])

    def pair(k, ms):
        c = 2 * k
        scores_into(st_b, c + 1)
        ms = consume(st_a, c, ms, False)
        scores_into(st_a, c + 2)
        return consume(st_b, c + 1, ms, False)

    c_diag = (i * tq) // ch
    m0 = jnp.full((1, rows), M_INIT, F32)
    scores_into(st_a, 0)
    ms = lax.fori_loop(0, c_diag // 2, pair, (m0, m0))

    @pl.when(c_diag % 2 == 0)
    def _():
        consume(st_a, c_diag, ms, True)

    @pl.when(c_diag % 2 == 1)
    def _():
        scores_into(st_b, c_diag)
        consume(st_b, c_diag, consume(st_a, c_diag - 1, ms, False), True)

    _finish(o_ref, 0, tq, acc_e[...], acc_o[...], gate_ref, 1)


def _selected_attention(q, kv, k_col, v_col, qfeat, maskfeat, kfeat, gates):
    b, s, _ = q.shape
    tq = Q_BLOCK
    n_tot = kfeat.shape[0]
    return pl.pallas_call(
        _selected_kernel,
        grid=(b, N_KV_GROUPS, s // tq),
        in_specs=[pl.BlockSpec((1, tq, PAIRS * LANES), lambda bi, g, i: (bi, i, g)),
                  pl.BlockSpec((1, s, LANES), lambda bi, g, i: (bi, 0, k_col)),
                  pl.BlockSpec((1, s, LANES), lambda bi, g, i: (bi, 0, v_col)),
                  pl.BlockSpec((1, PAIRS, LANES), lambda bi, g, i: (g, 0, 0)),
                  pl.BlockSpec((1, 1, tq, LANES), lambda bi, g, i: (bi, g, i, 0)),
                  _const_spec(kfeat.shape),
                  pl.BlockSpec((1, tq, LANES), lambda bi, g, i: (bi, i, g))],
        out_specs=pl.BlockSpec((1, tq, PAIRS * LANES), lambda bi, g, i: (bi, i, g)),
        out_shape=jax.ShapeDtypeStruct(q.shape, MXU_DT),
        scratch_shapes=[pltpu.VMEM((n_tot, 2, SLC_CHUNK, 2 * LANES), MXU_DT),
                        pltpu.VMEM((n_tot, LANES, SLC_CHUNK), MXU_DT),
                        pltpu.VMEM((n_tot, LANES, SLC_CHUNK), MXU_DT),
                        pltpu.VMEM((PAIRS * tq, 2 * LANES), MXU_DT),
                        pltpu.VMEM((LANES, PAIRS * tq), F32),
                        pltpu.VMEM((LANES, PAIRS * tq), F32),
                        pltpu.VMEM((2 * SLC_CHUNK, PAIRS * tq), F32),
                        pltpu.VMEM((2 * SLC_CHUNK, PAIRS * tq), F32)],
        compiler_params=_params(("arbitrary", "arbitrary", "arbitrary")),
        name="nsa_selected",
    )(q, kv, kv, qfeat, maskfeat, kfeat, gates)


def _outproj_kernel(oa_ref, oc_ref, os_ref, ow_ref, x_ref, w_ref, b_ref, g_ref, beta_ref,
                    wr_ref, br_ref, x1_ref, route_ref, *, alpha):
    half = oa_ref.shape[1]
    on = (oc_ref[...].astype(F32) + os_ref[...].astype(F32) + ow_ref[...].astype(F32)).astype(MXU_DT)
    mix = _dot(oa_ref[...], w_ref[0:half]) + _dot(on, w_ref[half:]) + b_ref[...]
    x1 = _layer_norm(alpha * x_ref[...] + mix, g_ref[...], beta_ref[...])
    x1_ref[...] = x1

    x_hi = x1.astype(MXU_DT)
    x_lo = (x1 - x_hi.astype(F32)).astype(MXU_DT)
    logits = (_dot(x_hi, wr_ref[0]) + _dot(x_hi, wr_ref[1]) + _dot(x_lo, wr_ref[0])) + br_ref[...]
    lane = lax.broadcasted_iota(jnp.int32, logits.shape, 1)
    lanef = lane.astype(F32)
    big = float(4 * LANES)
    gl = jnp.where(lane < N_GROUPS, logits, -jnp.inf)
    gmax = gl.max(axis=1, keepdims=True)
    gsel = jnp.where(gl == gmax, lanef, big).min(axis=1, keepdims=True)
    g_w = 1.0 / jnp.exp(gl - gmax).sum(axis=1, keepdims=True)
    lo = N_GROUPS + gsel * EXPERTS_PER_GROUP
    el = jnp.where(lanef >= lo, jnp.where(lanef < lo + EXPERTS_PER_GROUP, logits, -jnp.inf), -jnp.inf)
    v1 = el.max(axis=1, keepdims=True)
    i1 = jnp.where(el == v1, lanef, big).min(axis=1, keepdims=True)
    el2 = jnp.where(lanef == i1, -jnp.inf, el)
    v2 = el2.max(axis=1, keepdims=True)
    i2 = jnp.where(el2 == v2, lanef, big).min(axis=1, keepdims=True)
    e2 = jnp.exp(v2 - v1)
    w1 = g_w / (1.0 + e2)
    w2 = g_w * e2 / (1.0 + e2)
    route_ref[...] = jnp.where(lane == 0, i1 - N_GROUPS,
                               jnp.where(lane == 1, i2 - N_GROUPS,
                                         jnp.where(lane == 2, w1, jnp.where(lane == 3, w2, 0.0))))


def _output_projection(oa, oc, os_, ow, x2d, w, b, ln_g, ln_b, wr, br, alpha):
    n, d = x2d.shape
    half = oa.shape[1]
    tm = min(PROJ_TILE, n)
    row = lambda i: (i, 0)
    return pl.pallas_call(
        functools.partial(_outproj_kernel, alpha=alpha),
        grid=(n // tm,),
        in_specs=[pl.BlockSpec((tm, half), row)] * 4 + [pl.BlockSpec((tm, d), row)]
        + [_const_spec(a.shape) for a in (w, b, ln_g, ln_b, wr, br)],
        out_specs=[pl.BlockSpec((tm, d), row), pl.BlockSpec((tm, LANES), row)],
        out_shape=[jax.ShapeDtypeStruct((n, d), F32), jax.ShapeDtypeStruct((n, LANES), F32)],
        compiler_params=_params(("arbitrary",)),
        name="output_projection_ln_route",
    )(oa, oc, os_, ow, x2d, w, b, ln_g, ln_b, wr, br)


def _gather_kernel(idx_ref, src_ref, o_ref, sem):
    rows = o_ref.shape[0]
    base = pl.program_id(0) * rows

    def copy(r, src_row):
        return pltpu.make_async_copy(src_ref.at[pl.ds(src_row, 1)], o_ref.at[pl.ds(r, 1)], sem)

    def start(r, carry):
        copy(r, idx_ref[base + r]).start()
        return carry

    def wait(r, carry):
        copy(r, 0).wait()
        return carry

    lax.fori_loop(0, rows, start, 0, unroll=GATHER_UNROLL)
    lax.fori_loop(0, rows, wait, 0, unroll=GATHER_UNROLL)


def _gather_rows(src, idx):
    n_out = idx.shape[0]
    d = src.shape[1]
    rows = min(GATHER_ROWS, n_out)
    return pl.pallas_call(
        _gather_kernel,
        grid_spec=pltpu.PrefetchScalarGridSpec(
            num_scalar_prefetch=1,
            grid=(n_out // rows,),
            in_specs=[pl.BlockSpec(memory_space=pl.ANY)],
            out_specs=pl.BlockSpec((rows, d), lambda i, idx_ref: (i, 0)),
            scratch_shapes=[pltpu.SemaphoreType.DMA(())]),
        out_shape=jax.ShapeDtypeStruct((n_out, d), src.dtype),
        compiler_params=_params(("arbitrary",)),
        name="gather_rows",
    )(idx, src)


def _expert_kernel(blk_e_ref, n_used_ref, x_ref, wg_ref, wu_ref, wd_ref, o_ref, wg_b, wu_b, wd_b):
    i = pl.program_id(0)
    used = i < n_used_ref[0]
    new_expert = jnp.logical_or(i == 0, blk_e_ref[i] != blk_e_ref[jnp.maximum(i - 1, 0)])

    @pl.when(jnp.logical_and(used, new_expert))
    def _():
        wg_b[...] = wg_ref[0, 0].astype(wg_b.dtype)
        wu_b[...] = wu_ref[0, 0].astype(wu_b.dtype)
        wd_b[...] = wd_ref[0, 0].astype(wd_b.dtype)

    @pl.when(used)
    def _():
        xb = x_ref[...].astype(MXU_DT)
        gate = _dot(xb, wg_b[...])
        up = _dot(xb, wu_b[...])
        hidden = (gate * jax.nn.sigmoid(gate) * up).astype(MXU_DT)
        o_ref[...] = _dot(hidden, wd_b[...])

    @pl.when(jnp.logical_not(used))
    def _():
        o_ref[...] = jnp.zeros(o_ref.shape, o_ref.dtype)


def _expert_ffn(xs, blk_e, n_used, wg, wu, wd, layer):
    p, d = xs.shape
    t = EXPERT_ROWS
    hid = wg.shape[3]
    wmap = lambda i, be, nu: (layer, be[i], 0, 0)
    return pl.pallas_call(
        _expert_kernel,
        grid_spec=pltpu.PrefetchScalarGridSpec(
            num_scalar_prefetch=2,
            grid=(p // t,),
            in_specs=[pl.BlockSpec((t, d), lambda i, be, nu: (i, 0)),
                      pl.BlockSpec((1, 1, d, hid), wmap),
                      pl.BlockSpec((1, 1, d, hid), wmap),
                      pl.BlockSpec((1, 1, hid, d), wmap)],
            out_specs=pl.BlockSpec((t, d), lambda i, be, nu: (i, 0)),
            scratch_shapes=[pltpu.VMEM((d, hid), MXU_DT), pltpu.VMEM((d, hid), MXU_DT),
                            pltpu.VMEM((hid, d), MXU_DT)]),
        out_shape=jax.ShapeDtypeStruct((p, d), F32),
        compiler_params=_params(("arbitrary",)),
        name="expert_ffn",
    )(blk_e, n_used, xs, wg, wu, wd)


def _combine_kernel(y0_ref, y1_ref, route_ref, x_ref, g_ref, b_ref, o_ref, *, alpha):
    r = route_ref[...]
    y = r[:, 2:3] * y0_ref[...] + r[:, 3:4] * y1_ref[...]
    o_ref[...] = _layer_norm(alpha * x_ref[...] + y, g_ref[...], b_ref[...])


def _combine(y_all, route, x1, ln_g, ln_b, alpha):
    n, d = x1.shape
    tm = min(ROW_TILE, n)
    nb = n // tm
    row = lambda i: (i, 0)
    return pl.pallas_call(
        functools.partial(_combine_kernel, alpha=alpha),
        grid=(nb,),
        in_specs=[pl.BlockSpec((tm, d), row), pl.BlockSpec((tm, d), lambda i: (i + nb, 0)),
                  pl.BlockSpec((tm, LANES), row), pl.BlockSpec((tm, d), row),
                  _const_spec(ln_g.shape), _const_spec(ln_b.shape)],
        out_specs=pl.BlockSpec((tm, d), row),
        out_shape=jax.ShapeDtypeStruct((n, d), F32),
        compiler_params=_params(("arbitrary",)),
        name="combine_ln",
    )(y_all, y_all, route, x1, ln_g, ln_b)


def _dispatch_plan(route, n_tokens):
    t = EXPERT_ROWS
    a = 2 * n_tokens
    eid = route[:, 0:2].astype(jnp.int32).reshape(a)
    onehot = (eid[:, None] == jnp.arange(N_EXPERTS, dtype=jnp.int32)[None, :]).astype(jnp.int32)
    counts = onehot.sum(axis=0)
    before = jnp.cumsum(onehot, axis=0) - onehot
    rank = jnp.take_along_axis(before, eid[:, None], axis=1)[:, 0]
    padded = (counts + t - 1) // t * t
    pend = jnp.cumsum(padded)
    pstart = pend - padded
    dest = (pstart[eid] + rank).astype(jnp.int32)
    p_rows = a + N_EXPERTS * t
    filler = jnp.arange(p_rows, dtype=jnp.int32) % n_tokens
    row_tok = filler.at[dest].set(jnp.arange(a, dtype=jnp.int32) // 2)
    blk_start = jnp.arange(p_rows // t, dtype=jnp.int32) * t
    blk_e = jnp.minimum((pend[None, :] <= blk_start[:, None]).astype(jnp.int32).sum(axis=1), N_EXPERTS - 1)
    n_used = (pend[-1] // t).astype(jnp.int32).reshape(1)
    dest_kmajor = dest.reshape(n_tokens, 2).T.reshape(a)
    return dest_kmajor, row_tok, blk_e.astype(jnp.int32), n_used


def kernel(x, w_in, b_in, swa_sinks, cmp_pe_k, cmp_w1_k, cmp_w2_k, cmp_pe_v, cmp_w1_v, cmp_w2_v,
           w_out, b_out, ln1_g, ln1_b, w_group, b_group, w_expert, b_expert, we_gate, we_up,
           we_down, ln2_g, ln2_b):
    bsz, seq, d_model = x.shape
    depth = w_in.shape[0]
    n_tok = bsz * seq
    n_heads = d_model // HEAD_DIM
    q_w = n_heads // 2 * HEAD_DIM
    kv_w = N_KV_GROUPS * HEAD_DIM
    alpha = (2.0 * depth) ** 0.25
    scale = HEAD_DIM ** -0.5
    assert q_w == N_KV_GROUPS * PAIRS * LANES and seq % SLC_CHUNK == 0

    slopes_a, slopes_n = _alibi_slopes(n_heads)
    qfeat_a = jnp.asarray(_query_feats(slopes_a), F32)
    qfeat_n = jnp.asarray(_query_feats(slopes_n), F32)
    pos = np.arange(seq)
    kfeat_swa = jnp.asarray(_key_feats(pos, SWA_WINDOW // BAND_CHUNK, BAND_CHUNK, False), MXU_DT)
    kfeat_win = jnp.asarray(_key_feats(pos, NSA_WINDOW // BAND_CHUNK, BAND_CHUNK, False), MXU_DT)
    kfeat_slc = jnp.asarray(_key_feats(pos, 0, SLC_CHUNK, True), MXU_DT)
    n_cmp = (seq - NSA_CMP_LEN) // NSA_CMP_STRIDE + 1
    n_cmp_pad = seq // NSA_CMP_STRIDE
    cmp_end = np.arange(n_cmp_pad) * NSA_CMP_STRIDE + NSA_CMP_LEN - 1
    kfeat_cmp = jnp.asarray(_key_feats(cmp_end, 0, n_cmp_pad, False), MXU_DT)
    n_slc = seq // NSA_SLC_LEN
    cs = np.arange(n_cmp_pad)[None, :] * NSA_CMP_STRIDE
    ss = np.arange(LANES)[:, None] * NSA_SLC_LEN
    ovt = ((cs < ss + NSA_SLC_LEN) & (cs + NSA_CMP_LEN - 1 >= ss)
           & (np.arange(n_cmp_pad)[None, :] < n_cmp) & (np.arange(LANES)[:, None] < n_slc))
    ovt = jnp.asarray(ovt.astype(np.float32), MXU_DT)
    slopes_a_s = jnp.asarray(slopes_a, F32)

    o_qa, o_ka = 0, q_w
    o_qn = q_w + 2 * kv_w
    o_kvn = o_qn + q_w
    o_gn = o_kvn + 6 * kv_w
    gate_w = 3 * HEADS_PER_GROUP

    h = x.reshape(n_tok, d_model)
    for l in range(depth):
        wl, bl = w_in[l], b_in[l]

        def seg(a, lo, hi):
            return a[..., lo:hi]

        def gate_cols(a):
            pad = [(0, 0)] * (a.ndim - 1) + [(0, LANES - gate_w)]
            return jnp.concatenate([jnp.pad(seg(a, o_gn + g * gate_w, o_gn + (g + 1) * gate_w), pad)
                                    for g in range(N_KV_GROUPS)], axis=-1)

        def regroup(a):
            return jnp.concatenate([seg(a, o_qa, o_qa + q_w) * scale, seg(a, o_qn, o_qn + q_w) * scale,
                                    seg(a, o_ka, o_ka + 2 * kv_w), seg(a, o_kvn, o_kvn + 6 * kv_w),
                                    gate_cols(a)], axis=-1)

        w_r = regroup(wl).astype(MXU_DT)
        b_r = regroup(bl)[None, :]
        qa, qn, kv, gates = _input_projection(h, w_r, b_r)
        qa = qa.reshape(bsz, seq, q_w)
        qn = qn.reshape(bsz, seq, q_w)
        kv = kv.reshape(bsz, seq, 8 * kv_w)
        gates = gates.reshape(bsz, seq, 2 * LANES)

        o_a = _banded_attention(qa, kv, 0, 1, qfeat_a, kfeat_swa, SWA_WINDOW,
                                sinks=swa_sinks[l], slopes=slopes_a_s)

        kvc = kv[:, :, 2 * kv_w:4 * kv_w].reshape(bsz, n_cmp_pad, NSA_CMP_STRIDE, 2, N_KV_GROUPS, HEAD_DIM)
        kvc = kvc.transpose(0, 3, 4, 1, 2, 5).reshape(bsz, 2, N_KV_GROUPS, n_cmp_pad, NSA_CMP_STRIDE * HEAD_DIM)
        pe = jnp.stack([cmp_pe_k[l], cmp_pe_v[l]]).reshape(2, 1, NSA_CMP_LEN * HEAD_DIM)
        pe = jnp.broadcast_to(pe, (2, 8, NSA_CMP_LEN * HEAD_DIM))
        w1 = jnp.stack([cmp_w1_k[l], cmp_w1_v[l]]).astype(MXU_DT)
        w2 = jnp.stack([cmp_w2_k[l], cmp_w2_v[l]]).astype(MXU_DT)
        kvcmp = _compress(kvc, pe, w1, w2)

        o_cmp, maskfeat = _compressed_and_select(qn, kvcmp, qfeat_n, kfeat_cmp, ovt, gates)
        o_slc = _selected_attention(qn, kv, 4, 5, qfeat_n, maskfeat, kfeat_slc, gates)
        o_win = _banded_attention(qn, kv, 6, 7, qfeat_n, kfeat_win, NSA_WINDOW,
                                  gates=gates, gate_col=2)

        wr = jnp.concatenate([w_group[l], w_expert[l],
                              jnp.zeros((d_model, LANES - N_GROUPS - N_EXPERTS), F32)], axis=1)
        br = jnp.concatenate([b_group[l], b_expert[l],
                              jnp.zeros((LANES - N_GROUPS - N_EXPERTS,), F32)])[None, :]
        wr_hi = wr.astype(MXU_DT)
        wr = jnp.stack([wr_hi, (wr - wr_hi.astype(F32)).astype(MXU_DT)])
        x1, route = _output_projection(
            o_a.reshape(n_tok, q_w), o_cmp.reshape(n_tok, q_w), o_slc.reshape(n_tok, q_w),
            o_win.reshape(n_tok, q_w), h, w_out[l].astype(MXU_DT), b_out[l][None, :],
            ln1_g[l][None, :], ln1_b[l][None, :], wr, br, alpha)

        dest_kmajor, row_tok, blk_e, n_used = _dispatch_plan(route, n_tok)
        xs = _gather_rows(x1, row_tok)
        ys = _expert_ffn(xs, blk_e, n_used, we_gate, we_up, we_down, l)
        y_all = _gather_rows(ys, dest_kmajor)
        h = _combine(y_all, route, x1, ln2_g[l][None, :], ln2_b[l][None, :], alpha)
    return h.reshape(bsz, seq, d_model)
```

```python
import functools
import math

import numpy as np
import jax
import jax.numpy as jnp
from jax import lax
from jax.experimental import pallas as pl
from jax.experimental.pallas import tpu as pltpu

F32 = jnp.float32
MXU_DT = jnp.bfloat16

HEAD_DIM = 64
HEADS_PER_GROUP = 8
PAIRS = HEADS_PER_GROUP // 2
N_KV_GROUPS = 2
SWA_WINDOW = 128
NSA_WINDOW = 512
NSA_CMP_LEN = 32
NSA_CMP_STRIDE = 16
NSA_CMP_HIDDEN = 256
NSA_SLC_LEN = 64
NSA_TOPK = 16
N_GROUPS = 8
EXPERTS_PER_GROUP = 8
N_EXPERTS = N_GROUPS * EXPERTS_PER_GROUP
EXPERT_HIDDEN = 512
LN_EPS = 1e-5

LANES = 128
SUBLANES = 8
Q_BLOCK = 256
Q_SUB = 128
BAND_CHUNK = 128
SLC_CHUNK = 256
EXPERT_ROWS = 256
ROW_TILE = 256
PROJ_TILE = 512
NEG = -1e30
M_INIT = -5e29
VMEM_LIMIT = 56 * 1024 * 1024

FEAT_EVEN = 64
FEAT_ODD = 72


def _alibi_slopes(n_heads_total):
    n = n_heads_total
    s = np.exp2(-8.0 * np.arange(1, n + 1, dtype=np.float32) / np.float32(n)).astype(np.float32)
    return s[0::2], s[1::2]


def _bf16_parts(v):
    v = np.asarray(v, np.float32)
    a = v.astype(jnp.bfloat16).astype(np.float32)
    r = (v - a).astype(np.float32)
    b = r.astype(jnp.bfloat16).astype(np.float32)
    c = (r - b).astype(np.float32)
    return a, b, c


def _query_feats(slopes):
    out = np.zeros((N_KV_GROUPS, PAIRS, LANES), np.float32)
    a, b, c = _bf16_parts(slopes)
    for g in range(N_KV_GROUPS):
        for p in range(PAIRS):
            for h, base in ((0, FEAT_EVEN), (1, FEAT_ODD)):
                idx = g * HEADS_PER_GROUP + 2 * p + h
                out[g, p, base:base + 6] = [a[idx], b[idx], c[idx], a[idx], b[idx], c[idx]]
    return out


def _key_feats(positions, n_pad_chunks, chunk, with_blocks):
    pos = np.asarray(positions, np.int64)
    n = pos.shape[0]
    assert n % chunk == 0
    hi = (pos // 64 * 64).astype(np.float32)
    lo = (pos % 64).astype(np.float32)
    f = np.zeros((n, 2, LANES), np.float32)
    for v, base in ((0, FEAT_EVEN), (1, FEAT_ODD)):
        f[:, v, base:base + 3] = hi[:, None]
        f[:, v, base + 3:base + 6] = lo[:, None]
    if with_blocks:
        blk = pos // NSA_SLC_LEN
        for v in range(2):
            f[np.arange(n), v, blk] = 1.0
    f = f.reshape(n // chunk, chunk, 2, LANES).transpose(0, 2, 1, 3)
    if n_pad_chunks:
        f = np.concatenate([np.zeros((n_pad_chunks,) + f.shape[1:], np.float32), f], axis=0)
    return f


def _params(sem, vmem=VMEM_LIMIT):
    return pltpu.CompilerParams(dimension_semantics=sem, vmem_limit_bytes=vmem)


def _const_spec(shape):
    nd = len(shape)
    return pl.BlockSpec(shape, lambda *_: (0,) * nd, pipeline_mode=pl.Buffered(1))


def _dot(a, b):
    return jnp.dot(a, b, preferred_element_type=F32)


def _dot_nt(a, b):
    return lax.dot_general(a, b, (((1,), (1,)), ((), ())), preferred_element_type=F32)


def _layer_norm(h, g, b):
    mu = jnp.mean(h, axis=-1, keepdims=True)
    d = h - mu
    var = jnp.mean(d * d, axis=-1, keepdims=True)
    return d * lax.rsqrt(var + LN_EPS) * g + b


def _inproj_kernel(x_ref, w_ref, b_ref, qa_ref, qn_ref, kv_ref, gt_ref):
    xb = x_ref[...].astype(MXU_DT)
    width = qa_ref.shape[1]
    for j, o_ref in enumerate((qa_ref, qn_ref, kv_ref)):
        acc = _dot(xb, w_ref[:, j * width:(j + 1) * width]) + b_ref[:, j * width:(j + 1) * width]
        o_ref[...] = acc.astype(o_ref.dtype)
    gt_ref[...] = _dot(xb, w_ref[:, 3 * width:]) + b_ref[:, 3 * width:]


def _input_projection(x2d, w, b):
    n, d = x2d.shape
    width = (w.shape[1] - 2 * LANES) // 3
    tm = min(PROJ_TILE, n)
    row = lambda i: (i, 0)
    return pl.pallas_call(
        _inproj_kernel,
        grid=(n // tm,),
        in_specs=[pl.BlockSpec((tm, d), row), _const_spec(w.shape), _const_spec(b.shape)],
        out_specs=[pl.BlockSpec((tm, width), row)] * 3 + [pl.BlockSpec((tm, 2 * LANES), row)],
        out_shape=[jax.ShapeDtypeStruct((n, width), MXU_DT)] * 3
        + [jax.ShapeDtypeStruct((n, 2 * LANES), F32)],
        compiler_params=_params(("arbitrary",)),
        name="input_projection",
    )(x2d, w, b)


def _gelu_tanh(x):
    c = math.sqrt(2.0 / math.pi)
    return 0.5 * x * (1.0 + jnp.tanh(c * (x + 0.044715 * (x * x * x))))


def _compress_kernel(c_ref, pe_ref, w1_ref, w2_ref, o_ref):
    half = c_ref.shape[-1]
    w1 = w1_ref[0]
    pe = pe_ref[0]
    pe_hi = pe.astype(MXU_DT)
    pe_lo = (pe - pe_hi.astype(F32)).astype(MXU_DT)
    base = (_dot(pe_hi, w1) + _dot(pe_lo, w1))[0:1]
    for g in range(N_KV_GROUPS):
        c = c_ref[0, 0, g]
        top = _dot(c, w1[:half])
        bot = _dot(c, w1[half:])
        n_rows = bot.shape[0]
        hidden = top + pltpu.roll(bot, n_rows - 1, 0) + base
        act = _gelu_tanh(hidden)
        o_ref[0, 0, :, g * HEAD_DIM:(g + 1) * HEAD_DIM] = _dot(act.astype(MXU_DT), w2_ref[0])


def _compress(chunks, pe, w1, w2):
    b, two, g, nch, half = chunks.shape
    return pl.pallas_call(
        _compress_kernel,
        grid=(b, two),
        in_specs=[pl.BlockSpec((1, 1, g, nch, half), lambda i, j: (i, j, 0, 0, 0)),
                  pl.BlockSpec((1, 8, 2 * half), lambda i, j: (j, 0, 0)),
                  pl.BlockSpec((1, 2 * half, NSA_CMP_HIDDEN), lambda i, j: (j, 0, 0)),
                  pl.BlockSpec((1, NSA_CMP_HIDDEN, HEAD_DIM), lambda i, j: (j, 0, 0))],
        out_specs=pl.BlockSpec((1, 1, nch, LANES), lambda i, j: (i, j, 0, 0)),
        out_shape=jax.ShapeDtypeStruct((b, two, nch, LANES), F32),
        compiler_params=_params(("arbitrary", "arbitrary")),
        name="nsa_compress",
    )(chunks, pe, w1, w2)


def _place_group(x, g_is_zero):
    lane = lax.broadcasted_iota(jnp.int32, x.shape, 1)
    rolled = pltpu.roll(x, HEAD_DIM, 1)
    lo_src, hi_src = (x, rolled) if g_is_zero else (rolled, x)
    lo = jnp.where(lane < HEAD_DIM, lo_src, 0.0)
    hi = jnp.where(lane >= HEAD_DIM, hi_src, 0.0)
    return lo, hi


def _build_keys(g, load_k, load_v, n_chunks, kf_ref, kaug, vte, vto, n_pad, chunk, ones_rows):
    fill = 1.0 if ones_rows else 0.0
    lane = lax.broadcasted_iota(jnp.int32, (chunk, LANES), 1)
    for gval in range(N_KV_GROUPS):
        @pl.when(g == gval)
        def _():
            def body(c, carry):
                start = pl.multiple_of(c * chunk, chunk)
                klo, khi = _place_group(load_k(start), gval == 0)
                vlo, vhi = _place_group(load_v(start), gval == 0)
                kaug[n_pad + c, 0, :, 0:LANES] = klo.astype(kaug.dtype)
                kaug[n_pad + c, 1, :, 0:LANES] = khi.astype(kaug.dtype)
                vte[n_pad + c] = jnp.where(lane < HEAD_DIM, vlo, fill).T.astype(vte.dtype)
                vto[n_pad + c] = jnp.where(lane >= HEAD_DIM, vhi, fill).T.astype(vto.dtype)
                return carry
            lax.fori_loop(0, n_chunks, body, 0)
    kaug[:, :, :, LANES:2 * LANES] = kf_ref[...]
    if n_pad:
        kaug[0:n_pad, :, :, 0:LANES] = jnp.zeros((n_pad, 2, chunk, LANES), kaug.dtype)
        vte[0:n_pad] = jnp.zeros((n_pad, LANES, chunk), vte.dtype)
        vto[0:n_pad] = jnp.zeros((n_pad, LANES, chunk), vto.dtype)


def _fill_queries(qaug, q_ref, r0, tq, feats):
    for p in range(PAIRS):
        qaug[p * tq:(p + 1) * tq, 0:LANES] = q_ref[0, r0:r0 + tq, p * LANES:(p + 1) * LANES]
        qaug[p * tq:(p + 1) * tq, LANES:2 * LANES] = feats[p].astype(qaug.dtype)


def _tile_pairs(x):
    return jnp.concatenate([x] * PAIRS, axis=1)


def _finish(o_ref, r0, tq, acct_e, acct_o, gate_ref, gate_col):
    lane = lax.broadcasted_iota(jnp.int32, (tq, LANES), 1)
    if gate_ref is not None:
        gsig = jax.nn.sigmoid(gate_ref[0, r0:r0 + tq, :])
    for p in range(PAIRS):
        cols = slice(p * tq, (p + 1) * tq)
        ae = acct_e[:, cols].T
        if acct_o is None:
            out = ae
        else:
            ao = acct_o[:, cols].T
            num = jnp.where(lane < HEAD_DIM, ae, ao)
            den = jnp.where(lane < HEAD_DIM, ae[:, HEAD_DIM:HEAD_DIM + 1], ao[:, 0:1])
            out = num / den
        if gate_ref is not None:
            ce = 3 * (2 * p) + gate_col
            co = 3 * (2 * p + 1) + gate_col
            out = out * jnp.where(lane < HEAD_DIM, gsig[:, ce:ce + 1], gsig[:, co:co + 1])
        o_ref[0, r0:r0 + tq, p * LANES:(p + 1) * LANES] = out.astype(o_ref.dtype)


def _banded_kernel(*refs, n_chunks, window, has_sink, gate_col):
    refs = list(refs)
    if has_sink:
        sink_ref, slope_ref = refs[0], refs[1]
        refs = refs[2:]
    q_ref, k_ref, v_ref, qf_ref, kf_ref = refs[:5]
    refs = refs[5:]
    gate_ref = None
    if gate_col is not None:
        gate_ref = refs[0]
        refs = refs[1:]
    o_ref, kaug, vte, vto, qaug = refs
    g = pl.program_id(1)
    tq = Q_SUB
    rows = PAIRS * tq
    ch = BAND_CHUNK
    n_pad = n_chunks - 1

    @pl.when(pl.program_id(2) == 0)
    def _():
        _build_keys(g, lambda s0: k_ref[0, pl.ds(s0, ch), :].astype(F32),
                    lambda s0: v_ref[0, pl.ds(s0, ch), :].astype(F32),
                    k_ref.shape[1] // ch, kf_ref, kaug, vte, vto, n_pad, ch, ones_rows=True)

    krow = lax.broadcasted_iota(jnp.int32, (ch, tq), 0)
    qcol = lax.broadcasted_iota(jnp.int32, (ch, tq), 1)
    row128 = lax.broadcasted_iota(jnp.int32, (LANES, rows), 0)
    for sub in range(q_ref.shape[1] // tq):
        i = pl.program_id(2) * (q_ref.shape[1] // tq) + sub
        _fill_queries(qaug.at[sub], q_ref, sub * tq, tq,
                      [jnp.broadcast_to(qf_ref[0, p:p + 1, :], (tq, LANES)) for p in range(PAIRS)])
        kc = kaug[pl.ds(i, n_chunks)].reshape(n_chunks * 2 * ch, 2 * LANES)
        st = _dot_nt(kc, qaug[sub])

        t = i * tq + qcol
        ev, od = [], []
        for u in range(n_chunks):
            spos = (i - n_pad + u) * ch + krow
            dist = t - spos
            vis = jnp.where(dist >= 0, jnp.where(dist < window, jnp.where(spos >= 0, 1.0, 0.0), 0.0), 0.0)
            bias = _tile_pairs(jnp.where(vis > 0.5, 0.0, NEG))
            ev.append(st[(2 * u) * ch:(2 * u + 1) * ch] + bias)
            od.append(st[(2 * u + 1) * ch:(2 * u + 2) * ch] + bias)

        tpos = (i * tq + lax.broadcasted_iota(jnp.int32, (1, tq), 1)).astype(F32)
        accs = []
        for h, slabs in enumerate((ev, od)):
            m = slabs[0].max(axis=0, keepdims=True)
            for x in slabs[1:]:
                m = jnp.maximum(m, x.max(axis=0, keepdims=True))
            if has_sink:
                sk = jnp.concatenate(
                    [sink_ref[g * HEADS_PER_GROUP + 2 * p + h]
                     + slope_ref[g * HEADS_PER_GROUP + 2 * p + h] * tpos for p in range(PAIRS)], axis=1)
                m = jnp.maximum(m, sk)
            vt = vte if h == 0 else vto
            acc = None
            for u, x in enumerate(slabs):
                d = _dot(vt[i + u], jnp.exp(x - m).astype(vt.dtype))
                acc = d if acc is None else acc + d
            if has_sink:
                sum_rows = (row128 >= HEAD_DIM) if h == 0 else (row128 < HEAD_DIM)
                acc = acc + jnp.where(sum_rows, jnp.exp(sk - m), 0.0)
            accs.append(acc)
        _finish(o_ref, sub * tq, tq, accs[0], accs[1], gate_ref, gate_col)


def _banded_attention(q, kv, k_col, v_col, qfeat, kfeat, window, sinks=None, slopes=None,
                      gates=None, gate_col=None):
    b, s, _ = q.shape
    tq = Q_BLOCK
    n_sub = tq // Q_SUB
    n_chunks = window // BAND_CHUNK + 1
    n_tot = kfeat.shape[0]
    has_sink = sinks is not None
    kern = functools.partial(_banded_kernel, n_chunks=n_chunks, window=window,
                             has_sink=has_sink, gate_col=gate_col)
    in_specs, args = [], []
    if has_sink:
        in_specs += [pl.BlockSpec(memory_space=pltpu.SMEM)] * 2
        args += [sinks, slopes]
    in_specs += [pl.BlockSpec((1, tq, PAIRS * LANES), lambda bi, g, i: (bi, i, g)),
                 pl.BlockSpec((1, s, LANES), lambda bi, g, i: (bi, 0, k_col)),
                 pl.BlockSpec((1, s, LANES), lambda bi, g, i: (bi, 0, v_col)),
                 pl.BlockSpec((1, PAIRS, LANES), lambda bi, g, i: (g, 0, 0)),
                 _const_spec(kfeat.shape)]
    args += [q, kv, kv, qfeat, kfeat]
    if gate_col is not None:
        in_specs.append(pl.BlockSpec((1, tq, LANES), lambda bi, g, i: (bi, i, g)))
        args.append(gates)
    return pl.pallas_call(
        kern,
        grid=(b, N_KV_GROUPS, s // tq),
        in_specs=in_specs,
        out_specs=pl.BlockSpec((1, tq, PAIRS * LANES), lambda bi, g, i: (bi, i, g)),
        out_shape=jax.ShapeDtypeStruct(q.shape, MXU_DT),
        scratch_shapes=[pltpu.VMEM((n_tot, 2, BAND_CHUNK, 2 * LANES), MXU_DT),
                        pltpu.VMEM((n_tot, LANES, BAND_CHUNK), MXU_DT),
                        pltpu.VMEM((n_tot, LANES, BAND_CHUNK), MXU_DT),
                        pltpu.VMEM((n_sub, PAIRS * Q_SUB, 2 * LANES), MXU_DT)],
        compiler_params=_params(("arbitrary", "arbitrary", "arbitrary")),
        name="banded_attention_w%d" % window,
    )(*args)


def _cmpsel_kernel(q_ref, k_ref, v_ref, qf_ref, kf_ref, ovt_ref, gate_ref, o_ref, mf_ref,
                   kaug, vte, vto, qaug):
    g = pl.program_id(1)
    ncp = k_ref.shape[2]

    @pl.when(pl.program_id(2) == 0)
    def _():
        _build_keys(g, lambda s0: k_ref[0, 0], lambda s0: v_ref[0, 0], 1, kf_ref, kaug, vte, vto,
                    0, ncp, ones_rows=False)

    n_sub = q_ref.shape[1] // Q_SUB
    for sub in range(n_sub):
        _cmpsel_block(pl.program_id(2) * n_sub + sub, sub, ncp, q_ref, qf_ref, ovt_ref, gate_ref,
                      o_ref, mf_ref, kaug, vte, vto, qaug)


def _cmpsel_block(i, sub, ncp, q_ref, qf_ref, ovt_ref, gate_ref, o_ref, mf_ref, kaug, vte, vto, qaug):
    tq = Q_SUB
    n_slc = NSA_SLC_LEN
    _fill_queries(qaug.at[sub], q_ref, sub * tq, tq,
                  [jnp.broadcast_to(qf_ref[0, p:p + 1, :], (tq, LANES)) for p in range(PAIRS)])
    st = _dot_nt(kaug[0].reshape(2 * ncp, 2 * LANES), qaug[sub])

    krow = lax.broadcasted_iota(jnp.int32, (ncp, tq), 0)
    qcol = lax.broadcasted_iota(jnp.int32, (ncp, tq), 1)
    t = i * tq + qcol
    cmp_end = krow * NSA_CMP_STRIDE + (NSA_CMP_LEN - 1)
    vis = jnp.where(t >= cmp_end, jnp.where(krow < ncp - 1, 1.0, 0.0), 0.0)
    bias = _tile_pairs(jnp.where(vis > 0.5, 0.0, NEG))

    pn = []
    for h in range(2):
        x = st[h * ncp:(h + 1) * ncp] + bias
        m = x.max(axis=0, keepdims=True)
        m = jnp.where(m > 0.5 * NEG, m, 0.0)
        e = jnp.exp(x - m)
        d = e.sum(axis=0, keepdims=True)
        pn.append(e * (1.0 / jnp.where(d > 0.0, d, 1.0)))
    acct = _dot(vte[0], pn[0].astype(vte.dtype)) + _dot(vto[0], pn[1].astype(vto.dtype))
    _finish(o_ref, sub * tq, tq, acct, None, gate_ref, 0)

    ps = pn[0] + pn[1]
    psum = ps[:, 0:tq]
    for p in range(1, PAIRS):
        psum = psum + ps[:, p * tq:(p + 1) * tq]
    p_hi = psum.astype(MXU_DT)
    p_lo = (psum - p_hi.astype(F32)).astype(MXU_DT)
    imp = (_dot(ovt_ref[...], p_hi) + _dot(ovt_ref[...], p_lo))[0:n_slc]
    blk = lax.broadcasted_iota(jnp.int32, (n_slc, tq), 0)
    tq_pos = i * tq + lax.broadcasted_iota(jnp.int32, (n_slc, tq), 1)
    cur = lax.shift_right_logical(tq_pos, int(math.log2(NSA_SLC_LEN)))
    forced = jnp.where(blk == 0, 1.0, jnp.where(blk == cur, 1.0, jnp.where(blk == cur - 1, 1.0, 0.0)))
    valid = blk <= cur
    score = jnp.where(valid, jnp.where(forced > 0.5, jnp.inf, imp), -jnp.inf)
    groups = [score[8 * v:8 * v + 8] for v in range(n_slc // 8)]
    ranks = [jnp.zeros((8, tq), F32) for _ in groups]
    sublane = lax.broadcasted_iota(jnp.int32, (8, tq), 0)
    for j in range(n_slc):
        rj = score[j:j + 1, :]
        for v in range(n_slc // 8):
            if 8 * v > j:
                beats = jnp.where(rj >= groups[v], 1.0, 0.0)
            elif 8 * v + 7 < j:
                beats = jnp.where(rj > groups[v], 1.0, 0.0)
            else:
                beats = jnp.where(sublane > (j - 8 * v), jnp.where(rj >= groups[v], 1.0, 0.0),
                                  jnp.where(rj > groups[v], 1.0, 0.0))
            ranks[v] = ranks[v] + beats
    rank = jnp.concatenate(ranks, axis=0)
    keep = jnp.where(valid, jnp.where(rank < float(NSA_TOPK), 1.0, 0.0), 0.0)
    feat = jnp.where(keep > 0.5, 0.0, NEG)
    feat = jnp.concatenate([feat, jnp.zeros((LANES - n_slc, tq), F32)], axis=0)
    mf_ref[0, 0, sub * tq:(sub + 1) * tq, :] = feat.T.astype(mf_ref.dtype)


def _compressed_and_select(q, kvcmp, qfeat, kfeat, ovt, gates):
    b, s, _ = q.shape
    tq = Q_BLOCK
    ncp = kvcmp.shape[2]
    return pl.pallas_call(
        _cmpsel_kernel,
        grid=(b, N_KV_GROUPS, s // tq),
        in_specs=[pl.BlockSpec((1, tq, PAIRS * LANES), lambda bi, g, i: (bi, i, g)),
                  pl.BlockSpec((1, 1, ncp, LANES), lambda bi, g, i: (bi, 0, 0, 0)),
                  pl.BlockSpec((1, 1, ncp, LANES), lambda bi, g, i: (bi, 1, 0, 0)),
                  pl.BlockSpec((1, PAIRS, LANES), lambda bi, g, i: (g, 0, 0)),
                  _const_spec(kfeat.shape), _const_spec(ovt.shape),
                  pl.BlockSpec((1, tq, LANES), lambda bi, g, i: (bi, i, g))],
        out_specs=[pl.BlockSpec((1, tq, PAIRS * LANES), lambda bi, g, i: (bi, i, g)),
                   pl.BlockSpec((1, 1, tq, LANES), lambda bi, g, i: (bi, g, i, 0))],
        out_shape=[jax.ShapeDtypeStruct(q.shape, MXU_DT),
                   jax.ShapeDtypeStruct((b, N_KV_GROUPS, s, LANES), MXU_DT)],
        scratch_shapes=[pltpu.VMEM((1, 2, ncp, 2 * LANES), MXU_DT),
                        pltpu.VMEM((1, LANES, ncp), MXU_DT),
                        pltpu.VMEM((1, LANES, ncp), MXU_DT),
                        pltpu.VMEM((tq // Q_SUB, PAIRS * Q_SUB, 2 * LANES), MXU_DT)],
        compiler_params=_params(("arbitrary", "arbitrary", "arbitrary")),
        name="nsa_compressed_select",
    )(q, kvcmp, kvcmp, qfeat, kfeat, ovt, gates)


def _selected_kernel(q_ref, k_ref, v_ref, qf_ref, mf_ref, kf_ref, gate_ref, o_ref,
                     kaug, vte, vto, qaug, acc_e, acc_o, st_a, st_b):
    g = pl.program_id(1)
    i = pl.program_id(2)
    tq = q_ref.shape[1]
    ch = SLC_CHUNK
    rows = PAIRS * tq

    @pl.when(i == 0)
    def _():
        _build_keys(g, lambda s0: k_ref[0, pl.ds(s0, ch), :].astype(F32),
                    lambda s0: v_ref[0, pl.ds(s0, ch), :].astype(F32),
                    k_ref.shape[1] // ch, kf_ref, kaug, vte, vto, 0, ch, ones_rows=True)

    mask_feat = mf_ref[0, 0].astype(F32)
    _fill_queries(qaug, q_ref, 0, tq, [mask_feat + qf_ref[0, p:p + 1, :] for p in range(PAIRS)])
    acc_e[...] = jnp.zeros((LANES, rows), F32)
    acc_o[...] = jnp.zeros((LANES, rows), F32)

    def update(x, m_old, acc_ref, vt_c):
        m_new = jnp.maximum(m_old, x.max(axis=0, keepdims=True))
        alpha = jnp.exp(m_old - m_new)
        p = jnp.exp(x - m_new).astype(vt_c.dtype)
        acc_ref[...] = alpha * acc_ref[...] + _dot(vt_c, p)
        return m_new

    def scores_into(buf, c):
        buf[...] = _dot_nt(kaug[c].reshape(2 * ch, 2 * LANES), qaug[...])

    def consume(buf, c, ms, causal):
        xe, xo = buf[0:ch, :], buf[ch:2 * ch, :]
        if causal:
            krow = lax.broadcasted_iota(jnp.int32, (ch, tq), 0)
            qcol = lax.broadcasted_iota(jnp.int32, (ch, tq), 1)
            bias = _tile_pairs(jnp.where(c * ch + krow <= i * tq + qcol, 0.0, NEG))
            xe, xo = xe + bias, xo + bias
        return update(xe, ms[0], acc_e, vte[c]), update(xo, ms[1], acc_o, vto[c])

    def pair(k, ms):
        c = 2 * k
        scores_into(st_b, c + 1)
        ms = consume(st_a, c, ms, False)
        scores_into(st_a, c + 2)
        return consume(st_b, c + 1, ms, False)

    c_diag = (i * tq) // ch
    m0 = jnp.full((1, rows), M_INIT, F32)
    scores_into(st_a, 0)
    ms = lax.fori_loop(0, c_diag // 2, pair, (m0, m0))

    @pl.when(c_diag % 2 == 0)
    def _():
        consume(st_a, c_diag, ms, True)

    @pl.when(c_diag % 2 == 1)
    def _():
        scores_into(st_b, c_diag)
        consume(st_b, c_diag, consume(st_a, c_diag - 1, ms, False), True)

    _finish(o_ref, 0, tq, acc_e[...], acc_o[...], gate_ref, 1)


def _selected_attention(q, kv, k_col, v_col, qfeat, maskfeat, kfeat, gates):
    b, s, _ = q.shape
    tq = Q_BLOCK
    n_tot = kfeat.shape[0]
    return pl.pallas_call(
        _selected_kernel,
        grid=(b, N_KV_GROUPS, s // tq),
        in_specs=[pl.BlockSpec((1, tq, PAIRS * LANES), lambda bi, g, i: (bi, i, g)),
                  pl.BlockSpec((1, s, LANES), lambda bi, g, i: (bi, 0, k_col)),
                  pl.BlockSpec((1, s, LANES), lambda bi, g, i: (bi, 0, v_col)),
                  pl.BlockSpec((1, PAIRS, LANES), lambda bi, g, i: (g, 0, 0)),
                  pl.BlockSpec((1, 1, tq, LANES), lambda bi, g, i: (bi, g, i, 0)),
                  _const_spec(kfeat.shape),
                  pl.BlockSpec((1, tq, LANES), lambda bi, g, i: (bi, i, g))],
        out_specs=pl.BlockSpec((1, tq, PAIRS * LANES), lambda bi, g, i: (bi, i, g)),
        out_shape=jax.ShapeDtypeStruct(q.shape, MXU_DT),
        scratch_shapes=[pltpu.VMEM((n_tot, 2, SLC_CHUNK, 2 * LANES), MXU_DT),
                        pltpu.VMEM((n_tot, LANES, SLC_CHUNK), MXU_DT),
                        pltpu.VMEM((n_tot, LANES, SLC_CHUNK), MXU_DT),
                        pltpu.VMEM((PAIRS * tq, 2 * LANES), MXU_DT),
                        pltpu.VMEM((LANES, PAIRS * tq), F32),
                        pltpu.VMEM((LANES, PAIRS * tq), F32),
                        pltpu.VMEM((2 * SLC_CHUNK, PAIRS * tq), F32),
                        pltpu.VMEM((2 * SLC_CHUNK, PAIRS * tq), F32)],
        compiler_params=_params(("arbitrary", "arbitrary", "arbitrary")),
        name="nsa_selected",
    )(q, kv, kv, qfeat, maskfeat, kfeat, gates)


def _outproj_kernel(oa_ref, oc_ref, os_ref, ow_ref, x_ref, w_ref, b_ref, g_ref, beta_ref,
                    wr_ref, br_ref, x1_ref, route_ref, *, alpha):
    half = oa_ref.shape[1]
    on = (oc_ref[...].astype(F32) + os_ref[...].astype(F32) + ow_ref[...].astype(F32)).astype(MXU_DT)
    mix = _dot(oa_ref[...], w_ref[0:half]) + _dot(on, w_ref[half:]) + b_ref[...]
    x1 = _layer_norm(alpha * x_ref[...] + mix, g_ref[...], beta_ref[...])
    x1_ref[...] = x1

    x_hi = x1.astype(MXU_DT)
    x_lo = (x1 - x_hi.astype(F32)).astype(MXU_DT)
    logits = (_dot(x_hi, wr_ref[0]) + _dot(x_hi, wr_ref[1]) + _dot(x_lo, wr_ref[0])) + br_ref[...]
    lane = lax.broadcasted_iota(jnp.int32, logits.shape, 1)
    lanef = lane.astype(F32)
    big = float(4 * LANES)
    gl = jnp.where(lane < N_GROUPS, logits, -jnp.inf)
    gmax = gl.max(axis=1, keepdims=True)
    gsel = jnp.where(gl == gmax, lanef, big).min(axis=1, keepdims=True)
    g_w = 1.0 / jnp.exp(gl - gmax).sum(axis=1, keepdims=True)
    lo = N_GROUPS + gsel * EXPERTS_PER_GROUP
    el = jnp.where(lanef >= lo, jnp.where(lanef < lo + EXPERTS_PER_GROUP, logits, -jnp.inf), -jnp.inf)
    v1 = el.max(axis=1, keepdims=True)
    i1 = jnp.where(el == v1, lanef, big).min(axis=1, keepdims=True)
    el2 = jnp.where(lanef == i1, -jnp.inf, el)
    v2 = el2.max(axis=1, keepdims=True)
    i2 = jnp.where(el2 == v2, lanef, big).min(axis=1, keepdims=True)
    e2 = jnp.exp(v2 - v1)
    w1 = g_w / (1.0 + e2)
    w2 = g_w * e2 / (1.0 + e2)
    route_ref[...] = jnp.where(lane == 0, i1 - N_GROUPS,
                               jnp.where(lane == 1, i2 - N_GROUPS,
                                         jnp.where(lane == 2, w1, jnp.where(lane == 3, w2, 0.0))))


def _output_projection(oa, oc, os_, ow, x2d, w, b, ln_g, ln_b, wr, br, alpha):
    n, d = x2d.shape
    half = oa.shape[1]
    tm = min(PROJ_TILE, n)
    row = lambda i: (i, 0)
    return pl.pallas_call(
        functools.partial(_outproj_kernel, alpha=alpha),
        grid=(n // tm,),
        in_specs=[pl.BlockSpec((tm, half), row)] * 4 + [pl.BlockSpec((tm, d), row)]
        + [_const_spec(a.shape) for a in (w, b, ln_g, ln_b, wr, br)],
        out_specs=[pl.BlockSpec((tm, d), row), pl.BlockSpec((tm, LANES), row)],
        out_shape=[jax.ShapeDtypeStruct((n, d), F32), jax.ShapeDtypeStruct((n, LANES), F32)],
        compiler_params=_params(("arbitrary",)),
        name="output_projection_ln_route",
    )(oa, oc, os_, ow, x2d, w, b, ln_g, ln_b, wr, br)


def _row_copy(src_hbm, src_row, dst, j, u, sem):
    return pltpu.make_async_copy(src_hbm.at[pl.ds(src_row, 1)], dst.at[j, pl.ds(u, 1)], sem)


def _rows_start(src_hbm, idx_ref, base, dst, sem, n_groups):
    def body(j, carry):
        for u in range(SUBLANES):
            _row_copy(src_hbm, idx_ref[base + j * SUBLANES + u], dst, j, u, sem).start(priority=u % 2)
        return carry
    lax.fori_loop(0, n_groups, body, 0)


def _rows_wait(src_hbm, dst, sem, n_groups):
    def body(j, carry):
        for u in range(SUBLANES):
            _row_copy(src_hbm, 0, dst, j, u, sem).wait()
        return carry
    lax.fori_loop(0, n_groups, body, 0)


def _expert_kernel(blk_e_ref, n_used_ref, n_grp_ref, row_tok_ref, x_hbm, wg_ref, wu_ref, wd_ref, o_ref,
                   xbuf, sem, wg_b, wu_b, wd_b):
    i = pl.program_id(0)
    n_used = n_used_ref[0]
    groups, d = xbuf.shape[1], xbuf.shape[3]
    t = groups * SUBLANES
    used = i < n_used
    slot = lax.rem(i, 2)
    new_expert = jnp.logical_or(i == 0, blk_e_ref[i] != blk_e_ref[jnp.maximum(i - 1, 0)])

    @pl.when(i == 0)
    def _():
        xbuf[...] = jnp.zeros(xbuf.shape, xbuf.dtype)
        _rows_start(x_hbm, row_tok_ref, 0, xbuf.at[0], sem.at[0], n_grp_ref[0])

    @pl.when(i + 1 < n_used)
    def _():
        _rows_start(x_hbm, row_tok_ref, (i + 1) * t, xbuf.at[1 - slot], sem.at[1 - slot], n_grp_ref[i + 1])

    @pl.when(jnp.logical_and(used, new_expert))
    def _():
        wg_b[...] = wg_ref[0, 0].astype(wg_b.dtype)
        wu_b[...] = wu_ref[0, 0].astype(wu_b.dtype)
        wd_b[...] = wd_ref[0, 0].astype(wd_b.dtype)

    @pl.when(used)
    def _():
        _rows_wait(x_hbm, xbuf.at[slot], sem.at[slot], n_grp_ref[i])
        xb = xbuf[slot].reshape(t, d).astype(MXU_DT)
        gate = _dot(xb, wg_b[...])
        up = _dot(xb, wu_b[...])
        hidden = (gate * jax.nn.sigmoid(gate) * up).astype(MXU_DT)
        o_ref[...] = _dot(hidden, wd_b[...])

    @pl.when(jnp.logical_not(used))
    def _():
        o_ref[...] = jnp.zeros(o_ref.shape, o_ref.dtype)


def _expert_ffn(x1, row_tok, blk_e, n_used, n_grp, wg, wu, wd, layer):
    d = x1.shape[1]
    p = row_tok.shape[0]
    t = EXPERT_ROWS
    hid = wg.shape[3]
    wmap = lambda i, be, nu, ng, rt: (layer, be[i], 0, 0)
    return pl.pallas_call(
        _expert_kernel,
        grid_spec=pltpu.PrefetchScalarGridSpec(
            num_scalar_prefetch=4,
            grid=(p // t,),
            in_specs=[pl.BlockSpec(memory_space=pl.ANY),
                      pl.BlockSpec((1, 1, d, hid), wmap),
                      pl.BlockSpec((1, 1, d, hid), wmap),
                      pl.BlockSpec((1, 1, hid, d), wmap)],
            out_specs=pl.BlockSpec((t, d), lambda i, be, nu, ng, rt: (i, 0)),
            scratch_shapes=[pltpu.VMEM((2, t // SUBLANES, SUBLANES, d), F32), pltpu.SemaphoreType.DMA((2,)),
                            pltpu.VMEM((d, hid), MXU_DT), pltpu.VMEM((d, hid), MXU_DT),
                            pltpu.VMEM((hid, d), MXU_DT)]),
        out_shape=jax.ShapeDtypeStruct((p, d), F32),
        compiler_params=_params(("arbitrary",)),
        name="expert_ffn",
    )(blk_e, n_used, n_grp, row_tok, x1, wg, wu, wd)


def _combine_kernel(dest_ref, ys_hbm, route_ref, x_ref, g_ref, b_ref, o_ref, ybuf, sem, *, alpha, n_tok):
    i = pl.program_id(0)
    tm, d = x_ref.shape
    groups = tm // SUBLANES
    slot = lax.rem(i, 2)

    def start(blk, s):
        for k in range(2):
            _rows_start(ys_hbm, dest_ref, k * n_tok + blk * tm, ybuf.at[s, k], sem.at[s], groups)

    @pl.when(i == 0)
    def _():
        start(0, 0)

    @pl.when(i + 1 < pl.num_programs(0))
    def _():
        start(i + 1, 1 - slot)

    for k in range(2):
        _rows_wait(ys_hbm, ybuf.at[slot, k], sem.at[slot], groups)
    r = route_ref[...]
    y = r[:, 2:3] * ybuf[slot, 0].reshape(tm, d) + r[:, 3:4] * ybuf[slot, 1].reshape(tm, d)
    o_ref[...] = _layer_norm(alpha * x_ref[...] + y, g_ref[...], b_ref[...])


def _combine(ys, dest_kmajor, route, x1, ln_g, ln_b, alpha):
    n, d = x1.shape
    tm = min(ROW_TILE, n)
    row = lambda i, dst: (i, 0)
    zero = lambda i, dst: (0, 0)
    return pl.pallas_call(
        functools.partial(_combine_kernel, alpha=alpha, n_tok=n),
        grid_spec=pltpu.PrefetchScalarGridSpec(
            num_scalar_prefetch=1,
            grid=(n // tm,),
            in_specs=[pl.BlockSpec(memory_space=pl.ANY),
                      pl.BlockSpec((tm, LANES), row), pl.BlockSpec((tm, d), row),
                      pl.BlockSpec(ln_g.shape, zero), pl.BlockSpec(ln_b.shape, zero)],
            out_specs=pl.BlockSpec((tm, d), row),
            scratch_shapes=[pltpu.VMEM((2, 2, tm // SUBLANES, SUBLANES, d), F32),
                            pltpu.SemaphoreType.DMA((2,))]),
        out_shape=jax.ShapeDtypeStruct((n, d), F32),
        compiler_params=_params(("arbitrary",)),
        name="combine_ln",
    )(dest_kmajor, ys, route, x1, ln_g, ln_b)


def _dispatch_plan(route, n_tokens):
    t = EXPERT_ROWS
    a = 2 * n_tokens
    eid = route[:, 0:2].astype(jnp.int32).reshape(a)
    onehot = (eid[:, None] == jnp.arange(N_EXPERTS, dtype=jnp.int32)[None, :]).astype(jnp.int32)
    counts = onehot.sum(axis=0)
    before = jnp.cumsum(onehot, axis=0) - onehot
    rank = jnp.take_along_axis(before, eid[:, None], axis=1)[:, 0]
    padded = (counts + t - 1) // t * t
    pend = jnp.cumsum(padded)
    pstart = pend - padded
    dest = (pstart[eid] + rank).astype(jnp.int32)
    p_rows = a + N_EXPERTS * t
    filler = jnp.arange(p_rows, dtype=jnp.int32) % n_tokens
    row_tok = filler.at[dest].set(jnp.arange(a, dtype=jnp.int32) // 2)
    blk_start = jnp.arange(p_rows // t, dtype=jnp.int32) * t
    blk_e = jnp.minimum((pend[None, :] <= blk_start[:, None]).astype(jnp.int32).sum(axis=1), N_EXPERTS - 1)
    n_used = (pend[-1] // t).astype(jnp.int32).reshape(1)
    valid = jnp.clip((pstart + counts)[blk_e] - blk_start, 0, t)
    n_grp = ((valid + SUBLANES - 1) // SUBLANES).astype(jnp.int32)
    dest_kmajor = dest.reshape(n_tokens, 2).T.reshape(a)
    return dest_kmajor, row_tok, blk_e.astype(jnp.int32), n_used, n_grp


def kernel(x, w_in, b_in, swa_sinks, cmp_pe_k, cmp_w1_k, cmp_w2_k, cmp_pe_v, cmp_w1_v, cmp_w2_v,
           w_out, b_out, ln1_g, ln1_b, w_group, b_group, w_expert, b_expert, we_gate, we_up,
           we_down, ln2_g, ln2_b):
    bsz, seq, d_model = x.shape
    depth = w_in.shape[0]
    n_tok = bsz * seq
    n_heads = d_model // HEAD_DIM
    q_w = n_heads // 2 * HEAD_DIM
    kv_w = N_KV_GROUPS * HEAD_DIM
    alpha = (2.0 * depth) ** 0.25
    scale = HEAD_DIM ** -0.5
    assert q_w == N_KV_GROUPS * PAIRS * LANES and seq % SLC_CHUNK == 0

    slopes_a, slopes_n = _alibi_slopes(n_heads)
    qfeat_a = jnp.asarray(_query_feats(slopes_a), F32)
    qfeat_n = jnp.asarray(_query_feats(slopes_n), F32)
    pos = np.arange(seq)
    kfeat_swa = jnp.asarray(_key_feats(pos, SWA_WINDOW // BAND_CHUNK, BAND_CHUNK, False), MXU_DT)
    kfeat_win = jnp.asarray(_key_feats(pos, NSA_WINDOW // BAND_CHUNK, BAND_CHUNK, False), MXU_DT)
    kfeat_slc = jnp.asarray(_key_feats(pos, 0, SLC_CHUNK, True), MXU_DT)
    n_cmp = (seq - NSA_CMP_LEN) // NSA_CMP_STRIDE + 1
    n_cmp_pad = seq // NSA_CMP_STRIDE
    cmp_end = np.arange(n_cmp_pad) * NSA_CMP_STRIDE + NSA_CMP_LEN - 1
    kfeat_cmp = jnp.asarray(_key_feats(cmp_end, 0, n_cmp_pad, False), MXU_DT)
    n_slc = seq // NSA_SLC_LEN
    cs = np.arange(n_cmp_pad)[None, :] * NSA_CMP_STRIDE
    ss = np.arange(LANES)[:, None] * NSA_SLC_LEN
    ovt = ((cs < ss + NSA_SLC_LEN) & (cs + NSA_CMP_LEN - 1 >= ss)
           & (np.arange(n_cmp_pad)[None, :] < n_cmp) & (np.arange(LANES)[:, None] < n_slc))
    ovt = jnp.asarray(ovt.astype(np.float32), MXU_DT)
    slopes_a_s = jnp.asarray(slopes_a, F32)

    o_qa, o_ka = 0, q_w
    o_qn = q_w + 2 * kv_w
    o_kvn = o_qn + q_w
    o_gn = o_kvn + 6 * kv_w
    gate_w = 3 * HEADS_PER_GROUP

    h = x.reshape(n_tok, d_model)
    for l in range(depth):
        wl, bl = w_in[l], b_in[l]

        def seg(a, lo, hi):
            return a[..., lo:hi]

        def gate_cols(a):
            pad = [(0, 0)] * (a.ndim - 1) + [(0, LANES - gate_w)]
            return jnp.concatenate([jnp.pad(seg(a, o_gn + g * gate_w, o_gn + (g + 1) * gate_w), pad)
                                    for g in range(N_KV_GROUPS)], axis=-1)

        def regroup(a):
            return jnp.concatenate([seg(a, o_qa, o_qa + q_w) * scale, seg(a, o_qn, o_qn + q_w) * scale,
                                    seg(a, o_ka, o_ka + 2 * kv_w), seg(a, o_kvn, o_kvn + 6 * kv_w),
                                    gate_cols(a)], axis=-1)

        w_r = regroup(wl).astype(MXU_DT)
        b_r = regroup(bl)[None, :]
        qa, qn, kv, gates = _input_projection(h, w_r, b_r)
        qa = qa.reshape(bsz, seq, q_w)
        qn = qn.reshape(bsz, seq, q_w)
        kv = kv.reshape(bsz, seq, 8 * kv_w)
        gates = gates.reshape(bsz, seq, 2 * LANES)

        o_a = _banded_attention(qa, kv, 0, 1, qfeat_a, kfeat_swa, SWA_WINDOW,
                                sinks=swa_sinks[l], slopes=slopes_a_s)

        kvc = kv[:, :, 2 * kv_w:4 * kv_w].reshape(bsz, n_cmp_pad, NSA_CMP_STRIDE, 2, N_KV_GROUPS, HEAD_DIM)
        kvc = kvc.transpose(0, 3, 4, 1, 2, 5).reshape(bsz, 2, N_KV_GROUPS, n_cmp_pad, NSA_CMP_STRIDE * HEAD_DIM)
        pe = jnp.stack([cmp_pe_k[l], cmp_pe_v[l]]).reshape(2, 1, NSA_CMP_LEN * HEAD_DIM)
        pe = jnp.broadcast_to(pe, (2, 8, NSA_CMP_LEN * HEAD_DIM))
        w1 = jnp.stack([cmp_w1_k[l], cmp_w1_v[l]]).astype(MXU_DT)
        w2 = jnp.stack([cmp_w2_k[l], cmp_w2_v[l]]).astype(MXU_DT)
        kvcmp = _compress(kvc, pe, w1, w2)

        o_cmp, maskfeat = _compressed_and_select(qn, kvcmp, qfeat_n, kfeat_cmp, ovt, gates)
        o_slc = _selected_attention(qn, kv, 4, 5, qfeat_n, maskfeat, kfeat_slc, gates)
        o_win = _banded_attention(qn, kv, 6, 7, qfeat_n, kfeat_win, NSA_WINDOW,
                                  gates=gates, gate_col=2)

        wr = jnp.concatenate([w_group[l], w_expert[l],
                              jnp.zeros((d_model, LANES - N_GROUPS - N_EXPERTS), F32)], axis=1)
        br = jnp.concatenate([b_group[l], b_expert[l],
                              jnp.zeros((LANES - N_GROUPS - N_EXPERTS,), F32)])[None, :]
        wr_hi = wr.astype(MXU_DT)
        wr = jnp.stack([wr_hi, (wr - wr_hi.astype(F32)).astype(MXU_DT)])
        x1, route = _output_projection(
            o_a.reshape(n_tok, q_w), o_cmp.reshape(n_tok, q_w), o_slc.reshape(n_tok, q_w),
            o_win.reshape(n_tok, q_w), h, w_out[l].astype(MXU_DT), b_out[l][None, :],
            ln1_g[l][None, :], ln1_b[l][None, :], wr, br, alpha)

        dest_kmajor, row_tok, blk_e, n_used, n_grp = _dispatch_plan(route, n_tok)
        ys = _expert_ffn(x1, row_tok, blk_e, n_used, n_grp, we_gate, we_up, we_down, l)
        h = _combine(ys, dest_kmajor, route, x1, ln2_g[l][None, :], ln2_b[l][None, :], alpha)
    return h.reshape(bsz, seq, d_model)
```

```python
import functools
import math

import numpy as np
import jax
import jax.numpy as jnp
from jax import lax
from jax.experimental import pallas as pl
from jax.experimental.pallas import tpu as pltpu

F32 = jnp.float32
MXU_DT = jnp.bfloat16

HEAD_DIM = 64
HEADS_PER_GROUP = 8
PAIRS = HEADS_PER_GROUP // 2
N_KV_GROUPS = 2
SWA_WINDOW = 128
NSA_WINDOW = 512
NSA_CMP_LEN = 32
NSA_CMP_STRIDE = 16
NSA_CMP_HIDDEN = 256
NSA_SLC_LEN = 64
NSA_TOPK = 16
N_GROUPS = 8
EXPERTS_PER_GROUP = 8
N_EXPERTS = N_GROUPS * EXPERTS_PER_GROUP
EXPERT_HIDDEN = 512
LN_EPS = 1e-5

LANES = 128
SUBLANES = 8
V_ROWS = 80
Q_BLOCK = 512
Q_SUB = 128
BAND_CHUNK = 128
SLC_CHUNK = 256
EXPERT_ROWS = 256
ROW_TILE = 256
PROJ_TILE = 512
NEG = -1e30
M_INIT = -5e29
VMEM_LIMIT = 56 * 1024 * 1024

FEAT_EVEN = 64
FEAT_ODD = 72


def _alibi_slopes(n_heads_total):
    n = n_heads_total
    s = np.exp2(-8.0 * np.arange(1, n + 1, dtype=np.float32) / np.float32(n)).astype(np.float32)
    return s[0::2], s[1::2]


def _bf16_parts(v):
    v = np.asarray(v, np.float32)
    a = v.astype(jnp.bfloat16).astype(np.float32)
    r = (v - a).astype(np.float32)
    b = r.astype(jnp.bfloat16).astype(np.float32)
    c = (r - b).astype(np.float32)
    return a, b, c


def _query_feats(slopes):
    out = np.zeros((N_KV_GROUPS, PAIRS, LANES), np.float32)
    a, b, c = _bf16_parts(slopes)
    for g in range(N_KV_GROUPS):
        for p in range(PAIRS):
            for h, base in ((0, FEAT_EVEN), (1, FEAT_ODD)):
                idx = g * HEADS_PER_GROUP + 2 * p + h
                out[g, p, base:base + 6] = [a[idx], b[idx], c[idx], a[idx], b[idx], c[idx]]
    return out


def _key_feats(positions, n_pad_chunks, chunk, with_blocks):
    pos = np.asarray(positions, np.int64)
    n = pos.shape[0]
    assert n % chunk == 0
    hi = (pos // 64 * 64).astype(np.float32)
    lo = (pos % 64).astype(np.float32)
    f = np.zeros((n, 2, LANES), np.float32)
    for v, base in ((0, FEAT_EVEN), (1, FEAT_ODD)):
        f[:, v, base:base + 3] = hi[:, None]
        f[:, v, base + 3:base + 6] = lo[:, None]
    if with_blocks:
        blk = pos // NSA_SLC_LEN
        for v in range(2):
            f[np.arange(n), v, blk] = 1.0
    f = f.reshape(n // chunk, chunk, 2, LANES).transpose(0, 2, 1, 3)
    if n_pad_chunks:
        f = np.concatenate([np.zeros((n_pad_chunks,) + f.shape[1:], np.float32), f], axis=0)
    return f


def _params(sem, vmem=VMEM_LIMIT):
    return pltpu.CompilerParams(dimension_semantics=sem, vmem_limit_bytes=vmem)


def _const_spec(shape):
    nd = len(shape)
    return pl.BlockSpec(shape, lambda *_: (0,) * nd, pipeline_mode=pl.Buffered(1))


def _dot(a, b):
    return jnp.dot(a, b, preferred_element_type=F32)


def _dot_nt(a, b):
    return lax.dot_general(a, b, (((1,), (1,)), ((), ())), preferred_element_type=F32)


def _layer_norm(h, g, b):
    mu = jnp.mean(h, axis=-1, keepdims=True)
    d = h - mu
    var = jnp.mean(d * d, axis=-1, keepdims=True)
    return d * lax.rsqrt(var + LN_EPS) * g + b


def _inproj_kernel(x_ref, w_ref, b_ref, qa_ref, qn_ref, kv_ref, gt_ref):
    xb = x_ref[...].astype(MXU_DT)
    width = qa_ref.shape[1]
    for j, o_ref in enumerate((qa_ref, qn_ref, kv_ref)):
        acc = _dot(xb, w_ref[:, j * width:(j + 1) * width]) + b_ref[:, j * width:(j + 1) * width]
        o_ref[...] = acc.astype(o_ref.dtype)
    gt_ref[...] = _dot(xb, w_ref[:, 3 * width:]) + b_ref[:, 3 * width:]


def _input_projection(x2d, w, b):
    n, d = x2d.shape
    width = (w.shape[1] - 2 * LANES) // 3
    tm = min(PROJ_TILE, n)
    row = lambda i: (i, 0)
    return pl.pallas_call(
        _inproj_kernel,
        grid=(n // tm,),
        in_specs=[pl.BlockSpec((tm, d), row), _const_spec(w.shape), _const_spec(b.shape)],
        out_specs=[pl.BlockSpec((tm, width), row)] * 3 + [pl.BlockSpec((tm, 2 * LANES), row)],
        out_shape=[jax.ShapeDtypeStruct((n, width), MXU_DT)] * 3
        + [jax.ShapeDtypeStruct((n, 2 * LANES), F32)],
        compiler_params=_params(("arbitrary",)),
        name="input_projection",
    )(x2d, w, b)


def _gelu_tanh(x):
    c = math.sqrt(2.0 / math.pi)
    return 0.5 * x * (1.0 + jnp.tanh(c * (x + 0.044715 * (x * x * x))))


def _compress_kernel(c_ref, pe_ref, w1_ref, w2_ref, o_ref):
    half = c_ref.shape[-1]
    w1 = w1_ref[0]
    pe = pe_ref[0]
    pe_hi = pe.astype(MXU_DT)
    pe_lo = (pe - pe_hi.astype(F32)).astype(MXU_DT)
    base = (_dot(pe_hi, w1) + _dot(pe_lo, w1))[0:1]
    for g in range(N_KV_GROUPS):
        c = c_ref[0, 0, g]
        top = _dot(c, w1[:half])
        bot = _dot(c, w1[half:])
        n_rows = bot.shape[0]
        hidden = top + pltpu.roll(bot, n_rows - 1, 0) + base
        act = _gelu_tanh(hidden)
        o_ref[0, 0, :, g * HEAD_DIM:(g + 1) * HEAD_DIM] = _dot(act.astype(MXU_DT), w2_ref[0])


def _compress(chunks, pe, w1, w2):
    b, two, g, nch, half = chunks.shape
    return pl.pallas_call(
        _compress_kernel,
        grid=(b, two),
        in_specs=[pl.BlockSpec((1, 1, g, nch, half), lambda i, j: (i, j, 0, 0, 0)),
                  pl.BlockSpec((1, 8, 2 * half), lambda i, j: (j, 0, 0)),
                  pl.BlockSpec((1, 2 * half, NSA_CMP_HIDDEN), lambda i, j: (j, 0, 0)),
                  pl.BlockSpec((1, NSA_CMP_HIDDEN, HEAD_DIM), lambda i, j: (j, 0, 0))],
        out_specs=pl.BlockSpec((1, 1, nch, LANES), lambda i, j: (i, j, 0, 0)),
        out_shape=jax.ShapeDtypeStruct((b, two, nch, LANES), F32),
        compiler_params=_params(("arbitrary", "arbitrary")),
        name="nsa_compress",
    )(chunks, pe, w1, w2)


def _place_group(x, g_is_zero):
    lane = lax.broadcasted_iota(jnp.int32, x.shape, 1)
    rolled = pltpu.roll(x, HEAD_DIM, 1)
    lo_src, hi_src = (x, rolled) if g_is_zero else (rolled, x)
    lo = jnp.where(lane < HEAD_DIM, lo_src, 0.0)
    hi = jnp.where(lane >= HEAD_DIM, hi_src, 0.0)
    return lo, hi


def _build_keys(g, load_k, load_v, n_chunks, kf_ref, kaug, vt, n_pad, chunk):
    ones = jnp.ones((V_ROWS - HEAD_DIM, chunk), F32)
    for gval in range(N_KV_GROUPS):
        @pl.when(g == gval)
        def _():
            def body(c, carry):
                start = pl.multiple_of(c * chunk, chunk)
                klo, khi = _place_group(load_k(start), gval == 0)
                vlo, _ = _place_group(load_v(start), gval == 0)
                kaug[n_pad + c, 0, :, 0:LANES] = klo.astype(kaug.dtype)
                kaug[n_pad + c, 1, :, 0:LANES] = khi.astype(kaug.dtype)
                vt[n_pad + c] = jnp.concatenate([vlo.T[0:HEAD_DIM], ones], axis=0).astype(vt.dtype)
                return carry
            lax.fori_loop(0, n_chunks, body, 0)
    kaug[:, :, :, LANES:2 * LANES] = kf_ref[...]
    if n_pad:
        kaug[0:n_pad, :, :, 0:LANES] = jnp.zeros((n_pad, 2, chunk, LANES), kaug.dtype)
        vt[0:n_pad] = jnp.zeros((n_pad, V_ROWS, chunk), vt.dtype)


def _fill_queries(qaug, q_ref, r0, tq, feats):
    for p in range(PAIRS):
        qaug[p * tq:(p + 1) * tq, 0:LANES] = q_ref[0, r0:r0 + tq, p * LANES:(p + 1) * LANES]
        qaug[p * tq:(p + 1) * tq, LANES:2 * LANES] = feats[p].astype(qaug.dtype)


def _tile_pairs(x):
    return jnp.concatenate([x] * PAIRS, axis=1)


def _finish(o_ref, r0, tq, acct_e, acct_o, gate_ref, gate_col, normalized=False):
    lane = lax.broadcasted_iota(jnp.int32, (tq, LANES), 1)
    if gate_ref is not None:
        gsig = jax.nn.sigmoid(gate_ref[0, r0:r0 + tq, :])
    for p in range(PAIRS):
        cols = slice(p * tq, (p + 1) * tq)
        top = jnp.concatenate([acct_e[0:HEAD_DIM, cols], acct_o[0:HEAD_DIM, cols]], axis=0)
        if not normalized:
            den = jnp.concatenate(
                [jnp.broadcast_to(acct_e[HEAD_DIM:HEAD_DIM + 1, cols], (HEAD_DIM, tq)),
                 jnp.broadcast_to(acct_o[HEAD_DIM:HEAD_DIM + 1, cols], (HEAD_DIM, tq))], axis=0)
            top = top / den
        out = top.T
        if gate_ref is not None:
            ce = 3 * (2 * p) + gate_col
            co = 3 * (2 * p + 1) + gate_col
            out = out * jnp.where(lane < HEAD_DIM, gsig[:, ce:ce + 1], gsig[:, co:co + 1])
        o_ref[0, r0:r0 + tq, p * LANES:(p + 1) * LANES] = out.astype(o_ref.dtype)


def _banded_kernel(*refs, n_chunks, window, has_sink, gate_col):
    refs = list(refs)
    if has_sink:
        sink_ref, slope_ref = refs[0], refs[1]
        refs = refs[2:]
    q_ref, k_ref, v_ref, qf_ref, kf_ref = refs[:5]
    refs = refs[5:]
    gate_ref = None
    if gate_col is not None:
        gate_ref = refs[0]
        refs = refs[1:]
    o_ref, kaug, vt, qaug = refs
    g = pl.program_id(1)
    tq = Q_SUB
    rows = PAIRS * tq
    ch = BAND_CHUNK
    n_pad = n_chunks - 1

    @pl.when(pl.program_id(2) == 0)
    def _():
        _build_keys(g, lambda s0: k_ref[0, pl.ds(s0, ch), :].astype(F32),
                    lambda s0: v_ref[0, pl.ds(s0, ch), :].astype(F32),
                    k_ref.shape[1] // ch, kf_ref, kaug, vt, n_pad, ch)

    krow = lax.broadcasted_iota(jnp.int32, (ch, tq), 0)
    qcol = lax.broadcasted_iota(jnp.int32, (ch, tq), 1)
    vrow = lax.broadcasted_iota(jnp.int32, (V_ROWS, rows), 0)
    for sub in range(q_ref.shape[1] // tq):
        i = pl.program_id(2) * (q_ref.shape[1] // tq) + sub
        _fill_queries(qaug.at[sub], q_ref, sub * tq, tq,
                      [jnp.broadcast_to(qf_ref[0, p:p + 1, :], (tq, LANES)) for p in range(PAIRS)])
        kc = kaug[pl.ds(i, n_chunks)].reshape(n_chunks * 2 * ch, 2 * LANES)
        st = _dot_nt(kc, qaug[sub])

        t = i * tq + qcol
        ev, od = [], []
        for u in range(n_chunks):
            spos = (i - n_pad + u) * ch + krow
            dist = t - spos
            vis = jnp.where(dist >= 0, jnp.where(dist < window, jnp.where(spos >= 0, 1.0, 0.0), 0.0), 0.0)
            bias = _tile_pairs(jnp.where(vis > 0.5, 0.0, NEG))
            ev.append(st[(2 * u) * ch:(2 * u + 1) * ch] + bias)
            od.append(st[(2 * u + 1) * ch:(2 * u + 2) * ch] + bias)

        tpos = (i * tq + lax.broadcasted_iota(jnp.int32, (1, tq), 1)).astype(F32)
        vmat = jnp.concatenate([vt[i + u] for u in range(n_chunks)], axis=1)
        accs = []
        for h, slabs in enumerate((ev, od)):
            m = slabs[0].max(axis=0, keepdims=True)
            for x in slabs[1:]:
                m = jnp.maximum(m, x.max(axis=0, keepdims=True))
            if has_sink:
                sk = jnp.concatenate(
                    [sink_ref[g * HEADS_PER_GROUP + 2 * p + h]
                     + slope_ref[g * HEADS_PER_GROUP + 2 * p + h] * tpos for p in range(PAIRS)], axis=1)
                m = jnp.maximum(m, sk)
            pmat = jnp.concatenate([jnp.exp(x - m).astype(vt.dtype) for x in slabs], axis=0)
            acc = _dot(vmat, pmat)
            if has_sink:
                acc = acc + jnp.where(vrow >= HEAD_DIM, jnp.exp(sk - m), 0.0)
            accs.append(acc)
        _finish(o_ref, sub * tq, tq, accs[0], accs[1], gate_ref, gate_col)


def _banded_attention(q, kv, k_col, v_col, qfeat, kfeat, window, sinks=None, slopes=None,
                      gates=None, gate_col=None):
    b, s, _ = q.shape
    tq = Q_BLOCK
    n_sub = tq // Q_SUB
    n_chunks = window // BAND_CHUNK + 1
    n_tot = kfeat.shape[0]
    has_sink = sinks is not None
    kern = functools.partial(_banded_kernel, n_chunks=n_chunks, window=window,
                             has_sink=has_sink, gate_col=gate_col)
    in_specs, args = [], []
    if has_sink:
        in_specs += [pl.BlockSpec(memory_space=pltpu.SMEM)] * 2
        args += [sinks, slopes]
    in_specs += [pl.BlockSpec((1, tq, PAIRS * LANES), lambda bi, g, i: (bi, i, g)),
                 pl.BlockSpec((1, s, LANES), lambda bi, g, i: (bi, 0, k_col)),
                 pl.BlockSpec((1, s, LANES), lambda bi, g, i: (bi, 0, v_col)),
                 pl.BlockSpec((1, PAIRS, LANES), lambda bi, g, i: (g, 0, 0)),
                 _const_spec(kfeat.shape)]
    args += [q, kv, kv, qfeat, kfeat]
    if gate_col is not None:
        in_specs.append(pl.BlockSpec((1, tq, LANES), lambda bi, g, i: (bi, i, g)))
        args.append(gates)
    return pl.pallas_call(
        kern,
        grid=(b, N_KV_GROUPS, s // tq),
        in_specs=in_specs,
        out_specs=pl.BlockSpec((1, tq, PAIRS * LANES), lambda bi, g, i: (bi, i, g)),
        out_shape=jax.ShapeDtypeStruct(q.shape, MXU_DT),
        scratch_shapes=[pltpu.VMEM((n_tot, 2, BAND_CHUNK, 2 * LANES), MXU_DT),
                        pltpu.VMEM((n_tot, V_ROWS, BAND_CHUNK), MXU_DT),
                        pltpu.VMEM((n_sub, PAIRS * Q_SUB, 2 * LANES), MXU_DT)],
        compiler_params=_params(("arbitrary", "arbitrary", "arbitrary")),
        name="banded_attention_w%d" % window,
    )(*args)


def _cmpsel_kernel(q_ref, k_ref, v_ref, qf_ref, kf_ref, ovt_ref, gate_ref, o_ref, mf_ref,
                   kaug, vt, qaug):
    g = pl.program_id(1)
    ncp = k_ref.shape[2]

    @pl.when(pl.program_id(2) == 0)
    def _():
        _build_keys(g, lambda s0: k_ref[0, 0], lambda s0: v_ref[0, 0], 1, kf_ref, kaug, vt, 0, ncp)

    n_sub = q_ref.shape[1] // Q_SUB
    for sub in range(n_sub):
        _cmpsel_block(pl.program_id(2) * n_sub + sub, sub, ncp, q_ref, qf_ref, ovt_ref, gate_ref,
                      o_ref, mf_ref, kaug, vt, qaug)


def _cmpsel_block(i, sub, ncp, q_ref, qf_ref, ovt_ref, gate_ref, o_ref, mf_ref, kaug, vt, qaug):
    tq = Q_SUB
    n_slc = NSA_SLC_LEN
    _fill_queries(qaug.at[sub], q_ref, sub * tq, tq,
                  [jnp.broadcast_to(qf_ref[0, p:p + 1, :], (tq, LANES)) for p in range(PAIRS)])
    st = _dot_nt(kaug[0].reshape(2 * ncp, 2 * LANES), qaug[sub])

    krow = lax.broadcasted_iota(jnp.int32, (ncp, tq), 0)
    qcol = lax.broadcasted_iota(jnp.int32, (ncp, tq), 1)
    t = i * tq + qcol
    cmp_end = krow * NSA_CMP_STRIDE + (NSA_CMP_LEN - 1)
    vis = jnp.where(t >= cmp_end, jnp.where(krow < ncp - 1, 1.0, 0.0), 0.0)
    bias = _tile_pairs(jnp.where(vis > 0.5, 0.0, NEG))

    pn = []
    for h in range(2):
        x = st[h * ncp:(h + 1) * ncp] + bias
        m = x.max(axis=0, keepdims=True)
        m = jnp.where(m > 0.5 * NEG, m, 0.0)
        e = jnp.exp(x - m)
        d = e.sum(axis=0, keepdims=True)
        pn.append(e * (1.0 / jnp.where(d > 0.0, d, 1.0)))
    _finish(o_ref, sub * tq, tq, _dot(vt[0], pn[0].astype(vt.dtype)), _dot(vt[0], pn[1].astype(vt.dtype)),
            gate_ref, 0, normalized=True)

    ps = pn[0] + pn[1]
    psum = ps[:, 0:tq]
    for p in range(1, PAIRS):
        psum = psum + ps[:, p * tq:(p + 1) * tq]
    p_hi = psum.astype(MXU_DT)
    p_lo = (psum - p_hi.astype(F32)).astype(MXU_DT)
    imp = (_dot(ovt_ref[...], p_hi) + _dot(ovt_ref[...], p_lo))[0:n_slc]
    blk = lax.broadcasted_iota(jnp.int32, (n_slc, tq), 0)
    tq_pos = i * tq + lax.broadcasted_iota(jnp.int32, (n_slc, tq), 1)
    cur = lax.shift_right_logical(tq_pos, int(math.log2(NSA_SLC_LEN)))
    forced = jnp.where(blk == 0, 1.0, jnp.where(blk == cur, 1.0, jnp.where(blk == cur - 1, 1.0, 0.0)))
    valid = blk <= cur
    score = jnp.where(valid, jnp.where(forced > 0.5, jnp.inf, imp), -jnp.inf)
    groups = [score[8 * v:8 * v + 8] for v in range(n_slc // 8)]
    ranks = [jnp.zeros((8, tq), F32) for _ in groups]
    sublane = lax.broadcasted_iota(jnp.int32, (8, tq), 0)
    for j in range(n_slc):
        rj = score[j:j + 1, :]
        for v in range(n_slc // 8):
            if 8 * v > j:
                beats = jnp.where(rj >= groups[v], 1.0, 0.0)
            elif 8 * v + 7 < j:
                beats = jnp.where(rj > groups[v], 1.0, 0.0)
            else:
                beats = jnp.where(sublane > (j - 8 * v), jnp.where(rj >= groups[v], 1.0, 0.0),
                                  jnp.where(rj > groups[v], 1.0, 0.0))
            ranks[v] = ranks[v] + beats
    rank = jnp.concatenate(ranks, axis=0)
    keep = jnp.where(valid, jnp.where(rank < float(NSA_TOPK), 1.0, 0.0), 0.0)
    feat = jnp.where(keep > 0.5, 0.0, NEG)
    feat = jnp.concatenate([feat, jnp.zeros((LANES - n_slc, tq), F32)], axis=0)
    mf_ref[0, 0, sub * tq:(sub + 1) * tq, :] = feat.T.astype(mf_ref.dtype)


def _compressed_and_select(q, kvcmp, qfeat, kfeat, ovt, gates):
    b, s, _ = q.shape
    tq = Q_BLOCK
    ncp = kvcmp.shape[2]
    return pl.pallas_call(
        _cmpsel_kernel,
        grid=(b, N_KV_GROUPS, s // tq),
        in_specs=[pl.BlockSpec((1, tq, PAIRS * LANES), lambda bi, g, i: (bi, i, g)),
                  pl.BlockSpec((1, 1, ncp, LANES), lambda bi, g, i: (bi, 0, 0, 0)),
                  pl.BlockSpec((1, 1, ncp, LANES), lambda bi, g, i: (bi, 1, 0, 0)),
                  pl.BlockSpec((1, PAIRS, LANES), lambda bi, g, i: (g, 0, 0)),
                  _const_spec(kfeat.shape), _const_spec(ovt.shape),
                  pl.BlockSpec((1, tq, LANES), lambda bi, g, i: (bi, i, g))],
        out_specs=[pl.BlockSpec((1, tq, PAIRS * LANES), lambda bi, g, i: (bi, i, g)),
                   pl.BlockSpec((1, 1, tq, LANES), lambda bi, g, i: (bi, g, i, 0))],
        out_shape=[jax.ShapeDtypeStruct(q.shape, MXU_DT),
                   jax.ShapeDtypeStruct((b, N_KV_GROUPS, s, LANES), MXU_DT)],
        scratch_shapes=[pltpu.VMEM((1, 2, ncp, 2 * LANES), MXU_DT),
                        pltpu.VMEM((1, V_ROWS, ncp), MXU_DT),
                        pltpu.VMEM((tq // Q_SUB, PAIRS * Q_SUB, 2 * LANES), MXU_DT)],
        compiler_params=_params(("arbitrary", "arbitrary", "arbitrary")),
        name="nsa_compressed_select",
    )(q, kvcmp, kvcmp, qfeat, kfeat, ovt, gates)


def _selected_kernel(q_ref, k_ref, v_ref, qf_ref, mf_ref, kf_ref, gate_ref, o_ref,
                     kaug, vt, qaug, acc_e, acc_o, st_a, st_b):
    g = pl.program_id(1)
    i = pl.program_id(2)
    tq = q_ref.shape[1]
    ch = SLC_CHUNK
    rows = PAIRS * tq

    @pl.when(i == 0)
    def _():
        _build_keys(g, lambda s0: k_ref[0, pl.ds(s0, ch), :].astype(F32),
                    lambda s0: v_ref[0, pl.ds(s0, ch), :].astype(F32),
                    k_ref.shape[1] // ch, kf_ref, kaug, vt, 0, ch)

    mask_feat = mf_ref[0, 0].astype(F32)
    _fill_queries(qaug, q_ref, 0, tq, [mask_feat + qf_ref[0, p:p + 1, :] for p in range(PAIRS)])
    acc_e[...] = jnp.zeros((V_ROWS, rows), F32)
    acc_o[...] = jnp.zeros((V_ROWS, rows), F32)

    def update(x, m_old, acc_ref, vt_c):
        m_new = jnp.maximum(m_old, x.max(axis=0, keepdims=True))
        alpha = jnp.exp(m_old - m_new)
        p = jnp.exp(x - m_new).astype(vt_c.dtype)
        acc_ref[...] = alpha * acc_ref[...] + _dot(vt_c, p)
        return m_new

    def scores_into(buf, c):
        buf[...] = _dot_nt(kaug[c].reshape(2 * ch, 2 * LANES), qaug[...])

    def consume(buf, c, ms, causal):
        xe, xo = buf[0:ch, :], buf[ch:2 * ch, :]
        if causal:
            krow = lax.broadcasted_iota(jnp.int32, (ch, tq), 0)
            qcol = lax.broadcasted_iota(jnp.int32, (ch, tq), 1)
            bias = _tile_pairs(jnp.where(c * ch + krow <= i * tq + qcol, 0.0, NEG))
            xe, xo = xe + bias, xo + bias
        return update(xe, ms[0], acc_e, vt[c]), update(xo, ms[1], acc_o, vt[c])

    def pair(k, ms):
        c = 2 * k
        scores_into(st_b, c + 1)
        ms = consume(st_a, c, ms, False)
        scores_into(st_a, c + 2)
        return consume(st_b, c + 1, ms, False)

    assert tq == 2 * ch
    m0 = jnp.full((1, rows), M_INIT, F32)
    scores_into(st_a, 0)
    ms = lax.fori_loop(0, i, pair, (m0, m0))
    scores_into(st_b, 2 * i + 1)
    consume(st_b, 2 * i + 1, consume(st_a, 2 * i, ms, True), True)
    _finish(o_ref, 0, tq, acc_e[...], acc_o[...], gate_ref, 1)


def _selected_attention(q, kv, k_col, v_col, qfeat, maskfeat, kfeat, gates):
    b, s, _ = q.shape
    tq = Q_BLOCK
    n_tot = kfeat.shape[0]
    return pl.pallas_call(
        _selected_kernel,
        grid=(b, N_KV_GROUPS, s // tq),
        in_specs=[pl.BlockSpec((1, tq, PAIRS * LANES), lambda bi, g, i: (bi, i, g)),
                  pl.BlockSpec((1, s, LANES), lambda bi, g, i: (bi, 0, k_col)),
                  pl.BlockSpec((1, s, LANES), lambda bi, g, i: (bi, 0, v_col)),
                  pl.BlockSpec((1, PAIRS, LANES), lambda bi, g, i: (g, 0, 0)),
                  pl.BlockSpec((1, 1, tq, LANES), lambda bi, g, i: (bi, g, i, 0)),
                  _const_spec(kfeat.shape),
                  pl.BlockSpec((1, tq, LANES), lambda bi, g, i: (bi, i, g))],
        out_specs=pl.BlockSpec((1, tq, PAIRS * LANES), lambda bi, g, i: (bi, i, g)),
        out_shape=jax.ShapeDtypeStruct(q.shape, MXU_DT),
        scratch_shapes=[pltpu.VMEM((n_tot, 2, SLC_CHUNK, 2 * LANES), MXU_DT),
                        pltpu.VMEM((n_tot, V_ROWS, SLC_CHUNK), MXU_DT),
                        pltpu.VMEM((PAIRS * tq, 2 * LANES), MXU_DT),
                        pltpu.VMEM((V_ROWS, PAIRS * tq), F32),
                        pltpu.VMEM((V_ROWS, PAIRS * tq), F32),
                        pltpu.VMEM((2 * SLC_CHUNK, PAIRS * tq), F32),
                        pltpu.VMEM((2 * SLC_CHUNK, PAIRS * tq), F32)],
        compiler_params=_params(("arbitrary", "arbitrary", "arbitrary")),
        name="nsa_selected",
    )(q, kv, kv, qfeat, maskfeat, kfeat, gates)


def _outproj_kernel(oa_ref, oc_ref, os_ref, ow_ref, x_ref, w_ref, b_ref, g_ref, beta_ref,
                    wr_ref, br_ref, x1_ref, route_ref, *, alpha):
    half = oa_ref.shape[1]
    on = (oc_ref[...].astype(F32) + os_ref[...].astype(F32) + ow_ref[...].astype(F32)).astype(MXU_DT)
    mix = _dot(oa_ref[...], w_ref[0:half]) + _dot(on, w_ref[half:]) + b_ref[...]
    x1 = _layer_norm(alpha * x_ref[...] + mix, g_ref[...], beta_ref[...])
    x1_ref[...] = x1

    x_hi = x1.astype(MXU_DT)
    x_lo = (x1 - x_hi.astype(F32)).astype(MXU_DT)
    logits = (_dot(x_hi, wr_ref[0]) + _dot(x_hi, wr_ref[1]) + _dot(x_lo, wr_ref[0])) + br_ref[...]
    lane = lax.broadcasted_iota(jnp.int32, logits.shape, 1)
    lanef = lane.astype(F32)
    big = float(4 * LANES)
    gl = jnp.where(lane < N_GROUPS, logits, -jnp.inf)
    gmax = gl.max(axis=1, keepdims=True)
    gsel = jnp.where(gl == gmax, lanef, big).min(axis=1, keepdims=True)
    g_w = 1.0 / jnp.exp(gl - gmax).sum(axis=1, keepdims=True)
    lo = N_GROUPS + gsel * EXPERTS_PER_GROUP
    el = jnp.where(lanef >= lo, jnp.where(lanef < lo + EXPERTS_PER_GROUP, logits, -jnp.inf), -jnp.inf)
    v1 = el.max(axis=1, keepdims=True)
    i1 = jnp.where(el == v1, lanef, big).min(axis=1, keepdims=True)
    el2 = jnp.where(lanef == i1, -jnp.inf, el)
    v2 = el2.max(axis=1, keepdims=True)
    i2 = jnp.where(el2 == v2, lanef, big).min(axis=1, keepdims=True)
    e2 = jnp.exp(v2 - v1)
    w1 = g_w / (1.0 + e2)
    w2 = g_w * e2 / (1.0 + e2)
    route_ref[...] = jnp.where(lane == 0, i1 - N_GROUPS,
                               jnp.where(lane == 1, i2 - N_GROUPS,
                                         jnp.where(lane == 2, w1, jnp.where(lane == 3, w2, 0.0))))


def _output_projection(oa, oc, os_, ow, x2d, w, b, ln_g, ln_b, wr, br, alpha):
    n, d = x2d.shape
    half = oa.shape[1]
    tm = min(PROJ_TILE, n)
    row = lambda i: (i, 0)
    return pl.pallas_call(
        functools.partial(_outproj_kernel, alpha=alpha),
        grid=(n // tm,),
        in_specs=[pl.BlockSpec((tm, half), row)] * 4 + [pl.BlockSpec((tm, d), row)]
        + [_const_spec(a.shape) for a in (w, b, ln_g, ln_b, wr, br)],
        out_specs=[pl.BlockSpec((tm, d), row), pl.BlockSpec((tm, LANES), row)],
        out_shape=[jax.ShapeDtypeStruct((n, d), F32), jax.ShapeDtypeStruct((n, LANES), F32)],
        compiler_params=_params(("arbitrary",)),
        name="output_projection_ln_route",
    )(oa, oc, os_, ow, x2d, w, b, ln_g, ln_b, wr, br)


def _row_copy(src_hbm, src_row, dst, j, u, sem):
    return pltpu.make_async_copy(src_hbm.at[pl.ds(src_row, 1)], dst.at[j, pl.ds(u, 1)], sem)


def _rows_start(src_hbm, idx_ref, base, dst, sem, n_groups):
    def body(j, carry):
        for u in range(SUBLANES):
            _row_copy(src_hbm, idx_ref[base + j * SUBLANES + u], dst, j, u, sem).start(priority=u % 2)
        return carry
    lax.fori_loop(0, n_groups, body, 0)


def _rows_wait(src_hbm, dst, sem, n_groups):
    def body(j, carry):
        for u in range(SUBLANES):
            _row_copy(src_hbm, 0, dst, j, u, sem).wait()
        return carry
    lax.fori_loop(0, n_groups, body, 0)


def _expert_kernel(blk_e_ref, n_used_ref, n_grp_ref, row_tok_ref, x_hbm, wg_ref, wu_ref, wd_ref, o_ref,
                   xbuf, sem, wg_b, wu_b, wd_b):
    i = pl.program_id(0)
    n_used = n_used_ref[0]
    groups, d = xbuf.shape[1], xbuf.shape[3]
    t = groups * SUBLANES
    used = i < n_used
    slot = lax.rem(i, 2)
    new_expert = jnp.logical_or(i == 0, blk_e_ref[i] != blk_e_ref[jnp.maximum(i - 1, 0)])

    @pl.when(i == 0)
    def _():
        xbuf[...] = jnp.zeros(xbuf.shape, xbuf.dtype)
        _rows_start(x_hbm, row_tok_ref, 0, xbuf.at[0], sem.at[0], n_grp_ref[0])

    @pl.when(i + 1 < n_used)
    def _():
        _rows_start(x_hbm, row_tok_ref, (i + 1) * t, xbuf.at[1 - slot], sem.at[1 - slot], n_grp_ref[i + 1])

    @pl.when(jnp.logical_and(used, new_expert))
    def _():
        wg_b[...] = wg_ref[0, 0].astype(wg_b.dtype)
        wu_b[...] = wu_ref[0, 0].astype(wu_b.dtype)
        wd_b[...] = wd_ref[0, 0].astype(wd_b.dtype)

    @pl.when(used)
    def _():
        _rows_wait(x_hbm, xbuf.at[slot], sem.at[slot], n_grp_ref[i])
        xb = xbuf[slot].reshape(t, d).astype(MXU_DT)
        gate = _dot(xb, wg_b[...])
        up = _dot(xb, wu_b[...])
        hidden = (gate * jax.nn.sigmoid(gate) * up).astype(MXU_DT)
        o_ref[...] = _dot(hidden, wd_b[...])

    @pl.when(jnp.logical_not(used))
    def _():
        o_ref[...] = jnp.zeros(o_ref.shape, o_ref.dtype)


def _expert_ffn(x1, row_tok, blk_e, n_used, n_grp, wg, wu, wd, layer):
    d = x1.shape[1]
    p = row_tok.shape[0]
    t = EXPERT_ROWS
    hid = wg.shape[3]
    wmap = lambda i, be, nu, ng, rt: (layer, be[i], 0, 0)
    return pl.pallas_call(
        _expert_kernel,
        grid_spec=pltpu.PrefetchScalarGridSpec(
            num_scalar_prefetch=4,
            grid=(p // t,),
            in_specs=[pl.BlockSpec(memory_space=pl.ANY),
                      pl.BlockSpec((1, 1, d, hid), wmap),
                      pl.BlockSpec((1, 1, d, hid), wmap),
                      pl.BlockSpec((1, 1, hid, d), wmap)],
            out_specs=pl.BlockSpec((t, d), lambda i, be, nu, ng, rt: (i, 0)),
            scratch_shapes=[pltpu.VMEM((2, t // SUBLANES, SUBLANES, d), F32), pltpu.SemaphoreType.DMA((2,)),
                            pltpu.VMEM((d, hid), MXU_DT), pltpu.VMEM((d, hid), MXU_DT),
                            pltpu.VMEM((hid, d), MXU_DT)]),
        out_shape=jax.ShapeDtypeStruct((p, d), F32),
        compiler_params=_params(("arbitrary",)),
        name="expert_ffn",
    )(blk_e, n_used, n_grp, row_tok, x1, wg, wu, wd)


def _combine_kernel(dest_ref, ys_hbm, route_ref, x_ref, g_ref, b_ref, o_ref, ybuf, sem, *, alpha, n_tok):
    i = pl.program_id(0)
    tm, d = x_ref.shape
    groups = tm // SUBLANES
    slot = lax.rem(i, 2)

    def start(blk, s):
        for k in range(2):
            _rows_start(ys_hbm, dest_ref, k * n_tok + blk * tm, ybuf.at[s, k], sem.at[s], groups)

    @pl.when(i == 0)
    def _():
        start(0, 0)

    @pl.when(i + 1 < pl.num_programs(0))
    def _():
        start(i + 1, 1 - slot)

    for k in range(2):
        _rows_wait(ys_hbm, ybuf.at[slot, k], sem.at[slot], groups)
    r = route_ref[...]
    y = r[:, 2:3] * ybuf[slot, 0].reshape(tm, d) + r[:, 3:4] * ybuf[slot, 1].reshape(tm, d)
    o_ref[...] = _layer_norm(alpha * x_ref[...] + y, g_ref[...], b_ref[...])


def _combine(ys, dest_kmajor, route, x1, ln_g, ln_b, alpha):
    n, d = x1.shape
    tm = min(ROW_TILE, n)
    row = lambda i, dst: (i, 0)
    zero = lambda i, dst: (0, 0)
    return pl.pallas_call(
        functools.partial(_combine_kernel, alpha=alpha, n_tok=n),
        grid_spec=pltpu.PrefetchScalarGridSpec(
            num_scalar_prefetch=1,
            grid=(n // tm,),
            in_specs=[pl.BlockSpec(memory_space=pl.ANY),
                      pl.BlockSpec((tm, LANES), row), pl.BlockSpec((tm, d), row),
                      pl.BlockSpec(ln_g.shape, zero), pl.BlockSpec(ln_b.shape, zero)],
            out_specs=pl.BlockSpec((tm, d), row),
            scratch_shapes=[pltpu.VMEM((2, 2, tm // SUBLANES, SUBLANES, d), F32),
                            pltpu.SemaphoreType.DMA((2,))]),
        out_shape=jax.ShapeDtypeStruct((n, d), F32),
        compiler_params=_params(("arbitrary",)),
        name="combine_ln",
    )(dest_kmajor, ys, route, x1, ln_g, ln_b)


def _dispatch_plan(route, n_tokens):
    t = EXPERT_ROWS
    a = 2 * n_tokens
    eid = route[:, 0:2].astype(jnp.int32).reshape(a)
    onehot = (eid[:, None] == jnp.arange(N_EXPERTS, dtype=jnp.int32)[None, :]).astype(jnp.int32)
    counts = onehot.sum(axis=0)
    before = jnp.cumsum(onehot, axis=0) - onehot
    rank = jnp.take_along_axis(before, eid[:, None], axis=1)[:, 0]
    padded = (counts + t - 1) // t * t
    pend = jnp.cumsum(padded)
    pstart = pend - padded
    dest = (pstart[eid] + rank).astype(jnp.int32)
    p_rows = a + N_EXPERTS * t
    filler = jnp.arange(p_rows, dtype=jnp.int32) % n_tokens
    row_tok = filler.at[dest].set(jnp.arange(a, dtype=jnp.int32) // 2)
    blk_start = jnp.arange(p_rows // t, dtype=jnp.int32) * t
    blk_e = jnp.minimum((pend[None, :] <= blk_start[:, None]).astype(jnp.int32).sum(axis=1), N_EXPERTS - 1)
    n_used = (pend[-1] // t).astype(jnp.int32).reshape(1)
    valid = jnp.clip((pstart + counts)[blk_e] - blk_start, 0, t)
    n_grp = ((valid + SUBLANES - 1) // SUBLANES).astype(jnp.int32)
    dest_kmajor = dest.reshape(n_tokens, 2).T.reshape(a)
    return dest_kmajor, row_tok, blk_e.astype(jnp.int32), n_used, n_grp


def kernel(x, w_in, b_in, swa_sinks, cmp_pe_k, cmp_w1_k, cmp_w2_k, cmp_pe_v, cmp_w1_v, cmp_w2_v,
           w_out, b_out, ln1_g, ln1_b, w_group, b_group, w_expert, b_expert, we_gate, we_up,
           we_down, ln2_g, ln2_b):
    bsz, seq, d_model = x.shape
    depth = w_in.shape[0]
    n_tok = bsz * seq
    n_heads = d_model // HEAD_DIM
    q_w = n_heads // 2 * HEAD_DIM
    kv_w = N_KV_GROUPS * HEAD_DIM
    alpha = (2.0 * depth) ** 0.25
    scale = HEAD_DIM ** -0.5
    assert q_w == N_KV_GROUPS * PAIRS * LANES and seq % SLC_CHUNK == 0

    slopes_a, slopes_n = _alibi_slopes(n_heads)
    qfeat_a = jnp.asarray(_query_feats(slopes_a), F32)
    qfeat_n = jnp.asarray(_query_feats(slopes_n), F32)
    pos = np.arange(seq)
    kfeat_swa = jnp.asarray(_key_feats(pos, SWA_WINDOW // BAND_CHUNK, BAND_CHUNK, False), MXU_DT)
    kfeat_win = jnp.asarray(_key_feats(pos, NSA_WINDOW // BAND_CHUNK, BAND_CHUNK, False), MXU_DT)
    kfeat_slc = jnp.asarray(_key_feats(pos, 0, SLC_CHUNK, True), MXU_DT)
    n_cmp = (seq - NSA_CMP_LEN) // NSA_CMP_STRIDE + 1
    n_cmp_pad = seq // NSA_CMP_STRIDE
    cmp_end = np.arange(n_cmp_pad) * NSA_CMP_STRIDE + NSA_CMP_LEN - 1
    kfeat_cmp = jnp.asarray(_key_feats(cmp_end, 0, n_cmp_pad, False), MXU_DT)
    n_slc = seq // NSA_SLC_LEN
    cs = np.arange(n_cmp_pad)[None, :] * NSA_CMP_STRIDE
    ss = np.arange(LANES)[:, None] * NSA_SLC_LEN
    ovt = ((cs < ss + NSA_SLC_LEN) & (cs + NSA_CMP_LEN - 1 >= ss)
           & (np.arange(n_cmp_pad)[None, :] < n_cmp) & (np.arange(LANES)[:, None] < n_slc))
    ovt = jnp.asarray(ovt.astype(np.float32), MXU_DT)
    slopes_a_s = jnp.asarray(slopes_a, F32)

    o_qa, o_ka = 0, q_w
    o_qn = q_w + 2 * kv_w
    o_kvn = o_qn + q_w
    o_gn = o_kvn + 6 * kv_w
    gate_w = 3 * HEADS_PER_GROUP

    h = x.reshape(n_tok, d_model)
    for l in range(depth):
        wl, bl = w_in[l], b_in[l]

        def seg(a, lo, hi):
            return a[..., lo:hi]

        def gate_cols(a):
            pad = [(0, 0)] * (a.ndim - 1) + [(0, LANES - gate_w)]
            return jnp.concatenate([jnp.pad(seg(a, o_gn + g * gate_w, o_gn + (g + 1) * gate_w), pad)
                                    for g in range(N_KV_GROUPS)], axis=-1)

        def regroup(a):
            return jnp.concatenate([seg(a, o_qa, o_qa + q_w) * scale, seg(a, o_qn, o_qn + q_w) * scale,
                                    seg(a, o_ka, o_ka + 2 * kv_w), seg(a, o_kvn, o_kvn + 6 * kv_w),
                                    gate_cols(a)], axis=-1)

        w_r = regroup(wl).astype(MXU_DT)
        b_r = regroup(bl)[None, :]
        qa, qn, kv, gates = _input_projection(h, w_r, b_r)
        qa = qa.reshape(bsz, seq, q_w)
        qn = qn.reshape(bsz, seq, q_w)
        kv = kv.reshape(bsz, seq, 8 * kv_w)
        gates = gates.reshape(bsz, seq, 2 * LANES)

        o_a = _banded_attention(qa, kv, 0, 1, qfeat_a, kfeat_swa, SWA_WINDOW,
                                sinks=swa_sinks[l], slopes=slopes_a_s)

        kvc = kv[:, :, 2 * kv_w:4 * kv_w].reshape(bsz, n_cmp_pad, NSA_CMP_STRIDE, 2, N_KV_GROUPS, HEAD_DIM)
        kvc = kvc.transpose(0, 3, 4, 1, 2, 5).reshape(bsz, 2, N_KV_GROUPS, n_cmp_pad, NSA_CMP_STRIDE * HEAD_DIM)
        pe = jnp.stack([cmp_pe_k[l], cmp_pe_v[l]]).reshape(2, 1, NSA_CMP_LEN * HEAD_DIM)
        pe = jnp.broadcast_to(pe, (2, 8, NSA_CMP_LEN * HEAD_DIM))
        w1 = jnp.stack([cmp_w1_k[l], cmp_w1_v[l]]).astype(MXU_DT)
        w2 = jnp.stack([cmp_w2_k[l], cmp_w2_v[l]]).astype(MXU_DT)
        kvcmp = _compress(kvc, pe, w1, w2)

        o_cmp, maskfeat = _compressed_and_select(qn, kvcmp, qfeat_n, kfeat_cmp, ovt, gates)
        o_slc = _selected_attention(qn, kv, 4, 5, qfeat_n, maskfeat, kfeat_slc, gates)
        o_win = _banded_attention(qn, kv, 6, 7, qfeat_n, kfeat_win, NSA_WINDOW,
                                  gates=gates, gate_col=2)

        wr = jnp.concatenate([w_group[l], w_expert[l],
                              jnp.zeros((d_model, LANES - N_GROUPS - N_EXPERTS), F32)], axis=1)
        br = jnp.concatenate([b_group[l], b_expert[l],
                              jnp.zeros((LANES - N_GROUPS - N_EXPERTS,), F32)])[None, :]
        wr_hi = wr.astype(MXU_DT)
        wr = jnp.stack([wr_hi, (wr - wr_hi.astype(F32)).astype(MXU_DT)])
        x1, route = _output_projection(
            o_a.reshape(n_tok, q_w), o_cmp.reshape(n_tok, q_w), o_slc.reshape(n_tok, q_w),
            o_win.reshape(n_tok, q_w), h, w_out[l].astype(MXU_DT), b_out[l][None, :],
            ln1_g[l][None, :], ln1_b[l][None, :], wr, br, alpha)

        dest_kmajor, row_tok, blk_e, n_used, n_grp = _dispatch_plan(route, n_tok)
        ys = _expert_ffn(x1, row_tok, blk_e, n_used, n_grp, we_gate, we_up, we_down, l)
        h = _combine(ys, dest_kmajor, route, x1, ln2_g[l][None, :], ln2_b[l][None, :], alpha)
    return h.reshape(bsz, seq, d_model)
```

```python
import functools
import math

import numpy as np
import jax
import jax.numpy as jnp
from jax import lax
from jax.experimental import pallas as pl
from jax.experimental.pallas import tpu as pltpu

F32 = jnp.float32
MXU_DT = jnp.bfloat16

HEAD_DIM = 64
HEADS_PER_GROUP = 8
PAIRS = HEADS_PER_GROUP // 2
N_KV_GROUPS = 2
SWA_WINDOW = 128
NSA_WINDOW = 512
NSA_CMP_LEN = 32
NSA_CMP_STRIDE = 16
NSA_CMP_HIDDEN = 256
NSA_SLC_LEN = 64
NSA_TOPK = 16
N_GROUPS = 8
EXPERTS_PER_GROUP = 8
N_EXPERTS = N_GROUPS * EXPERTS_PER_GROUP
EXPERT_HIDDEN = 512
LN_EPS = 1e-5

LANES = 128
SUBLANES = 8
V_ROWS = 80
Q_BLOCK = 512
Q_SUB = 128
BAND_CHUNK = 128
SLC_CHUNK = 256
EXPERT_ROWS = 256
ROW_TILE = 256
PROJ_TILE = 512
NEG = -1e30
M_INIT = -5e29
VMEM_LIMIT = 56 * 1024 * 1024

FEAT_EVEN = 64
FEAT_ODD = 72


def _alibi_slopes(n_heads_total):
    n = n_heads_total
    s = np.exp2(-8.0 * np.arange(1, n + 1, dtype=np.float32) / np.float32(n)).astype(np.float32)
    return s[0::2], s[1::2]


def _bf16_parts(v):
    v = np.asarray(v, np.float32)
    a = v.astype(jnp.bfloat16).astype(np.float32)
    r = (v - a).astype(np.float32)
    b = r.astype(jnp.bfloat16).astype(np.float32)
    c = (r - b).astype(np.float32)
    return a, b, c


def _query_feats(slopes):
    out = np.zeros((N_KV_GROUPS, PAIRS, LANES), np.float32)
    a, b, c = _bf16_parts(slopes)
    for g in range(N_KV_GROUPS):
        for p in range(PAIRS):
            for h, base in ((0, FEAT_EVEN), (1, FEAT_ODD)):
                idx = g * HEADS_PER_GROUP + 2 * p + h
                out[g, p, base:base + 6] = [a[idx], b[idx], c[idx], a[idx], b[idx], c[idx]]
    return out


def _key_feats(positions, n_pad_chunks, chunk, with_blocks):
    pos = np.asarray(positions, np.int64)
    n = pos.shape[0]
    assert n % chunk == 0
    hi = (pos // 64 * 64).astype(np.float32)
    lo = (pos % 64).astype(np.float32)
    f = np.zeros((n, 2, LANES), np.float32)
    for v, base in ((0, FEAT_EVEN), (1, FEAT_ODD)):
        f[:, v, base:base + 3] = hi[:, None]
        f[:, v, base + 3:base + 6] = lo[:, None]
    if with_blocks:
        blk = pos // NSA_SLC_LEN
        for v in range(2):
            f[np.arange(n), v, blk] = 1.0
    f = f.reshape(n // chunk, chunk, 2, LANES).transpose(0, 2, 1, 3)
    if n_pad_chunks:
        f = np.concatenate([np.zeros((n_pad_chunks,) + f.shape[1:], np.float32), f], axis=0)
    return f


def _params(sem, vmem=VMEM_LIMIT):
    return pltpu.CompilerParams(dimension_semantics=sem, vmem_limit_bytes=vmem)


def _const_spec(shape):
    nd = len(shape)
    return pl.BlockSpec(shape, lambda *_: (0,) * nd, pipeline_mode=pl.Buffered(1))


def _dot(a, b):
    return jnp.dot(a, b, preferred_element_type=F32)


def _dot_nt(a, b):
    return lax.dot_general(a, b, (((1,), (1,)), ((), ())), preferred_element_type=F32)


def _layer_norm(h, g, b):
    mu = jnp.mean(h, axis=-1, keepdims=True)
    d = h - mu
    var = jnp.mean(d * d, axis=-1, keepdims=True)
    return d * lax.rsqrt(var + LN_EPS) * g + b


def _inproj_kernel(x_ref, w_ref, b_ref, qa_ref, qn_ref, kv_ref, gt_ref):
    xb = x_ref[...].astype(MXU_DT)
    width = qa_ref.shape[1]
    for j, o_ref in enumerate((qa_ref, qn_ref, kv_ref)):
        acc = _dot(xb, w_ref[:, j * width:(j + 1) * width]) + b_ref[:, j * width:(j + 1) * width]
        o_ref[...] = acc.astype(o_ref.dtype)
    gt_ref[...] = _dot(xb, w_ref[:, 3 * width:]) + b_ref[:, 3 * width:]


def _input_projection(x2d, w, b):
    n, d = x2d.shape
    width = (w.shape[1] - 2 * LANES) // 3
    tm = min(PROJ_TILE, n)
    row = lambda i: (i, 0)
    return pl.pallas_call(
        _inproj_kernel,
        grid=(n // tm,),
        in_specs=[pl.BlockSpec((tm, d), row), _const_spec(w.shape), _const_spec(b.shape)],
        out_specs=[pl.BlockSpec((tm, width), row)] * 3 + [pl.BlockSpec((tm, 2 * LANES), row)],
        out_shape=[jax.ShapeDtypeStruct((n, width), MXU_DT)] * 3
        + [jax.ShapeDtypeStruct((n, 2 * LANES), F32)],
        compiler_params=_params(("arbitrary",)),
        name="input_projection",
    )(x2d, w, b)


def _gelu_tanh(x):
    c = math.sqrt(2.0 / math.pi)
    return 0.5 * x * (1.0 + jnp.tanh(c * (x + 0.044715 * (x * x * x))))


def _compress_kernel(c_ref, pe_ref, w1_ref, w2_ref, o_ref):
    half = c_ref.shape[-1]
    w1 = w1_ref[0]
    pe = pe_ref[0]
    pe_hi = pe.astype(MXU_DT)
    pe_lo = (pe - pe_hi.astype(F32)).astype(MXU_DT)
    base = (_dot(pe_hi, w1) + _dot(pe_lo, w1))[0:1]
    for g in range(N_KV_GROUPS):
        c = c_ref[0, 0, g]
        top = _dot(c, w1[:half])
        bot = _dot(c, w1[half:])
        n_rows = bot.shape[0]
        hidden = top + pltpu.roll(bot, n_rows - 1, 0) + base
        act = _gelu_tanh(hidden)
        o_ref[0, 0, :, g * HEAD_DIM:(g + 1) * HEAD_DIM] = _dot(act.astype(MXU_DT), w2_ref[0])


def _compress(chunks, pe, w1, w2):
    b, two, g, nch, half = chunks.shape
    return pl.pallas_call(
        _compress_kernel,
        grid=(b, two),
        in_specs=[pl.BlockSpec((1, 1, g, nch, half), lambda i, j: (i, j, 0, 0, 0)),
                  pl.BlockSpec((1, 8, 2 * half), lambda i, j: (j, 0, 0)),
                  pl.BlockSpec((1, 2 * half, NSA_CMP_HIDDEN), lambda i, j: (j, 0, 0)),
                  pl.BlockSpec((1, NSA_CMP_HIDDEN, HEAD_DIM), lambda i, j: (j, 0, 0))],
        out_specs=pl.BlockSpec((1, 1, nch, LANES), lambda i, j: (i, j, 0, 0)),
        out_shape=jax.ShapeDtypeStruct((b, two, nch, LANES), F32),
        compiler_params=_params(("arbitrary", "arbitrary")),
        name="nsa_compress",
    )(chunks, pe, w1, w2)


def _place_group(x, g_is_zero):
    lane = lax.broadcasted_iota(jnp.int32, x.shape, 1)
    rolled = pltpu.roll(x, HEAD_DIM, 1)
    lo_src, hi_src = (x, rolled) if g_is_zero else (rolled, x)
    lo = jnp.where(lane < HEAD_DIM, lo_src, 0.0)
    hi = jnp.where(lane >= HEAD_DIM, hi_src, 0.0)
    return lo, hi


def _build_keys(g, load_k, load_v, n_chunks, kf_ref, kaug, vt, n_pad, chunk):
    ones = jnp.ones((V_ROWS - HEAD_DIM, chunk), F32)
    for gval in range(N_KV_GROUPS):
        @pl.when(g == gval)
        def _():
            def body(c, carry):
                start = pl.multiple_of(c * chunk, chunk)
                klo, khi = _place_group(load_k(start), gval == 0)
                vlo, _ = _place_group(load_v(start), gval == 0)
                kaug[n_pad + c, 0, :, 0:LANES] = klo.astype(kaug.dtype)
                kaug[n_pad + c, 1, :, 0:LANES] = khi.astype(kaug.dtype)
                vt[n_pad + c] = jnp.concatenate([vlo.T[0:HEAD_DIM], ones], axis=0).astype(vt.dtype)
                return carry
            lax.fori_loop(0, n_chunks, body, 0)
    kaug[:, :, :, LANES:2 * LANES] = kf_ref[...]
    if n_pad:
        kaug[0:n_pad, :, :, 0:LANES] = jnp.zeros((n_pad, 2, chunk, LANES), kaug.dtype)
        vt[0:n_pad] = jnp.zeros((n_pad, V_ROWS, chunk), vt.dtype)


def _fill_queries(qaug, q_ref, r0, tq, feats):
    for p in range(PAIRS):
        qaug[p * tq:(p + 1) * tq, 0:LANES] = q_ref[0, r0:r0 + tq, p * LANES:(p + 1) * LANES]
        qaug[p * tq:(p + 1) * tq, LANES:2 * LANES] = feats[p].astype(qaug.dtype)


def _tile_pairs(x):
    return jnp.concatenate([x] * PAIRS, axis=1)


def _finish(o_ref, r0, tq, acct_e, acct_o, gate_ref, gate_col, normalized=False):
    lane = lax.broadcasted_iota(jnp.int32, (tq, LANES), 1)
    if gate_ref is not None:
        gsig = jax.nn.sigmoid(gate_ref[0, r0:r0 + tq, :])
    for p in range(PAIRS):
        cols = slice(p * tq, (p + 1) * tq)
        top = jnp.concatenate([acct_e[0:HEAD_DIM, cols], acct_o[0:HEAD_DIM, cols]], axis=0)
        if not normalized:
            den = jnp.concatenate(
                [jnp.broadcast_to(acct_e[HEAD_DIM:HEAD_DIM + 1, cols], (HEAD_DIM, tq)),
                 jnp.broadcast_to(acct_o[HEAD_DIM:HEAD_DIM + 1, cols], (HEAD_DIM, tq))], axis=0)
            top = top / den
        out = top.T
        if gate_ref is not None:
            ce = 3 * (2 * p) + gate_col
            co = 3 * (2 * p + 1) + gate_col
            out = out * jnp.where(lane < HEAD_DIM, gsig[:, ce:ce + 1], gsig[:, co:co + 1])
        o_ref[0, r0:r0 + tq, p * LANES:(p + 1) * LANES] = out.astype(o_ref.dtype)


def _banded_kernel(*refs, n_chunks, window, has_sink, gate_col):
    refs = list(refs)
    if has_sink:
        sink_ref, slope_ref = refs[0], refs[1]
        refs = refs[2:]
    q_ref, k_ref, v_ref, qf_ref, kf_ref = refs[:5]
    refs = refs[5:]
    gate_ref = None
    if gate_col is not None:
        gate_ref = refs[0]
        refs = refs[1:]
    o_ref, kaug, vt, qaug = refs
    g = pl.program_id(1)
    tq = Q_SUB
    rows = PAIRS * tq
    ch = BAND_CHUNK
    n_pad = n_chunks - 1

    @pl.when(pl.program_id(2) == 0)
    def _():
        _build_keys(g, lambda s0: k_ref[0, pl.ds(s0, ch), :].astype(F32),
                    lambda s0: v_ref[0, pl.ds(s0, ch), :].astype(F32),
                    k_ref.shape[1] // ch, kf_ref, kaug, vt, n_pad, ch)

    krow = lax.broadcasted_iota(jnp.int32, (ch, tq), 0)
    qcol = lax.broadcasted_iota(jnp.int32, (ch, tq), 1)
    vrow = lax.broadcasted_iota(jnp.int32, (V_ROWS, rows), 0)
    for sub in range(q_ref.shape[1] // tq):
        i = pl.program_id(2) * (q_ref.shape[1] // tq) + sub
        _fill_queries(qaug.at[sub], q_ref, sub * tq, tq,
                      [jnp.broadcast_to(qf_ref[0, p:p + 1, :], (tq, LANES)) for p in range(PAIRS)])
        kc = kaug[pl.ds(i, n_chunks)].reshape(n_chunks * 2 * ch, 2 * LANES)
        st = _dot_nt(kc, qaug[sub])

        t = i * tq + qcol
        ev, od = [], []
        for u in range(n_chunks):
            spos = (i - n_pad + u) * ch + krow
            dist = t - spos
            vis = jnp.where(dist >= 0, jnp.where(dist < window, jnp.where(spos >= 0, 1.0, 0.0), 0.0), 0.0)
            bias = _tile_pairs(jnp.where(vis > 0.5, 0.0, NEG))
            ev.append(st[(2 * u) * ch:(2 * u + 1) * ch] + bias)
            od.append(st[(2 * u + 1) * ch:(2 * u + 2) * ch] + bias)

        tpos = (i * tq + lax.broadcasted_iota(jnp.int32, (1, tq), 1)).astype(F32)
        vmat = jnp.concatenate([vt[i + u] for u in range(n_chunks)], axis=1)
        accs = []
        for h, slabs in enumerate((ev, od)):
            m = slabs[0].max(axis=0, keepdims=True)
            for x in slabs[1:]:
                m = jnp.maximum(m, x.max(axis=0, keepdims=True))
            if has_sink:
                sk = jnp.concatenate(
                    [sink_ref[g * HEADS_PER_GROUP + 2 * p + h]
                     + slope_ref[g * HEADS_PER_GROUP + 2 * p + h] * tpos for p in range(PAIRS)], axis=1)
                m = jnp.maximum(m, sk)
            pmat = jnp.concatenate([jnp.exp(x - m).astype(vt.dtype) for x in slabs], axis=0)
            acc = _dot(vmat, pmat)
            if has_sink:
                acc = acc + jnp.where(vrow >= HEAD_DIM, jnp.exp(sk - m), 0.0)
            accs.append(acc)
        _finish(o_ref, sub * tq, tq, accs[0], accs[1], gate_ref, gate_col)


def _banded_attention(q, kv, k_col, v_col, qfeat, kfeat, window, sinks=None, slopes=None,
                      gates=None, gate_col=None):
    b, s, _ = q.shape
    tq = Q_BLOCK
    n_sub = tq // Q_SUB
    n_chunks = window // BAND_CHUNK + 1
    n_tot = kfeat.shape[0]
    has_sink = sinks is not None
    kern = functools.partial(_banded_kernel, n_chunks=n_chunks, window=window,
                             has_sink=has_sink, gate_col=gate_col)
    in_specs, args = [], []
    if has_sink:
        in_specs += [pl.BlockSpec(memory_space=pltpu.SMEM)] * 2
        args += [sinks, slopes]
    in_specs += [pl.BlockSpec((1, tq, PAIRS * LANES), lambda bi, g, i: (bi, i, g)),
                 pl.BlockSpec((1, s, LANES), lambda bi, g, i: (bi, 0, k_col)),
                 pl.BlockSpec((1, s, LANES), lambda bi, g, i: (bi, 0, v_col)),
                 pl.BlockSpec((1, PAIRS, LANES), lambda bi, g, i: (g, 0, 0)),
                 _const_spec(kfeat.shape)]
    args += [q, kv, kv, qfeat, kfeat]
    if gate_col is not None:
        in_specs.append(pl.BlockSpec((1, tq, LANES), lambda bi, g, i: (bi, i, g)))
        args.append(gates)
    return pl.pallas_call(
        kern,
        grid=(b, N_KV_GROUPS, s // tq),
        in_specs=in_specs,
        out_specs=pl.BlockSpec((1, tq, PAIRS * LANES), lambda bi, g, i: (bi, i, g)),
        out_shape=jax.ShapeDtypeStruct(q.shape, MXU_DT),
        scratch_shapes=[pltpu.VMEM((n_tot, 2, BAND_CHUNK, 2 * LANES), MXU_DT),
                        pltpu.VMEM((n_tot, V_ROWS, BAND_CHUNK), MXU_DT),
                        pltpu.VMEM((n_sub, PAIRS * Q_SUB, 2 * LANES), MXU_DT)],
        compiler_params=_params(("arbitrary", "arbitrary", "arbitrary")),
        name="banded_attention_w%d" % window,
    )(*args)


def _cmpsel_kernel(q_ref, k_ref, v_ref, qf_ref, kf_ref, ovt_ref, gate_ref, o_ref, mf_ref,
                   kaug, vt, qaug):
    g = pl.program_id(1)
    ncp = k_ref.shape[2]

    @pl.when(pl.program_id(2) == 0)
    def _():
        _build_keys(g, lambda s0: k_ref[0, 0], lambda s0: v_ref[0, 0], 1, kf_ref, kaug, vt, 0, ncp)

    n_sub = q_ref.shape[1] // Q_SUB
    for sub in range(n_sub):
        _cmpsel_block(pl.program_id(2) * n_sub + sub, sub, ncp, q_ref, qf_ref, ovt_ref, gate_ref,
                      o_ref, mf_ref, kaug, vt, qaug)


def _cmpsel_block(i, sub, ncp, q_ref, qf_ref, ovt_ref, gate_ref, o_ref, mf_ref, kaug, vt, qaug):
    tq = Q_SUB
    n_slc = NSA_SLC_LEN
    _fill_queries(qaug.at[sub], q_ref, sub * tq, tq,
                  [jnp.broadcast_to(qf_ref[0, p:p + 1, :], (tq, LANES)) for p in range(PAIRS)])
    st = _dot_nt(kaug[0].reshape(2 * ncp, 2 * LANES), qaug[sub])

    krow = lax.broadcasted_iota(jnp.int32, (ncp, tq), 0)
    qcol = lax.broadcasted_iota(jnp.int32, (ncp, tq), 1)
    t = i * tq + qcol
    cmp_end = krow * NSA_CMP_STRIDE + (NSA_CMP_LEN - 1)
    vis = jnp.where(t >= cmp_end, jnp.where(krow < ncp - 1, 1.0, 0.0), 0.0)
    bias = _tile_pairs(jnp.where(vis > 0.5, 0.0, NEG))

    pn = []
    for h in range(2):
        x = st[h * ncp:(h + 1) * ncp] + bias
        m = x.max(axis=0, keepdims=True)
        m = jnp.where(m > 0.5 * NEG, m, 0.0)
        e = jnp.exp(x - m)
        d = e.sum(axis=0, keepdims=True)
        pn.append(e * (1.0 / jnp.where(d > 0.0, d, 1.0)))
    _finish(o_ref, sub * tq, tq, _dot(vt[0], pn[0].astype(vt.dtype)), _dot(vt[0], pn[1].astype(vt.dtype)),
            gate_ref, 0, normalized=True)

    ps = pn[0] + pn[1]
    psum = ps[:, 0:tq]
    for p in range(1, PAIRS):
        psum = psum + ps[:, p * tq:(p + 1) * tq]
    p_hi = psum.astype(MXU_DT)
    p_lo = (psum - p_hi.astype(F32)).astype(MXU_DT)
    imp = (_dot(ovt_ref[...], p_hi) + _dot(ovt_ref[...], p_lo))[0:n_slc]
    blk = lax.broadcasted_iota(jnp.int32, (n_slc, tq), 0)
    tq_pos = i * tq + lax.broadcasted_iota(jnp.int32, (n_slc, tq), 1)
    cur = lax.shift_right_logical(tq_pos, int(math.log2(NSA_SLC_LEN)))
    forced = jnp.where(blk == 0, 1.0, jnp.where(blk == cur, 1.0, jnp.where(blk == cur - 1, 1.0, 0.0)))
    valid = blk <= cur
    score = jnp.where(valid, jnp.where(forced > 0.5, jnp.inf, imp), -jnp.inf)
    groups = [score[8 * v:8 * v + 8] for v in range(n_slc // 8)]
    ranks = [jnp.zeros((8, tq), F32) for _ in groups]
    sublane = lax.broadcasted_iota(jnp.int32, (8, tq), 0)
    for j in range(n_slc):
        rj = score[j:j + 1, :]
        for v in range(n_slc // 8):
            if 8 * v > j:
                beats = jnp.where(rj >= groups[v], 1.0, 0.0)
            elif 8 * v + 7 < j:
                beats = jnp.where(rj > groups[v], 1.0, 0.0)
            else:
                beats = jnp.where(sublane > (j - 8 * v), jnp.where(rj >= groups[v], 1.0, 0.0),
                                  jnp.where(rj > groups[v], 1.0, 0.0))
            ranks[v] = ranks[v] + beats
    rank = jnp.concatenate(ranks, axis=0)
    keep = jnp.where(valid, jnp.where(rank < float(NSA_TOPK), 1.0, 0.0), 0.0)
    feat = jnp.where(keep > 0.5, 0.0, NEG)
    feat = jnp.concatenate([feat, jnp.zeros((LANES - n_slc, tq), F32)], axis=0)
    mf_ref[0, 0, sub * tq:(sub + 1) * tq, :] = feat.T.astype(mf_ref.dtype)


def _compressed_and_select(q, kvcmp, qfeat, kfeat, ovt, gates):
    b, s, _ = q.shape
    tq = Q_BLOCK
    ncp = kvcmp.shape[2]
    return pl.pallas_call(
        _cmpsel_kernel,
        grid=(b, N_KV_GROUPS, s // tq),
        in_specs=[pl.BlockSpec((1, tq, PAIRS * LANES), lambda bi, g, i: (bi, i, g)),
                  pl.BlockSpec((1, 1, ncp, LANES), lambda bi, g, i: (bi, 0, 0, 0)),
                  pl.BlockSpec((1, 1, ncp, LANES), lambda bi, g, i: (bi, 1, 0, 0)),
                  pl.BlockSpec((1, PAIRS, LANES), lambda bi, g, i: (g, 0, 0)),
                  _const_spec(kfeat.shape), _const_spec(ovt.shape),
                  pl.BlockSpec((1, tq, LANES), lambda bi, g, i: (bi, i, g))],
        out_specs=[pl.BlockSpec((1, tq, PAIRS * LANES), lambda bi, g, i: (bi, i, g)),
                   pl.BlockSpec((1, 1, tq, LANES), lambda bi, g, i: (bi, g, i, 0))],
        out_shape=[jax.ShapeDtypeStruct(q.shape, MXU_DT),
                   jax.ShapeDtypeStruct((b, N_KV_GROUPS, s, LANES), MXU_DT)],
        scratch_shapes=[pltpu.VMEM((1, 2, ncp, 2 * LANES), MXU_DT),
                        pltpu.VMEM((1, V_ROWS, ncp), MXU_DT),
                        pltpu.VMEM((tq // Q_SUB, PAIRS * Q_SUB, 2 * LANES), MXU_DT)],
        compiler_params=_params(("arbitrary", "arbitrary", "arbitrary")),
        name="nsa_compressed_select",
    )(q, kvcmp, kvcmp, qfeat, kfeat, ovt, gates)


def _selected_kernel(q_ref, k_ref, v_ref, qf_ref, mf_ref, kf_ref, gate_ref, o_ref,
                     kaug, vt, qaug, acc_e, acc_o, st_a, st_b):
    g = pl.program_id(1)
    i = pl.program_id(2)
    tq = q_ref.shape[1]
    ch = SLC_CHUNK
    rows = PAIRS * tq

    @pl.when(i == 0)
    def _():
        _build_keys(g, lambda s0: k_ref[0, pl.ds(s0, ch), :].astype(F32),
                    lambda s0: v_ref[0, pl.ds(s0, ch), :].astype(F32),
                    k_ref.shape[1] // ch, kf_ref, kaug, vt, 0, ch)

    mask_feat = mf_ref[0, 0].astype(F32)
    _fill_queries(qaug, q_ref, 0, tq, [mask_feat + qf_ref[0, p:p + 1, :] for p in range(PAIRS)])
    acc_e[...] = jnp.zeros((V_ROWS, rows), F32)
    acc_o[...] = jnp.zeros((V_ROWS, rows), F32)

    def update(x, m_old, acc_ref, vt_c):
        m_new = jnp.maximum(m_old, x.max(axis=0, keepdims=True))
        alpha = jnp.exp(m_old - m_new)
        p = jnp.exp(x - m_new).astype(vt_c.dtype)
        acc_ref[...] = alpha * acc_ref[...] + _dot(vt_c, p)
        return m_new

    def scores_into(buf, c):
        buf[...] = _dot_nt(kaug[c].reshape(2 * ch, 2 * LANES), qaug[...])

    def consume(buf, c, ms, causal):
        xe, xo = buf[0:ch, :], buf[ch:2 * ch, :]
        if causal:
            krow = lax.broadcasted_iota(jnp.int32, (ch, tq), 0)
            qcol = lax.broadcasted_iota(jnp.int32, (ch, tq), 1)
            bias = _tile_pairs(jnp.where(c * ch + krow <= i * tq + qcol, 0.0, NEG))
            xe, xo = xe + bias, xo + bias
        return update(xe, ms[0], acc_e, vt[c]), update(xo, ms[1], acc_o, vt[c])

    def pair(k, ms):
        c = 2 * k
        scores_into(st_b, c + 1)
        ms = consume(st_a, c, ms, False)
        scores_into(st_a, c + 2)
        return consume(st_b, c + 1, ms, False)

    assert tq == 2 * ch
    m0 = jnp.full((1, rows), M_INIT, F32)
    scores_into(st_a, 0)
    ms = lax.fori_loop(0, i, pair, (m0, m0))
    scores_into(st_b, 2 * i + 1)
    consume(st_b, 2 * i + 1, consume(st_a, 2 * i, ms, True), True)
    _finish(o_ref, 0, tq, acc_e[...], acc_o[...], gate_ref, 1)


def _selected_attention(q, kv, k_col, v_col, qfeat, maskfeat, kfeat, gates):
    b, s, _ = q.shape
    tq = Q_BLOCK
    n_tot = kfeat.shape[0]
    return pl.pallas_call(
        _selected_kernel,
        grid=(b, N_KV_GROUPS, s // tq),
        in_specs=[pl.BlockSpec((1, tq, PAIRS * LANES), lambda bi, g, i: (bi, i, g)),
                  pl.BlockSpec((1, s, LANES), lambda bi, g, i: (bi, 0, k_col)),
                  pl.BlockSpec((1, s, LANES), lambda bi, g, i: (bi, 0, v_col)),
                  pl.BlockSpec((1, PAIRS, LANES), lambda bi, g, i: (g, 0, 0)),
                  pl.BlockSpec((1, 1, tq, LANES), lambda bi, g, i: (bi, g, i, 0)),
                  _const_spec(kfeat.shape),
                  pl.BlockSpec((1, tq, LANES), lambda bi, g, i: (bi, i, g))],
        out_specs=pl.BlockSpec((1, tq, PAIRS * LANES), lambda bi, g, i: (bi, i, g)),
        out_shape=jax.ShapeDtypeStruct(q.shape, MXU_DT),
        scratch_shapes=[pltpu.VMEM((n_tot, 2, SLC_CHUNK, 2 * LANES), MXU_DT),
                        pltpu.VMEM((n_tot, V_ROWS, SLC_CHUNK), MXU_DT),
                        pltpu.VMEM((PAIRS * tq, 2 * LANES), MXU_DT),
                        pltpu.VMEM((V_ROWS, PAIRS * tq), F32),
                        pltpu.VMEM((V_ROWS, PAIRS * tq), F32),
                        pltpu.VMEM((2 * SLC_CHUNK, PAIRS * tq), F32),
                        pltpu.VMEM((2 * SLC_CHUNK, PAIRS * tq), F32)],
        compiler_params=_params(("arbitrary", "arbitrary", "arbitrary")),
        name="nsa_selected",
    )(q, kv, kv, qfeat, maskfeat, kfeat, gates)


def _outproj_kernel(oa_ref, oc_ref, os_ref, ow_ref, x_ref, w_ref, b_ref, g_ref, beta_ref,
                    wr_ref, br_ref, x1_ref, route_ref, *, alpha):
    half = oa_ref.shape[1]
    on = (oc_ref[...].astype(F32) + os_ref[...].astype(F32) + ow_ref[...].astype(F32)).astype(MXU_DT)
    mix = _dot(oa_ref[...], w_ref[0:half]) + _dot(on, w_ref[half:]) + b_ref[...]
    x1 = _layer_norm(alpha * x_ref[...] + mix, g_ref[...], beta_ref[...])
    x1_ref[...] = x1

    x_hi = x1.astype(MXU_DT)
    x_lo = (x1 - x_hi.astype(F32)).astype(MXU_DT)
    logits = (_dot(x_hi, wr_ref[0]) + _dot(x_hi, wr_ref[1]) + _dot(x_lo, wr_ref[0])) + br_ref[...]
    lane = lax.broadcasted_iota(jnp.int32, logits.shape, 1)
    lanef = lane.astype(F32)
    big = float(4 * LANES)
    gl = jnp.where(lane < N_GROUPS, logits, -jnp.inf)
    gmax = gl.max(axis=1, keepdims=True)
    gsel = jnp.where(gl == gmax, lanef, big).min(axis=1, keepdims=True)
    g_w = 1.0 / jnp.exp(gl - gmax).sum(axis=1, keepdims=True)
    lo = N_GROUPS + gsel * EXPERTS_PER_GROUP
    el = jnp.where(lanef >= lo, jnp.where(lanef < lo + EXPERTS_PER_GROUP, logits, -jnp.inf), -jnp.inf)
    v1 = el.max(axis=1, keepdims=True)
    i1 = jnp.where(el == v1, lanef, big).min(axis=1, keepdims=True)
    el2 = jnp.where(lanef == i1, -jnp.inf, el)
    v2 = el2.max(axis=1, keepdims=True)
    i2 = jnp.where(el2 == v2, lanef, big).min(axis=1, keepdims=True)
    e2 = jnp.exp(v2 - v1)
    w1 = g_w / (1.0 + e2)
    w2 = g_w * e2 / (1.0 + e2)
    route_ref[...] = jnp.where(lane == 0, i1 - N_GROUPS,
                               jnp.where(lane == 1, i2 - N_GROUPS,
                                         jnp.where(lane == 2, w1, jnp.where(lane == 3, w2, 0.0))))


def _output_projection(oa, oc, os_, ow, x2d, w, b, ln_g, ln_b, wr, br, alpha):
    n, d = x2d.shape
    half = oa.shape[1]
    tm = min(PROJ_TILE, n)
    row = lambda i: (i, 0)
    return pl.pallas_call(
        functools.partial(_outproj_kernel, alpha=alpha),
        grid=(n // tm,),
        in_specs=[pl.BlockSpec((tm, half), row)] * 4 + [pl.BlockSpec((tm, d), row)]
        + [_const_spec(a.shape) for a in (w, b, ln_g, ln_b, wr, br)],
        out_specs=[pl.BlockSpec((tm, d), row), pl.BlockSpec((tm, LANES), row)],
        out_shape=[jax.ShapeDtypeStruct((n, d), F32), jax.ShapeDtypeStruct((n, LANES), F32)],
        compiler_params=_params(("arbitrary",)),
        name="output_projection_ln_route",
    )(oa, oc, os_, ow, x2d, w, b, ln_g, ln_b, wr, br)


def _row_copy(src_hbm, src_row, dst, j, u, sem):
    return pltpu.make_async_copy(src_hbm.at[pl.ds(src_row, 1)], dst.at[j, pl.ds(u, 1)], sem)


def _rows_start(src_hbm, idx_ref, base, dst, sem, n_groups):
    def body(j, carry):
        for u in range(SUBLANES):
            _row_copy(src_hbm, idx_ref[base + j * SUBLANES + u], dst, j, u, sem).start(priority=u % 2)
        return carry
    lax.fori_loop(0, n_groups, body, 0)


def _rows_wait(src_hbm, dst, sem, n_groups):
    def body(j, carry):
        for u in range(SUBLANES):
            _row_copy(src_hbm, 0, dst, j, u, sem).wait()
        return carry
    lax.fori_loop(0, n_groups, body, 0)


def _expert_kernel(blk_e_ref, n_used_ref, first_ref, wslot_ref, next_e_ref, row_tok_ref,
                   x_hbm, wg_hbm, wu_hbm, wd_hbm, o_ref,
                   xbuf, xsem, wg_f, wu_f, wd_f, wsem, wg_b, wu_b, wd_b, *, layer):
    i = pl.program_id(0)
    n_used = n_used_ref[0]
    groups, d = xbuf.shape[1], xbuf.shape[3]
    t = groups * SUBLANES
    hid = wg_b.shape[1]
    used = i < n_used
    slot = lax.rem(i, 2)

    def weight_copies(e, s):
        return [pltpu.make_async_copy(w.at[layer, e], buf.at[s], wsem.at[s])
                for w, buf in ((wg_hbm, wg_f), (wu_hbm, wu_f), (wd_hbm, wd_f))]

    def rows_start(blk, s, group_range):
        for j in group_range:
            for u in range(SUBLANES):
                _row_copy(x_hbm, row_tok_ref[blk * t + j * SUBLANES + u], xbuf.at[s], j, u,
                          xsem.at[s]).start(priority=u % 2)

    def rows_wait(s):
        for j in range(groups):
            for u in range(SUBLANES):
                _row_copy(x_hbm, 0, xbuf.at[s], j, u, xsem.at[s]).wait()

    @pl.when(i == 0)
    def _():
        for c in weight_copies(blk_e_ref[0], 0):
            c.start()
        rows_start(0, 0, range(groups))

    @pl.when(used)
    def _():
        rows_wait(slot)
        ws = wslot_ref[i]

        @pl.when(first_ref[i] == 1)
        def _():
            for c in weight_copies(0, ws):
                c.wait()

            @pl.when(next_e_ref[i] >= 0)
            def _():
                for c in weight_copies(next_e_ref[i], 1 - ws):
                    c.start()

            wg_b[...] = wg_f[ws].astype(wg_b.dtype)
            wu_b[...] = wu_f[ws].astype(wu_b.dtype)
            wd_b[...] = wd_f[ws].astype(wd_b.dtype)

        nxt = jnp.minimum(i + 1, n_used - 1)
        xb = xbuf[slot].reshape(t, d).astype(MXU_DT)
        pieces = 2
        y = None
        for piece in range(pieces):
            cols = slice(piece * hid // pieces, (piece + 1) * hid // pieces)
            gate = _dot(xb, wg_b[:, cols])
            up = _dot(xb, wu_b[:, cols])
            rows_start(nxt, 1 - slot, range(piece * groups // pieces, (piece + 1) * groups // pieces))
            hidden = (gate * jax.nn.sigmoid(gate) * up).astype(MXU_DT)
            part = _dot(hidden, wd_b[cols, :])
            y = part if y is None else y + part
        o_ref[...] = y

        @pl.when(i == n_used - 1)
        def _():
            rows_wait(1 - slot)

    @pl.when(jnp.logical_not(used))
    def _():
        o_ref[...] = jnp.zeros(o_ref.shape, o_ref.dtype)


def _expert_ffn(x1, row_tok, blk_e, n_used, first, wslot, next_e, wg, wu, wd, layer):
    d = x1.shape[1]
    p = row_tok.shape[0]
    t = EXPERT_ROWS
    hid = wg.shape[3]
    any_spec = pl.BlockSpec(memory_space=pl.ANY)
    return pl.pallas_call(
        functools.partial(_expert_kernel, layer=layer),
        grid_spec=pltpu.PrefetchScalarGridSpec(
            num_scalar_prefetch=6,
            grid=(p // t,),
            in_specs=[any_spec, any_spec, any_spec, any_spec],
            out_specs=pl.BlockSpec((t, d), lambda i, *_: (i, 0)),
            scratch_shapes=[pltpu.VMEM((2, t // SUBLANES, SUBLANES, d), F32), pltpu.SemaphoreType.DMA((2,)),
                            pltpu.VMEM((2, d, hid), F32), pltpu.VMEM((2, d, hid), F32),
                            pltpu.VMEM((2, hid, d), F32), pltpu.SemaphoreType.DMA((2,)),
                            pltpu.VMEM((d, hid), MXU_DT), pltpu.VMEM((d, hid), MXU_DT),
                            pltpu.VMEM((hid, d), MXU_DT)]),
        out_shape=jax.ShapeDtypeStruct((p, d), F32),
        compiler_params=_params(("arbitrary",)),
        name="expert_ffn",
    )(blk_e, n_used, first, wslot, next_e, row_tok, x1, wg, wu, wd)


def _combine_kernel(dest_ref, ys_hbm, route_ref, x_ref, g_ref, b_ref, o_ref, ybuf, sem, *, alpha, n_tok):
    i = pl.program_id(0)
    tm, d = x_ref.shape
    groups = tm // SUBLANES
    slot = lax.rem(i, 2)

    def start(blk, s):
        for k in range(2):
            _rows_start(ys_hbm, dest_ref, k * n_tok + blk * tm, ybuf.at[s, k], sem.at[s], groups)

    @pl.when(i == 0)
    def _():
        start(0, 0)

    @pl.when(i + 1 < pl.num_programs(0))
    def _():
        start(i + 1, 1 - slot)

    for k in range(2):
        _rows_wait(ys_hbm, ybuf.at[slot, k], sem.at[slot], groups)
    r = route_ref[...]
    y = r[:, 2:3] * ybuf[slot, 0].reshape(tm, d) + r[:, 3:4] * ybuf[slot, 1].reshape(tm, d)
    o_ref[...] = _layer_norm(alpha * x_ref[...] + y, g_ref[...], b_ref[...])


def _combine(ys, dest_kmajor, route, x1, ln_g, ln_b, alpha):
    n, d = x1.shape
    tm = min(ROW_TILE, n)
    row = lambda i, dst: (i, 0)
    zero = lambda i, dst: (0, 0)
    return pl.pallas_call(
        functools.partial(_combine_kernel, alpha=alpha, n_tok=n),
        grid_spec=pltpu.PrefetchScalarGridSpec(
            num_scalar_prefetch=1,
            grid=(n // tm,),
            in_specs=[pl.BlockSpec(memory_space=pl.ANY),
                      pl.BlockSpec((tm, LANES), row), pl.BlockSpec((tm, d), row),
                      pl.BlockSpec(ln_g.shape, zero), pl.BlockSpec(ln_b.shape, zero)],
            out_specs=pl.BlockSpec((tm, d), row),
            scratch_shapes=[pltpu.VMEM((2, 2, tm // SUBLANES, SUBLANES, d), F32),
                            pltpu.SemaphoreType.DMA((2,))]),
        out_shape=jax.ShapeDtypeStruct((n, d), F32),
        compiler_params=_params(("arbitrary",)),
        name="combine_ln",
    )(dest_kmajor, ys, route, x1, ln_g, ln_b)


def _dispatch_plan(route, n_tokens):
    t = EXPERT_ROWS
    a = 2 * n_tokens
    eid = route[:, 0:2].astype(jnp.int32).reshape(a)
    onehot = (eid[:, None] == jnp.arange(N_EXPERTS, dtype=jnp.int32)[None, :]).astype(jnp.int32)
    counts = onehot.sum(axis=0)
    before = jnp.cumsum(onehot, axis=0) - onehot
    rank = jnp.take_along_axis(before, eid[:, None], axis=1)[:, 0]
    padded = (counts + t - 1) // t * t
    pend = jnp.cumsum(padded)
    pstart = pend - padded
    dest = (pstart[eid] + rank).astype(jnp.int32)
    p_rows = a + N_EXPERTS * t
    filler = jnp.arange(p_rows, dtype=jnp.int32) % n_tokens
    row_tok = filler.at[dest].set(jnp.arange(a, dtype=jnp.int32) // 2)
    blk_start = jnp.arange(p_rows // t, dtype=jnp.int32) * t
    blk_e = jnp.minimum((pend[None, :] <= blk_start[:, None]).astype(jnp.int32).sum(axis=1), N_EXPERTS - 1)
    blk_e = blk_e.astype(jnp.int32)
    n_used = (pend[-1] // t).astype(jnp.int32)
    blk = jnp.arange(p_rows // t, dtype=jnp.int32)
    first = jnp.concatenate([jnp.ones((1,), jnp.int32), (blk_e[1:] != blk_e[:-1]).astype(jnp.int32)])
    wslot = (jnp.cumsum(first) - 1) % 2
    is_used = blk < n_used
    n_le = ((blk_e[None, :] <= blk_e[:, None]) & is_used[None, :]).astype(jnp.int32).sum(axis=1)
    next_e = jnp.where(n_le < n_used, blk_e[jnp.minimum(n_le, p_rows // t - 1)], -1)
    dest_kmajor = dest.reshape(n_tokens, 2).T.reshape(a)
    return (dest_kmajor, row_tok, blk_e, n_used.reshape(1), first, wslot.astype(jnp.int32),
            next_e.astype(jnp.int32))


def kernel(x, w_in, b_in, swa_sinks, cmp_pe_k, cmp_w1_k, cmp_w2_k, cmp_pe_v, cmp_w1_v, cmp_w2_v,
           w_out, b_out, ln1_g, ln1_b, w_group, b_group, w_expert, b_expert, we_gate, we_up,
           we_down, ln2_g, ln2_b):
    bsz, seq, d_model = x.shape
    depth = w_in.shape[0]
    n_tok = bsz * seq
    n_heads = d_model // HEAD_DIM
    q_w = n_heads // 2 * HEAD_DIM
    kv_w = N_KV_GROUPS * HEAD_DIM
    alpha = (2.0 * depth) ** 0.25
    scale = HEAD_DIM ** -0.5
    assert q_w == N_KV_GROUPS * PAIRS * LANES and seq % SLC_CHUNK == 0

    slopes_a, slopes_n = _alibi_slopes(n_heads)
    qfeat_a = jnp.asarray(_query_feats(slopes_a), F32)
    qfeat_n = jnp.asarray(_query_feats(slopes_n), F32)
    pos = np.arange(seq)
    kfeat_swa = jnp.asarray(_key_feats(pos, SWA_WINDOW // BAND_CHUNK, BAND_CHUNK, False), MXU_DT)
    kfeat_win = jnp.asarray(_key_feats(pos, NSA_WINDOW // BAND_CHUNK, BAND_CHUNK, False), MXU_DT)
    kfeat_slc = jnp.asarray(_key_feats(pos, 0, SLC_CHUNK, True), MXU_DT)
    n_cmp = (seq - NSA_CMP_LEN) // NSA_CMP_STRIDE + 1
    n_cmp_pad = seq // NSA_CMP_STRIDE
    cmp_end = np.arange(n_cmp_pad) * NSA_CMP_STRIDE + NSA_CMP_LEN - 1
    kfeat_cmp = jnp.asarray(_key_feats(cmp_end, 0, n_cmp_pad, False), MXU_DT)
    n_slc = seq // NSA_SLC_LEN
    cs = np.arange(n_cmp_pad)[None, :] * NSA_CMP_STRIDE
    ss = np.arange(LANES)[:, None] * NSA_SLC_LEN
    ovt = ((cs < ss + NSA_SLC_LEN) & (cs + NSA_CMP_LEN - 1 >= ss)
           & (np.arange(n_cmp_pad)[None, :] < n_cmp) & (np.arange(LANES)[:, None] < n_slc))
    ovt = jnp.asarray(ovt.astype(np.float32), MXU_DT)
    slopes_a_s = jnp.asarray(slopes_a, F32)

    o_qa, o_ka = 0, q_w
    o_qn = q_w + 2 * kv_w
    o_kvn = o_qn + q_w
    o_gn = o_kvn + 6 * kv_w
    gate_w = 3 * HEADS_PER_GROUP

    h = x.reshape(n_tok, d_model)
    for l in range(depth):
        wl, bl = w_in[l], b_in[l]

        def seg(a, lo, hi):
            return a[..., lo:hi]

        def gate_cols(a):
            pad = [(0, 0)] * (a.ndim - 1) + [(0, LANES - gate_w)]
            return jnp.concatenate([jnp.pad(seg(a, o_gn + g * gate_w, o_gn + (g + 1) * gate_w), pad)
                                    for g in range(N_KV_GROUPS)], axis=-1)

        def regroup(a):
            return jnp.concatenate([seg(a, o_qa, o_qa + q_w) * scale, seg(a, o_qn, o_qn + q_w) * scale,
                                    seg(a, o_ka, o_ka + 2 * kv_w), seg(a, o_kvn, o_kvn + 6 * kv_w),
                                    gate_cols(a)], axis=-1)

        w_r = regroup(wl).astype(MXU_DT)
        b_r = regroup(bl)[None, :]
        qa, qn, kv, gates = _input_projection(h, w_r, b_r)
        qa = qa.reshape(bsz, seq, q_w)
        qn = qn.reshape(bsz, seq, q_w)
        kv = kv.reshape(bsz, seq, 8 * kv_w)
        gates = gates.reshape(bsz, seq, 2 * LANES)

        o_a = _banded_attention(qa, kv, 0, 1, qfeat_a, kfeat_swa, SWA_WINDOW,
                                sinks=swa_sinks[l], slopes=slopes_a_s)

        kvc = kv[:, :, 2 * kv_w:4 * kv_w].reshape(bsz, n_cmp_pad, NSA_CMP_STRIDE, 2, N_KV_GROUPS, HEAD_DIM)
        kvc = kvc.transpose(0, 3, 4, 1, 2, 5).reshape(bsz, 2, N_KV_GROUPS, n_cmp_pad, NSA_CMP_STRIDE * HEAD_DIM)
        pe = jnp.stack([cmp_pe_k[l], cmp_pe_v[l]]).reshape(2, 1, NSA_CMP_LEN * HEAD_DIM)
        pe = jnp.broadcast_to(pe, (2, 8, NSA_CMP_LEN * HEAD_DIM))
        w1 = jnp.stack([cmp_w1_k[l], cmp_w1_v[l]]).astype(MXU_DT)
        w2 = jnp.stack([cmp_w2_k[l], cmp_w2_v[l]]).astype(MXU_DT)
        kvcmp = _compress(kvc, pe, w1, w2)

        o_cmp, maskfeat = _compressed_and_select(qn, kvcmp, qfeat_n, kfeat_cmp, ovt, gates)
        o_slc = _selected_attention(qn, kv, 4, 5, qfeat_n, maskfeat, kfeat_slc, gates)
        o_win = _banded_attention(qn, kv, 6, 7, qfeat_n, kfeat_win, NSA_WINDOW,
                                  gates=gates, gate_col=2)

        wr = jnp.concatenate([w_group[l], w_expert[l],
                              jnp.zeros((d_model, LANES - N_GROUPS - N_EXPERTS), F32)], axis=1)
        br = jnp.concatenate([b_group[l], b_expert[l],
                              jnp.zeros((LANES - N_GROUPS - N_EXPERTS,), F32)])[None, :]
        wr_hi = wr.astype(MXU_DT)
        wr = jnp.stack([wr_hi, (wr - wr_hi.astype(F32)).astype(MXU_DT)])
        x1, route = _output_projection(
            o_a.reshape(n_tok, q_w), o_cmp.reshape(n_tok, q_w), o_slc.reshape(n_tok, q_w),
            o_win.reshape(n_tok, q_w), h, w_out[l].astype(MXU_DT), b_out[l][None, :],
            ln1_g[l][None, :], ln1_b[l][None, :], wr, br, alpha)

        dest_kmajor, row_tok, blk_e, n_used, first, wslot, next_e = _dispatch_plan(route, n_tok)
        ys = _expert_ffn(x1, row_tok, blk_e, n_used, first, wslot, next_e, we_gate, we_up, we_down, l)
        h = _combine(ys, dest_kmajor, route, x1, ln2_g[l][None, :], ln2_b[l][None, :], alpha)
    return h.reshape(bsz, seq, d_model)
```

```python
import functools
import math

import numpy as np
import jax
import jax.numpy as jnp
from jax import lax
from jax.experimental import pallas as pl
from jax.experimental.pallas import tpu as pltpu

F32 = jnp.float32
MXU_DT = jnp.bfloat16

HEAD_DIM = 64
HEADS_PER_GROUP = 8
PAIRS = HEADS_PER_GROUP // 2
N_KV_GROUPS = 2
SWA_WINDOW = 128
NSA_WINDOW = 512
NSA_CMP_LEN = 32
NSA_CMP_STRIDE = 16
NSA_CMP_HIDDEN = 256
NSA_SLC_LEN = 64
NSA_TOPK = 16
N_GROUPS = 8
EXPERTS_PER_GROUP = 8
N_EXPERTS = N_GROUPS * EXPERTS_PER_GROUP
EXPERT_HIDDEN = 512
LN_EPS = 1e-5

LANES = 128
SUBLANES = 8
V_ROWS = 80
Q_BLOCK = 512
Q_SUB = 128
BAND_CHUNK = 128
SLC_CHUNK = 256
EXPERT_ROWS = 256
ROW_TILE = 256
PROJ_TILE = 512
LOG2E = 1.4426950408889634
NEG = -1e30
M_INIT = -5e29
VMEM_LIMIT = 56 * 1024 * 1024

FEAT_EVEN = 64
FEAT_ODD = 72


def _alibi_slopes(n_heads_total):
    n = n_heads_total
    s = np.exp2(-8.0 * np.arange(1, n + 1, dtype=np.float32) / np.float32(n)).astype(np.float32)
    return s[0::2], s[1::2]


def _bf16_parts(v):
    v = np.asarray(v, np.float32)
    a = v.astype(jnp.bfloat16).astype(np.float32)
    r = (v - a).astype(np.float32)
    b = r.astype(jnp.bfloat16).astype(np.float32)
    c = (r - b).astype(np.float32)
    return a, b, c


def _query_feats(slopes):
    out = np.zeros((N_KV_GROUPS, PAIRS, LANES), np.float32)
    a, b, c = _bf16_parts(slopes)
    for g in range(N_KV_GROUPS):
        for p in range(PAIRS):
            for h, base in ((0, FEAT_EVEN), (1, FEAT_ODD)):
                idx = g * HEADS_PER_GROUP + 2 * p + h
                out[g, p, base:base + 6] = [a[idx], b[idx], c[idx], a[idx], b[idx], c[idx]]
    return out


def _key_feats(positions, n_pad_chunks, chunk, with_blocks):
    pos = np.asarray(positions, np.int64)
    n = pos.shape[0]
    assert n % chunk == 0
    hi = (pos // 64 * 64).astype(np.float32)
    lo = (pos % 64).astype(np.float32)
    f = np.zeros((n, 2, LANES), np.float32)
    for v, base in ((0, FEAT_EVEN), (1, FEAT_ODD)):
        f[:, v, base:base + 3] = hi[:, None]
        f[:, v, base + 3:base + 6] = lo[:, None]
    if with_blocks:
        blk = pos // NSA_SLC_LEN
        for v in range(2):
            f[np.arange(n), v, blk] = 1.0
    f = f.reshape(n // chunk, chunk, 2, LANES).transpose(0, 2, 1, 3)
    if n_pad_chunks:
        f = np.concatenate([np.zeros((n_pad_chunks,) + f.shape[1:], np.float32), f], axis=0)
    return f


def _params(sem, vmem=VMEM_LIMIT):
    return pltpu.CompilerParams(dimension_semantics=sem, vmem_limit_bytes=vmem)


def _const_spec(shape):
    nd = len(shape)
    return pl.BlockSpec(shape, lambda *_: (0,) * nd, pipeline_mode=pl.Buffered(1))


def _dot(a, b):
    return jnp.dot(a, b, preferred_element_type=F32)


def _dot_nt(a, b):
    return lax.dot_general(a, b, (((1,), (1,)), ((), ())), preferred_element_type=F32)


def _layer_norm(h, g, b):
    mu = jnp.mean(h, axis=-1, keepdims=True)
    d = h - mu
    var = jnp.mean(d * d, axis=-1, keepdims=True)
    return d * lax.rsqrt(var + LN_EPS) * g + b


def _inproj_kernel(x_ref, w_ref, b_ref, qa_ref, qn_ref, kv_ref, gt_ref):
    xb = x_ref[...].astype(MXU_DT)
    width = qa_ref.shape[1]
    for j, o_ref in enumerate((qa_ref, qn_ref, kv_ref)):
        acc = _dot(xb, w_ref[:, j * width:(j + 1) * width]) + b_ref[:, j * width:(j + 1) * width]
        o_ref[...] = acc.astype(o_ref.dtype)
    gt_ref[...] = _dot(xb, w_ref[:, 3 * width:]) + b_ref[:, 3 * width:]


def _input_projection(x2d, w, b):
    n, d = x2d.shape
    width = (w.shape[1] - 2 * LANES) // 3
    tm = min(PROJ_TILE, n)
    row = lambda i: (i, 0)
    return pl.pallas_call(
        _inproj_kernel,
        grid=(n // tm,),
        in_specs=[pl.BlockSpec((tm, d), row), _const_spec(w.shape), _const_spec(b.shape)],
        out_specs=[pl.BlockSpec((tm, width), row)] * 3 + [pl.BlockSpec((tm, 2 * LANES), row)],
        out_shape=[jax.ShapeDtypeStruct((n, width), MXU_DT)] * 3
        + [jax.ShapeDtypeStruct((n, 2 * LANES), F32)],
        compiler_params=_params(("arbitrary",)),
        name="input_projection",
    )(x2d, w, b)


def _gelu_tanh(x):
    c = math.sqrt(2.0 / math.pi)
    return 0.5 * x * (1.0 + jnp.tanh(c * (x + 0.044715 * (x * x * x))))


def _compress_kernel(c_ref, pe_ref, w1_ref, w2_ref, o_ref):
    half = c_ref.shape[-1]
    w1 = w1_ref[0]
    pe = pe_ref[0]
    pe_hi = pe.astype(MXU_DT)
    pe_lo = (pe - pe_hi.astype(F32)).astype(MXU_DT)
    base = (_dot(pe_hi, w1) + _dot(pe_lo, w1))[0:1]
    for g in range(N_KV_GROUPS):
        c = c_ref[0, 0, g]
        top = _dot(c, w1[:half])
        bot = _dot(c, w1[half:])
        n_rows = bot.shape[0]
        hidden = top + pltpu.roll(bot, n_rows - 1, 0) + base
        act = _gelu_tanh(hidden)
        o_ref[0, 0, :, g * HEAD_DIM:(g + 1) * HEAD_DIM] = _dot(act.astype(MXU_DT), w2_ref[0])


def _compress(chunks, pe, w1, w2):
    b, two, g, nch, half = chunks.shape
    return pl.pallas_call(
        _compress_kernel,
        grid=(b, two),
        in_specs=[pl.BlockSpec((1, 1, g, nch, half), lambda i, j: (i, j, 0, 0, 0)),
                  pl.BlockSpec((1, 8, 2 * half), lambda i, j: (j, 0, 0)),
                  pl.BlockSpec((1, 2 * half, NSA_CMP_HIDDEN), lambda i, j: (j, 0, 0)),
                  pl.BlockSpec((1, NSA_CMP_HIDDEN, HEAD_DIM), lambda i, j: (j, 0, 0))],
        out_specs=pl.BlockSpec((1, 1, nch, LANES), lambda i, j: (i, j, 0, 0)),
        out_shape=jax.ShapeDtypeStruct((b, two, nch, LANES), F32),
        compiler_params=_params(("arbitrary", "arbitrary")),
        name="nsa_compress",
    )(chunks, pe, w1, w2)


def _place_group(x, g_is_zero):
    lane = lax.broadcasted_iota(jnp.int32, x.shape, 1)
    rolled = pltpu.roll(x, HEAD_DIM, 1)
    lo_src, hi_src = (x, rolled) if g_is_zero else (rolled, x)
    lo = jnp.where(lane < HEAD_DIM, lo_src, 0.0)
    hi = jnp.where(lane >= HEAD_DIM, hi_src, 0.0)
    return lo, hi


def _build_keys(g, load_k, load_v, n_chunks, kf_ref, kaug, vt, n_pad, chunk):
    ones = jnp.ones((V_ROWS - HEAD_DIM, chunk), F32)
    for gval in range(N_KV_GROUPS):
        @pl.when(g == gval)
        def _():
            def body(c, carry):
                start = pl.multiple_of(c * chunk, chunk)
                klo, khi = _place_group(load_k(start), gval == 0)
                vlo, _ = _place_group(load_v(start), gval == 0)
                kaug[n_pad + c, 0, :, 0:LANES] = klo.astype(kaug.dtype)
                kaug[n_pad + c, 1, :, 0:LANES] = khi.astype(kaug.dtype)
                vt[n_pad + c] = jnp.concatenate([vlo.T[0:HEAD_DIM], ones], axis=0).astype(vt.dtype)
                return carry
            lax.fori_loop(0, n_chunks, body, 0)
    kaug[:, :, :, LANES:2 * LANES] = kf_ref[...]
    if n_pad:
        kaug[0:n_pad, :, :, 0:LANES] = jnp.zeros((n_pad, 2, chunk, LANES), kaug.dtype)
        vt[0:n_pad] = jnp.zeros((n_pad, V_ROWS, chunk), vt.dtype)


def _fill_queries(qaug, q_ref, r0, tq, feats):
    for p in range(PAIRS):
        qaug[p * tq:(p + 1) * tq, 0:LANES] = q_ref[0, r0:r0 + tq, p * LANES:(p + 1) * LANES]
        qaug[p * tq:(p + 1) * tq, LANES:2 * LANES] = feats[p].astype(qaug.dtype)


def _tile_pairs(x):
    return jnp.concatenate([x] * PAIRS, axis=1)


def _finish(o_ref, r0, tq, acct_e, acct_o, gate_ref, gate_col, normalized=False):
    lane = lax.broadcasted_iota(jnp.int32, (tq, LANES), 1)
    if gate_ref is not None:
        gsig = jax.nn.sigmoid(gate_ref[0, r0:r0 + tq, :])
    for p in range(PAIRS):
        cols = slice(p * tq, (p + 1) * tq)
        top = jnp.concatenate([acct_e[0:HEAD_DIM, cols], acct_o[0:HEAD_DIM, cols]], axis=0)
        if not normalized:
            den = jnp.concatenate(
                [jnp.broadcast_to(acct_e[HEAD_DIM:HEAD_DIM + 1, cols], (HEAD_DIM, tq)),
                 jnp.broadcast_to(acct_o[HEAD_DIM:HEAD_DIM + 1, cols], (HEAD_DIM, tq))], axis=0)
            top = top / den
        out = top.T
        if gate_ref is not None:
            ce = 3 * (2 * p) + gate_col
            co = 3 * (2 * p + 1) + gate_col
            out = out * jnp.where(lane < HEAD_DIM, gsig[:, ce:ce + 1], gsig[:, co:co + 1])
        o_ref[0, r0:r0 + tq, p * LANES:(p + 1) * LANES] = out.astype(o_ref.dtype)


def _banded_kernel(*refs, n_chunks, window, has_sink, gate_col):
    refs = list(refs)
    if has_sink:
        sink_ref, slope_ref = refs[0], refs[1]
        refs = refs[2:]
    q_ref, k_ref, v_ref, qf_ref, kf_ref = refs[:5]
    refs = refs[5:]
    gate_ref = None
    if gate_col is not None:
        gate_ref = refs[0]
        refs = refs[1:]
    o_ref, kaug, vt, qaug = refs
    g = pl.program_id(1)
    tq = Q_SUB
    rows = PAIRS * tq
    ch = BAND_CHUNK
    n_pad = n_chunks - 1

    @pl.when(pl.program_id(2) == 0)
    def _():
        _build_keys(g, lambda s0: k_ref[0, pl.ds(s0, ch), :].astype(F32),
                    lambda s0: v_ref[0, pl.ds(s0, ch), :].astype(F32),
                    k_ref.shape[1] // ch, kf_ref, kaug, vt, n_pad, ch)

    krow = lax.broadcasted_iota(jnp.int32, (ch, tq), 0)
    qcol = lax.broadcasted_iota(jnp.int32, (ch, tq), 1)
    vrow = lax.broadcasted_iota(jnp.int32, (V_ROWS, rows), 0)
    for sub in range(q_ref.shape[1] // tq):
        i = pl.program_id(2) * (q_ref.shape[1] // tq) + sub
        _fill_queries(qaug.at[sub], q_ref, sub * tq, tq,
                      [jnp.broadcast_to(qf_ref[0, p:p + 1, :], (tq, LANES)) for p in range(PAIRS)])
        kc = kaug[pl.ds(i, n_chunks)].reshape(n_chunks * 2 * ch, 2 * LANES)
        st = _dot_nt(kc, qaug[sub])

        t = i * tq + qcol
        ev, od = [], []
        for u in range(n_chunks):
            spos = (i - n_pad + u) * ch + krow
            dist = t - spos
            vis = jnp.where(dist >= 0, jnp.where(dist < window, jnp.where(spos >= 0, 1.0, 0.0), 0.0), 0.0)
            bias = _tile_pairs(jnp.where(vis > 0.5, 0.0, NEG))
            ev.append(st[(2 * u) * ch:(2 * u + 1) * ch] + bias)
            od.append(st[(2 * u + 1) * ch:(2 * u + 2) * ch] + bias)

        tpos = (i * tq + lax.broadcasted_iota(jnp.int32, (1, tq), 1)).astype(F32)
        vmat = jnp.concatenate([vt[i + u] for u in range(n_chunks)], axis=1)
        accs = []
        for h, slabs in enumerate((ev, od)):
            m = slabs[0].max(axis=0, keepdims=True)
            for x in slabs[1:]:
                m = jnp.maximum(m, x.max(axis=0, keepdims=True))
            if has_sink:
                sk = jnp.concatenate(
                    [sink_ref[g * HEADS_PER_GROUP + 2 * p + h] * LOG2E
                     + slope_ref[g * HEADS_PER_GROUP + 2 * p + h] * tpos for p in range(PAIRS)], axis=1)
                m = jnp.maximum(m, sk)
            pmat = jnp.concatenate([jnp.exp2(x - m).astype(vt.dtype) for x in slabs], axis=0)
            acc = _dot(vmat, pmat)
            if has_sink:
                acc = acc + jnp.where(vrow >= HEAD_DIM, jnp.exp2(sk - m), 0.0)
            accs.append(acc)
        _finish(o_ref, sub * tq, tq, accs[0], accs[1], gate_ref, gate_col)


def _banded_attention(q, kv, k_col, v_col, qfeat, kfeat, window, sinks=None, slopes=None,
                      gates=None, gate_col=None):
    b, s, _ = q.shape
    tq = Q_BLOCK
    n_sub = tq // Q_SUB
    n_chunks = window // BAND_CHUNK + 1
    n_tot = kfeat.shape[0]
    has_sink = sinks is not None
    kern = functools.partial(_banded_kernel, n_chunks=n_chunks, window=window,
                             has_sink=has_sink, gate_col=gate_col)
    in_specs, args = [], []
    if has_sink:
        in_specs += [pl.BlockSpec(memory_space=pltpu.SMEM)] * 2
        args += [sinks, slopes]
    in_specs += [pl.BlockSpec((1, tq, PAIRS * LANES), lambda bi, g, i: (bi, i, g)),
                 pl.BlockSpec((1, s, LANES), lambda bi, g, i: (bi, 0, k_col)),
                 pl.BlockSpec((1, s, LANES), lambda bi, g, i: (bi, 0, v_col)),
                 pl.BlockSpec((1, PAIRS, LANES), lambda bi, g, i: (g, 0, 0)),
                 _const_spec(kfeat.shape)]
    args += [q, kv, kv, qfeat, kfeat]
    if gate_col is not None:
        in_specs.append(pl.BlockSpec((1, tq, LANES), lambda bi, g, i: (bi, i, g)))
        args.append(gates)
    return pl.pallas_call(
        kern,
        grid=(b, N_KV_GROUPS, s // tq),
        in_specs=in_specs,
        out_specs=pl.BlockSpec((1, tq, PAIRS * LANES), lambda bi, g, i: (bi, i, g)),
        out_shape=jax.ShapeDtypeStruct(q.shape, MXU_DT),
        scratch_shapes=[pltpu.VMEM((n_tot, 2, BAND_CHUNK, 2 * LANES), MXU_DT),
                        pltpu.VMEM((n_tot, V_ROWS, BAND_CHUNK), MXU_DT),
                        pltpu.VMEM((n_sub, PAIRS * Q_SUB, 2 * LANES), MXU_DT)],
        compiler_params=_params(("arbitrary", "arbitrary", "arbitrary")),
        name="banded_attention_w%d" % window,
    )(*args)


def _cmpsel_kernel(q_ref, k_ref, v_ref, qf_ref, kf_ref, ovt_ref, gate_ref, o_ref, mf_ref,
                   kaug, vt, qaug):
    g = pl.program_id(1)
    ncp = k_ref.shape[2]

    @pl.when(pl.program_id(2) == 0)
    def _():
        _build_keys(g, lambda s0: k_ref[0, 0], lambda s0: v_ref[0, 0], 1, kf_ref, kaug, vt, 0, ncp)

    n_sub = q_ref.shape[1] // Q_SUB
    for sub in range(n_sub):
        _cmpsel_block(pl.program_id(2) * n_sub + sub, sub, ncp, q_ref, qf_ref, ovt_ref, gate_ref,
                      o_ref, mf_ref, kaug, vt, qaug)


def _cmpsel_block(i, sub, ncp, q_ref, qf_ref, ovt_ref, gate_ref, o_ref, mf_ref, kaug, vt, qaug):
    tq = Q_SUB
    n_slc = NSA_SLC_LEN
    _fill_queries(qaug.at[sub], q_ref, sub * tq, tq,
                  [jnp.broadcast_to(qf_ref[0, p:p + 1, :], (tq, LANES)) for p in range(PAIRS)])
    st = _dot_nt(kaug[0].reshape(2 * ncp, 2 * LANES), qaug[sub])

    krow = lax.broadcasted_iota(jnp.int32, (ncp, tq), 0)
    qcol = lax.broadcasted_iota(jnp.int32, (ncp, tq), 1)
    t = i * tq + qcol
    cmp_end = krow * NSA_CMP_STRIDE + (NSA_CMP_LEN - 1)
    vis = jnp.where(t >= cmp_end, jnp.where(krow < ncp - 1, 1.0, 0.0), 0.0)
    bias = _tile_pairs(jnp.where(vis > 0.5, 0.0, NEG))

    pn = []
    for h in range(2):
        x = st[h * ncp:(h + 1) * ncp] + bias
        m = x.max(axis=0, keepdims=True)
        m = jnp.where(m > 0.5 * NEG, m, 0.0)
        e = jnp.exp2(x - m)
        d = e.sum(axis=0, keepdims=True)
        pn.append(e * (1.0 / jnp.where(d > 0.0, d, 1.0)))
    _finish(o_ref, sub * tq, tq, _dot(vt[0], pn[0].astype(vt.dtype)), _dot(vt[0], pn[1].astype(vt.dtype)),
            gate_ref, 0, normalized=True)

    ps = pn[0] + pn[1]
    psum = ps[:, 0:tq]
    for p in range(1, PAIRS):
        psum = psum + ps[:, p * tq:(p + 1) * tq]
    p_hi = psum.astype(MXU_DT)
    p_lo = (psum - p_hi.astype(F32)).astype(MXU_DT)
    imp = (_dot(ovt_ref[...], p_hi) + _dot(ovt_ref[...], p_lo))[0:n_slc]
    blk = lax.broadcasted_iota(jnp.int32, (n_slc, tq), 0)
    tq_pos = i * tq + lax.broadcasted_iota(jnp.int32, (n_slc, tq), 1)
    cur = lax.shift_right_logical(tq_pos, int(math.log2(NSA_SLC_LEN)))
    forced = jnp.where(blk == 0, 1.0, jnp.where(blk == cur, 1.0, jnp.where(blk == cur - 1, 1.0, 0.0)))
    valid = blk <= cur
    score = jnp.where(valid, jnp.where(forced > 0.5, jnp.inf, imp), -jnp.inf)
    groups = [score[8 * v:8 * v + 8] for v in range(n_slc // 8)]
    ranks = [jnp.zeros((8, tq), F32) for _ in groups]
    sublane = lax.broadcasted_iota(jnp.int32, (8, tq), 0)
    for j in range(n_slc):
        rj = score[j:j + 1, :]
        for v in range(n_slc // 8):
            if 8 * v > j:
                beats = jnp.where(rj >= groups[v], 1.0, 0.0)
            elif 8 * v + 7 < j:
                beats = jnp.where(rj > groups[v], 1.0, 0.0)
            else:
                beats = jnp.where(sublane > (j - 8 * v), jnp.where(rj >= groups[v], 1.0, 0.0),
                                  jnp.where(rj > groups[v], 1.0, 0.0))
            ranks[v] = ranks[v] + beats
    rank = jnp.concatenate(ranks, axis=0)
    keep = jnp.where(valid, jnp.where(rank < float(NSA_TOPK), 1.0, 0.0), 0.0)
    feat = jnp.where(keep > 0.5, 0.0, NEG)
    feat = jnp.concatenate([feat, jnp.zeros((LANES - n_slc, tq), F32)], axis=0)
    mf_ref[0, 0, sub * tq:(sub + 1) * tq, :] = feat.T.astype(mf_ref.dtype)


def _compressed_and_select(q, kvcmp, qfeat, kfeat, ovt, gates):
    b, s, _ = q.shape
    tq = Q_BLOCK
    ncp = kvcmp.shape[2]
    return pl.pallas_call(
        _cmpsel_kernel,
        grid=(b, N_KV_GROUPS, s // tq),
        in_specs=[pl.BlockSpec((1, tq, PAIRS * LANES), lambda bi, g, i: (bi, i, g)),
                  pl.BlockSpec((1, 1, ncp, LANES), lambda bi, g, i: (bi, 0, 0, 0)),
                  pl.BlockSpec((1, 1, ncp, LANES), lambda bi, g, i: (bi, 1, 0, 0)),
                  pl.BlockSpec((1, PAIRS, LANES), lambda bi, g, i: (g, 0, 0)),
                  _const_spec(kfeat.shape), _const_spec(ovt.shape),
                  pl.BlockSpec((1, tq, LANES), lambda bi, g, i: (bi, i, g))],
        out_specs=[pl.BlockSpec((1, tq, PAIRS * LANES), lambda bi, g, i: (bi, i, g)),
                   pl.BlockSpec((1, 1, tq, LANES), lambda bi, g, i: (bi, g, i, 0))],
        out_shape=[jax.ShapeDtypeStruct(q.shape, MXU_DT),
                   jax.ShapeDtypeStruct((b, N_KV_GROUPS, s, LANES), MXU_DT)],
        scratch_shapes=[pltpu.VMEM((1, 2, ncp, 2 * LANES), MXU_DT),
                        pltpu.VMEM((1, V_ROWS, ncp), MXU_DT),
                        pltpu.VMEM((tq // Q_SUB, PAIRS * Q_SUB, 2 * LANES), MXU_DT)],
        compiler_params=_params(("arbitrary", "arbitrary", "arbitrary")),
        name="nsa_compressed_select",
    )(q, kvcmp, kvcmp, qfeat, kfeat, ovt, gates)


def _selected_kernel(q_ref, k_ref, v_ref, qf_ref, mf_ref, kf_ref, gate_ref, o_ref,
                     kaug, vt, qaug, acc_e, acc_o, st_a, st_b):
    g = pl.program_id(1)
    i = pl.program_id(2)
    tq = q_ref.shape[1]
    ch = SLC_CHUNK
    rows = PAIRS * tq

    @pl.when(i == 0)
    def _():
        _build_keys(g, lambda s0: k_ref[0, pl.ds(s0, ch), :].astype(F32),
                    lambda s0: v_ref[0, pl.ds(s0, ch), :].astype(F32),
                    k_ref.shape[1] // ch, kf_ref, kaug, vt, 0, ch)

    mask_feat = mf_ref[0, 0].astype(F32)
    _fill_queries(qaug, q_ref, 0, tq, [mask_feat + qf_ref[0, p:p + 1, :] for p in range(PAIRS)])
    acc_e[...] = jnp.zeros((V_ROWS, rows), F32)
    acc_o[...] = jnp.zeros((V_ROWS, rows), F32)

    def update(x, m_old, acc_ref, vt_c):
        m_new = jnp.maximum(m_old, x.max(axis=0, keepdims=True))
        alpha = jnp.exp2(m_old - m_new)
        p = jnp.exp2(x - m_new).astype(vt_c.dtype)
        acc_ref[...] = alpha * acc_ref[...] + _dot(vt_c, p)
        return m_new

    def scores_into(buf, c):
        buf[...] = _dot_nt(kaug[c].reshape(2 * ch, 2 * LANES), qaug[...])

    def consume(buf, c, ms, causal):
        xe, xo = buf[0:ch, :], buf[ch:2 * ch, :]
        if causal:
            krow = lax.broadcasted_iota(jnp.int32, (ch, tq), 0)
            qcol = lax.broadcasted_iota(jnp.int32, (ch, tq), 1)
            bias = _tile_pairs(jnp.where(c * ch + krow <= i * tq + qcol, 0.0, NEG))
            xe, xo = xe + bias, xo + bias
        return update(xe, ms[0], acc_e, vt[c]), update(xo, ms[1], acc_o, vt[c])

    def pair(k, ms):
        c = 2 * k
        scores_into(st_b, c + 1)
        ms = consume(st_a, c, ms, False)
        scores_into(st_a, c + 2)
        return consume(st_b, c + 1, ms, False)

    assert tq == 2 * ch
    m0 = jnp.full((1, rows), M_INIT, F32)
    scores_into(st_a, 0)
    ms = lax.fori_loop(0, i, pair, (m0, m0))
    scores_into(st_b, 2 * i + 1)
    consume(st_b, 2 * i + 1, consume(st_a, 2 * i, ms, True), True)
    _finish(o_ref, 0, tq, acc_e[...], acc_o[...], gate_ref, 1)


def _selected_attention(q, kv, k_col, v_col, qfeat, maskfeat, kfeat, gates):
    b, s, _ = q.shape
    tq = Q_BLOCK
    n_tot = kfeat.shape[0]
    return pl.pallas_call(
        _selected_kernel,
        grid=(b, N_KV_GROUPS, s // tq),
        in_specs=[pl.BlockSpec((1, tq, PAIRS * LANES), lambda bi, g, i: (bi, i, g)),
                  pl.BlockSpec((1, s, LANES), lambda bi, g, i: (bi, 0, k_col)),
                  pl.BlockSpec((1, s, LANES), lambda bi, g, i: (bi, 0, v_col)),
                  pl.BlockSpec((1, PAIRS, LANES), lambda bi, g, i: (g, 0, 0)),
                  pl.BlockSpec((1, 1, tq, LANES), lambda bi, g, i: (bi, g, i, 0)),
                  _const_spec(kfeat.shape),
                  pl.BlockSpec((1, tq, LANES), lambda bi, g, i: (bi, i, g))],
        out_specs=pl.BlockSpec((1, tq, PAIRS * LANES), lambda bi, g, i: (bi, i, g)),
        out_shape=jax.ShapeDtypeStruct(q.shape, MXU_DT),
        scratch_shapes=[pltpu.VMEM((n_tot, 2, SLC_CHUNK, 2 * LANES), MXU_DT),
                        pltpu.VMEM((n_tot, V_ROWS, SLC_CHUNK), MXU_DT),
                        pltpu.VMEM((PAIRS * tq, 2 * LANES), MXU_DT),
                        pltpu.VMEM((V_ROWS, PAIRS * tq), F32),
                        pltpu.VMEM((V_ROWS, PAIRS * tq), F32),
                        pltpu.VMEM((2 * SLC_CHUNK, PAIRS * tq), F32),
                        pltpu.VMEM((2 * SLC_CHUNK, PAIRS * tq), F32)],
        compiler_params=_params(("arbitrary", "arbitrary", "arbitrary")),
        name="nsa_selected",
    )(q, kv, kv, qfeat, maskfeat, kfeat, gates)


def _outproj_kernel(oa_ref, oc_ref, os_ref, ow_ref, x_ref, w_ref, b_ref, g_ref, beta_ref,
                    wr_ref, br_ref, x1_ref, route_ref, *, alpha):
    half = oa_ref.shape[1]
    on = (oc_ref[...].astype(F32) + os_ref[...].astype(F32) + ow_ref[...].astype(F32)).astype(MXU_DT)
    mix = _dot(oa_ref[...], w_ref[0:half]) + _dot(on, w_ref[half:]) + b_ref[...]
    x1 = _layer_norm(alpha * x_ref[...] + mix, g_ref[...], beta_ref[...])
    x1_ref[...] = x1

    x_hi = x1.astype(MXU_DT)
    x_lo = (x1 - x_hi.astype(F32)).astype(MXU_DT)
    logits = (_dot(x_hi, wr_ref[0]) + _dot(x_hi, wr_ref[1]) + _dot(x_lo, wr_ref[0])) + br_ref[...]
    lane = lax.broadcasted_iota(jnp.int32, logits.shape, 1)
    lanef = lane.astype(F32)
    big = float(4 * LANES)
    gl = jnp.where(lane < N_GROUPS, logits, -jnp.inf)
    gmax = gl.max(axis=1, keepdims=True)
    gsel = jnp.where(gl == gmax, lanef, big).min(axis=1, keepdims=True)
    g_w = 1.0 / jnp.exp(gl - gmax).sum(axis=1, keepdims=True)
    lo = N_GROUPS + gsel * EXPERTS_PER_GROUP
    el = jnp.where(lanef >= lo, jnp.where(lanef < lo + EXPERTS_PER_GROUP, logits, -jnp.inf), -jnp.inf)
    v1 = el.max(axis=1, keepdims=True)
    i1 = jnp.where(el == v1, lanef, big).min(axis=1, keepdims=True)
    el2 = jnp.where(lanef == i1, -jnp.inf, el)
    v2 = el2.max(axis=1, keepdims=True)
    i2 = jnp.where(el2 == v2, lanef, big).min(axis=1, keepdims=True)
    e2 = jnp.exp(v2 - v1)
    w1 = g_w / (1.0 + e2)
    w2 = g_w * e2 / (1.0 + e2)
    route_ref[...] = jnp.where(lane == 0, i1 - N_GROUPS,
                               jnp.where(lane == 1, i2 - N_GROUPS,
                                         jnp.where(lane == 2, w1, jnp.where(lane == 3, w2, 0.0))))


def _output_projection(oa, oc, os_, ow, x2d, w, b, ln_g, ln_b, wr, br, alpha):
    n, d = x2d.shape
    half = oa.shape[1]
    tm = min(PROJ_TILE, n)
    row = lambda i: (i, 0)
    return pl.pallas_call(
        functools.partial(_outproj_kernel, alpha=alpha),
        grid=(n // tm,),
        in_specs=[pl.BlockSpec((tm, half), row)] * 4 + [pl.BlockSpec((tm, d), row)]
        + [_const_spec(a.shape) for a in (w, b, ln_g, ln_b, wr, br)],
        out_specs=[pl.BlockSpec((tm, d), row), pl.BlockSpec((tm, LANES), row)],
        out_shape=[jax.ShapeDtypeStruct((n, d), F32), jax.ShapeDtypeStruct((n, LANES), F32)],
        compiler_params=_params(("arbitrary",)),
        name="output_projection_ln_route",
    )(oa, oc, os_, ow, x2d, w, b, ln_g, ln_b, wr, br)


def _row_copy(src_hbm, src_row, dst, j, u, sem):
    return pltpu.make_async_copy(src_hbm.at[pl.ds(src_row, 1)], dst.at[j, pl.ds(u, 1)], sem)


def _rows_start(src_hbm, idx_ref, base, dst, sem, n_groups):
    def body(j, carry):
        for u in range(SUBLANES):
            _row_copy(src_hbm, idx_ref[base + j * SUBLANES + u], dst, j, u, sem).start(priority=u % 2)
        return carry
    lax.fori_loop(0, n_groups, body, 0)


def _rows_wait(src_hbm, dst, sem, n_groups):
    def body(j, carry):
        for u in range(SUBLANES):
            _row_copy(src_hbm, 0, dst, j, u, sem).wait()
        return carry
    lax.fori_loop(0, n_groups, body, 0)


def _expert_kernel(blk_e_ref, n_used_ref, first_ref, wslot_ref, next_e_ref, row_tok_ref,
                   x_hbm, wg_hbm, wu_hbm, wd_hbm, o_ref,
                   xbuf, xsem, wg_f, wu_f, wd_f, wsem, wg_b, wu_b, wd_b, *, layer):
    i = pl.program_id(0)
    n_used = n_used_ref[0]
    groups, d = xbuf.shape[1], xbuf.shape[3]
    t = groups * SUBLANES
    hid = wg_b.shape[1]
    used = i < n_used
    slot = lax.rem(i, 2)

    def weight_copies(e, s):
        return [pltpu.make_async_copy(w.at[layer, e], buf.at[s], wsem.at[s])
                for w, buf in ((wg_hbm, wg_f), (wu_hbm, wu_f), (wd_hbm, wd_f))]

    def rows_start(blk, s, group_range):
        for j in group_range:
            for u in range(SUBLANES):
                _row_copy(x_hbm, row_tok_ref[blk * t + j * SUBLANES + u], xbuf.at[s], j, u,
                          xsem.at[s]).start()

    def rows_wait(s):
        for j in range(groups):
            for u in range(SUBLANES):
                _row_copy(x_hbm, 0, xbuf.at[s], j, u, xsem.at[s]).wait()

    @pl.when(i == 0)
    def _():
        for c in weight_copies(blk_e_ref[0], 0):
            c.start(priority=1)
        rows_start(0, 0, range(groups))

    @pl.when(used)
    def _():
        rows_wait(slot)
        ws = wslot_ref[i]

        @pl.when(first_ref[i] == 1)
        def _():
            for c in weight_copies(0, ws):
                c.wait()

            @pl.when(next_e_ref[i] >= 0)
            def _():
                for c in weight_copies(next_e_ref[i], 1 - ws):
                    c.start(priority=1)

            wg_b[...] = wg_f[ws].astype(wg_b.dtype)
            wu_b[...] = wu_f[ws].astype(wu_b.dtype)
            wd_b[...] = wd_f[ws].astype(wd_b.dtype)

        nxt = jnp.minimum(i + 1, n_used - 1)
        xb = xbuf[slot].reshape(t, d).astype(MXU_DT)
        pieces = 2
        y = None
        for piece in range(pieces):
            cols = slice(piece * hid // pieces, (piece + 1) * hid // pieces)
            gate = _dot(xb, wg_b[:, cols])
            up = _dot(xb, wu_b[:, cols])
            rows_start(nxt, 1 - slot, range(piece * groups // pieces, (piece + 1) * groups // pieces))
            hidden = (gate * jax.nn.sigmoid(gate) * up).astype(MXU_DT)
            part = _dot(hidden, wd_b[cols, :])
            y = part if y is None else y + part
        o_ref[...] = y

        @pl.when(i == n_used - 1)
        def _():
            rows_wait(1 - slot)

    @pl.when(jnp.logical_not(used))
    def _():
        o_ref[...] = jnp.zeros(o_ref.shape, o_ref.dtype)


def _expert_ffn(x1, row_tok, blk_e, n_used, first, wslot, next_e, wg, wu, wd, layer):
    d = x1.shape[1]
    p = row_tok.shape[0]
    t = EXPERT_ROWS
    hid = wg.shape[3]
    any_spec = pl.BlockSpec(memory_space=pl.ANY)
    return pl.pallas_call(
        functools.partial(_expert_kernel, layer=layer),
        grid_spec=pltpu.PrefetchScalarGridSpec(
            num_scalar_prefetch=6,
            grid=(p // t,),
            in_specs=[any_spec, any_spec, any_spec, any_spec],
            out_specs=pl.BlockSpec((t, d), lambda i, *_: (i, 0)),
            scratch_shapes=[pltpu.VMEM((2, t // SUBLANES, SUBLANES, d), F32), pltpu.SemaphoreType.DMA((2,)),
                            pltpu.VMEM((2, d, hid), F32), pltpu.VMEM((2, d, hid), F32),
                            pltpu.VMEM((2, hid, d), F32), pltpu.SemaphoreType.DMA((2,)),
                            pltpu.VMEM((d, hid), MXU_DT), pltpu.VMEM((d, hid), MXU_DT),
                            pltpu.VMEM((hid, d), MXU_DT)]),
        out_shape=jax.ShapeDtypeStruct((p, d), F32),
        compiler_params=_params(("arbitrary",)),
        name="expert_ffn",
    )(blk_e, n_used, first, wslot, next_e, row_tok, x1, wg, wu, wd)


def _combine_kernel(dest_ref, ys_hbm, route_ref, x_ref, g_ref, b_ref, o_ref, ybuf, sem, *, alpha, n_tok):
    i = pl.program_id(0)
    tm, d = x_ref.shape
    groups = tm // SUBLANES
    slot = lax.rem(i, 2)

    def start(blk, s):
        for k in range(2):
            _rows_start(ys_hbm, dest_ref, k * n_tok + blk * tm, ybuf.at[s, k], sem.at[s], groups)

    @pl.when(i == 0)
    def _():
        start(0, 0)

    @pl.when(i + 1 < pl.num_programs(0))
    def _():
        start(i + 1, 1 - slot)

    for k in range(2):
        _rows_wait(ys_hbm, ybuf.at[slot, k], sem.at[slot], groups)
    r = route_ref[...]
    y = r[:, 2:3] * ybuf[slot, 0].reshape(tm, d) + r[:, 3:4] * ybuf[slot, 1].reshape(tm, d)
    o_ref[...] = _layer_norm(alpha * x_ref[...] + y, g_ref[...], b_ref[...])


def _combine(ys, dest_kmajor, route, x1, ln_g, ln_b, alpha):
    n, d = x1.shape
    tm = min(ROW_TILE, n)
    row = lambda i, dst: (i, 0)
    zero = lambda i, dst: (0, 0)
    return pl.pallas_call(
        functools.partial(_combine_kernel, alpha=alpha, n_tok=n),
        grid_spec=pltpu.PrefetchScalarGridSpec(
            num_scalar_prefetch=1,
            grid=(n // tm,),
            in_specs=[pl.BlockSpec(memory_space=pl.ANY),
                      pl.BlockSpec((tm, LANES), row), pl.BlockSpec((tm, d), row),
                      pl.BlockSpec(ln_g.shape, zero), pl.BlockSpec(ln_b.shape, zero)],
            out_specs=pl.BlockSpec((tm, d), row),
            scratch_shapes=[pltpu.VMEM((2, 2, tm // SUBLANES, SUBLANES, d), F32),
                            pltpu.SemaphoreType.DMA((2,))]),
        out_shape=jax.ShapeDtypeStruct((n, d), F32),
        compiler_params=_params(("arbitrary",)),
        name="combine_ln",
    )(dest_kmajor, ys, route, x1, ln_g, ln_b)


def _dispatch_plan(route, n_tokens):
    t = EXPERT_ROWS
    a = 2 * n_tokens
    eid = route[:, 0:2].astype(jnp.int32).reshape(a)
    onehot = (eid[:, None] == jnp.arange(N_EXPERTS, dtype=jnp.int32)[None, :]).astype(jnp.int32)
    counts = onehot.sum(axis=0)
    before = jnp.cumsum(onehot, axis=0) - onehot
    rank = jnp.take_along_axis(before, eid[:, None], axis=1)[:, 0]
    padded = (counts + t - 1) // t * t
    pend = jnp.cumsum(padded)
    pstart = pend - padded
    dest = (pstart[eid] + rank).astype(jnp.int32)
    p_rows = a + N_EXPERTS * t
    filler = jnp.arange(p_rows, dtype=jnp.int32) % n_tokens
    row_tok = filler.at[dest].set(jnp.arange(a, dtype=jnp.int32) // 2)
    blk_start = jnp.arange(p_rows // t, dtype=jnp.int32) * t
    blk_e = jnp.minimum((pend[None, :] <= blk_start[:, None]).astype(jnp.int32).sum(axis=1), N_EXPERTS - 1)
    blk_e = blk_e.astype(jnp.int32)
    n_used = (pend[-1] // t).astype(jnp.int32)
    blk = jnp.arange(p_rows // t, dtype=jnp.int32)
    first = jnp.concatenate([jnp.ones((1,), jnp.int32), (blk_e[1:] != blk_e[:-1]).astype(jnp.int32)])
    wslot = (jnp.cumsum(first) - 1) % 2
    is_used = blk < n_used
    n_le = ((blk_e[None, :] <= blk_e[:, None]) & is_used[None, :]).astype(jnp.int32).sum(axis=1)
    next_e = jnp.where(n_le < n_used, blk_e[jnp.minimum(n_le, p_rows // t - 1)], -1)
    dest_kmajor = dest.reshape(n_tokens, 2).T.reshape(a)
    return (dest_kmajor, row_tok, blk_e, n_used.reshape(1), first, wslot.astype(jnp.int32),
            next_e.astype(jnp.int32))


def kernel(x, w_in, b_in, swa_sinks, cmp_pe_k, cmp_w1_k, cmp_w2_k, cmp_pe_v, cmp_w1_v, cmp_w2_v,
           w_out, b_out, ln1_g, ln1_b, w_group, b_group, w_expert, b_expert, we_gate, we_up,
           we_down, ln2_g, ln2_b):
    bsz, seq, d_model = x.shape
    depth = w_in.shape[0]
    n_tok = bsz * seq
    n_heads = d_model // HEAD_DIM
    q_w = n_heads // 2 * HEAD_DIM
    kv_w = N_KV_GROUPS * HEAD_DIM
    alpha = (2.0 * depth) ** 0.25
    scale = HEAD_DIM ** -0.5 * LOG2E
    assert q_w == N_KV_GROUPS * PAIRS * LANES and seq % SLC_CHUNK == 0

    slopes_a, slopes_n = _alibi_slopes(n_heads)
    slopes_a = (slopes_a * np.float32(LOG2E)).astype(np.float32)
    slopes_n = (slopes_n * np.float32(LOG2E)).astype(np.float32)
    qfeat_a = jnp.asarray(_query_feats(slopes_a), F32)
    qfeat_n = jnp.asarray(_query_feats(slopes_n), F32)
    pos = np.arange(seq)
    kfeat_swa = jnp.asarray(_key_feats(pos, SWA_WINDOW // BAND_CHUNK, BAND_CHUNK, False), MXU_DT)
    kfeat_win = jnp.asarray(_key_feats(pos, NSA_WINDOW // BAND_CHUNK, BAND_CHUNK, False), MXU_DT)
    kfeat_slc = jnp.asarray(_key_feats(pos, 0, SLC_CHUNK, True), MXU_DT)
    n_cmp = (seq - NSA_CMP_LEN) // NSA_CMP_STRIDE + 1
    n_cmp_pad = seq // NSA_CMP_STRIDE
    cmp_end = np.arange(n_cmp_pad) * NSA_CMP_STRIDE + NSA_CMP_LEN - 1
    kfeat_cmp = jnp.asarray(_key_feats(cmp_end, 0, n_cmp_pad, False), MXU_DT)
    n_slc = seq // NSA_SLC_LEN
    cs = np.arange(n_cmp_pad)[None, :] * NSA_CMP_STRIDE
    ss = np.arange(LANES)[:, None] * NSA_SLC_LEN
    ovt = ((cs < ss + NSA_SLC_LEN) & (cs + NSA_CMP_LEN - 1 >= ss)
           & (np.arange(n_cmp_pad)[None, :] < n_cmp) & (np.arange(LANES)[:, None] < n_slc))
    ovt = jnp.asarray(ovt.astype(np.float32), MXU_DT)
    slopes_a_s = jnp.asarray(slopes_a, F32)

    o_qa, o_ka = 0, q_w
    o_qn = q_w + 2 * kv_w
    o_kvn = o_qn + q_w
    o_gn = o_kvn + 6 * kv_w
    gate_w = 3 * HEADS_PER_GROUP

    h = x.reshape(n_tok, d_model)
    for l in range(depth):
        wl, bl = w_in[l], b_in[l]

        def seg(a, lo, hi):
            return a[..., lo:hi]

        def gate_cols(a):
            pad = [(0, 0)] * (a.ndim - 1) + [(0, LANES - gate_w)]
            return jnp.concatenate([jnp.pad(seg(a, o_gn + g * gate_w, o_gn + (g + 1) * gate_w), pad)
                                    for g in range(N_KV_GROUPS)], axis=-1)

        def regroup(a):
            return jnp.concatenate([seg(a, o_qa, o_qa + q_w) * scale, seg(a, o_qn, o_qn + q_w) * scale,
                                    seg(a, o_ka, o_ka + 2 * kv_w), seg(a, o_kvn, o_kvn + 6 * kv_w),
                                    gate_cols(a)], axis=-1)

        w_r = regroup(wl).astype(MXU_DT)
        b_r = regroup(bl)[None, :]
        qa, qn, kv, gates = _input_projection(h, w_r, b_r)
        qa = qa.reshape(bsz, seq, q_w)
        qn = qn.reshape(bsz, seq, q_w)
        kv = kv.reshape(bsz, seq, 8 * kv_w)
        gates = gates.reshape(bsz, seq, 2 * LANES)

        o_a = _banded_attention(qa, kv, 0, 1, qfeat_a, kfeat_swa, SWA_WINDOW,
                                sinks=swa_sinks[l], slopes=slopes_a_s)

        kvc = kv[:, :, 2 * kv_w:4 * kv_w].reshape(bsz, n_cmp_pad, NSA_CMP_STRIDE, 2, N_KV_GROUPS, HEAD_DIM)
        kvc = kvc.transpose(0, 3, 4, 1, 2, 5).reshape(bsz, 2, N_KV_GROUPS, n_cmp_pad, NSA_CMP_STRIDE * HEAD_DIM)
        pe = jnp.stack([cmp_pe_k[l], cmp_pe_v[l]]).reshape(2, 1, NSA_CMP_LEN * HEAD_DIM)
        pe = jnp.broadcast_to(pe, (2, 8, NSA_CMP_LEN * HEAD_DIM))
        w1 = jnp.stack([cmp_w1_k[l], cmp_w1_v[l]]).astype(MXU_DT)
        w2 = jnp.stack([cmp_w2_k[l], cmp_w2_v[l]]).astype(MXU_DT)
        kvcmp = _compress(kvc, pe, w1, w2)

        o_cmp, maskfeat = _compressed_and_select(qn, kvcmp, qfeat_n, kfeat_cmp, ovt, gates)
        o_slc = _selected_attention(qn, kv, 4, 5, qfeat_n, maskfeat, kfeat_slc, gates)
        o_win = _banded_attention(qn, kv, 6, 7, qfeat_n, kfeat_win, NSA_WINDOW,
                                  gates=gates, gate_col=2)

        wr = jnp.concatenate([w_group[l], w_expert[l],
                              jnp.zeros((d_model, LANES - N_GROUPS - N_EXPERTS), F32)], axis=1)
        br = jnp.concatenate([b_group[l], b_expert[l],
                              jnp.zeros((LANES - N_GROUPS - N_EXPERTS,), F32)])[None, :]
        wr_hi = wr.astype(MXU_DT)
        wr = jnp.stack([wr_hi, (wr - wr_hi.astype(F32)).astype(MXU_DT)])
        x1, route = _output_projection(
            o_a.reshape(n_tok, q_w), o_cmp.reshape(n_tok, q_w), o_slc.reshape(n_tok, q_w),
            o_win.reshape(n_tok, q_w), h, w_out[l].astype(MXU_DT), b_out[l][None, :],
            ln1_g[l][None, :], ln1_b[l][None, :], wr, br, alpha)

        dest_kmajor, row_tok, blk_e, n_used, first, wslot, next_e = _dispatch_plan(route, n_tok)
        ys = _expert_ffn(x1, row_tok, blk_e, n_used, first, wslot, next_e, we_gate, we_up, we_down, l)
        h = _combine(ys, dest_kmajor, route, x1, ln2_g[l][None, :], ln2_b[l][None, :], alpha)
    return h.reshape(bsz, seq, d_model)
```

```python
import functools
import math

import numpy as np
import jax
import jax.numpy as jnp
from jax import lax
from jax.experimental import pallas as pl
from jax.experimental.pallas import tpu as pltpu

F32 = jnp.float32
MXU_DT = jnp.bfloat16

HEAD_DIM = 64
HEADS_PER_GROUP = 8
PAIRS = HEADS_PER_GROUP // 2
N_KV_GROUPS = 2
SWA_WINDOW = 128
NSA_WINDOW = 512
NSA_CMP_LEN = 32
NSA_CMP_STRIDE = 16
NSA_CMP_HIDDEN = 256
NSA_SLC_LEN = 64
NSA_TOPK = 16
N_GROUPS = 8
EXPERTS_PER_GROUP = 8
N_EXPERTS = N_GROUPS * EXPERTS_PER_GROUP
EXPERT_HIDDEN = 512
LN_EPS = 1e-5

LANES = 128
SUBLANES = 8
V_ROWS = 80
Q_BLOCK = 512
Q_SUB = 128
BAND_CHUNK = 128
SLC_CHUNK = 512
EXPERT_ROWS = 256
ROW_TILE = 256
PROJ_TILE = 512
LOG2E = 1.4426950408889634
NEG = -1e30
M_INIT = -5e29
VMEM_LIMIT = 56 * 1024 * 1024

FEAT_EVEN = 64
FEAT_ODD = 72


def _alibi_slopes(n_heads_total):
    n = n_heads_total
    s = np.exp2(-8.0 * np.arange(1, n + 1, dtype=np.float32) / np.float32(n)).astype(np.float32)
    return s[0::2], s[1::2]


def _bf16_parts(v):
    v = np.asarray(v, np.float32)
    a = v.astype(jnp.bfloat16).astype(np.float32)
    r = (v - a).astype(np.float32)
    b = r.astype(jnp.bfloat16).astype(np.float32)
    c = (r - b).astype(np.float32)
    return a, b, c


def _query_feats(slopes):
    out = np.zeros((N_KV_GROUPS, PAIRS, LANES), np.float32)
    a, b, c = _bf16_parts(slopes)
    for g in range(N_KV_GROUPS):
        for p in range(PAIRS):
            for h, base in ((0, FEAT_EVEN), (1, FEAT_ODD)):
                idx = g * HEADS_PER_GROUP + 2 * p + h
                out[g, p, base:base + 6] = [a[idx], b[idx], c[idx], a[idx], b[idx], c[idx]]
    return out


def _key_feats(positions, n_pad_chunks, chunk, with_blocks):
    pos = np.asarray(positions, np.int64)
    n = pos.shape[0]
    assert n % chunk == 0
    hi = (pos // 64 * 64).astype(np.float32)
    lo = (pos % 64).astype(np.float32)
    f = np.zeros((n, 2, LANES), np.float32)
    for v, base in ((0, FEAT_EVEN), (1, FEAT_ODD)):
        f[:, v, base:base + 3] = hi[:, None]
        f[:, v, base + 3:base + 6] = lo[:, None]
    if with_blocks:
        blk = pos // NSA_SLC_LEN
        for v in range(2):
            f[np.arange(n), v, blk] = 1.0
    f = f.reshape(n // chunk, chunk, 2, LANES).transpose(0, 2, 1, 3)
    if n_pad_chunks:
        f = np.concatenate([np.zeros((n_pad_chunks,) + f.shape[1:], np.float32), f], axis=0)
    return f


def _params(sem, vmem=VMEM_LIMIT):
    return pltpu.CompilerParams(dimension_semantics=sem, vmem_limit_bytes=vmem)


def _const_spec(shape):
    nd = len(shape)
    return pl.BlockSpec(shape, lambda *_: (0,) * nd, pipeline_mode=pl.Buffered(1))


def _layer_spec(arr, layer):
    nd = arr.ndim
    return pl.BlockSpec((None,) + arr.shape[1:], lambda *_: (layer,) + (0,) * (nd - 1),
                        pipeline_mode=pl.Buffered(1))


def _dot(a, b):
    return jnp.dot(a, b, preferred_element_type=F32)


def _dot_nt(a, b):
    return lax.dot_general(a, b, (((1,), (1,)), ((), ())), preferred_element_type=F32)


def _layer_norm(h, g, b):
    mu = jnp.mean(h, axis=-1, keepdims=True)
    d = h - mu
    var = jnp.mean(d * d, axis=-1, keepdims=True)
    return d * lax.rsqrt(var + LN_EPS) * g + b


def _inproj_kernel(x_ref, w_ref, b_ref, qa_ref, qn_ref, kv_ref, gt_ref):
    xb = x_ref[...].astype(MXU_DT)
    width = qa_ref.shape[1]
    for j, o_ref in enumerate((qa_ref, qn_ref, kv_ref)):
        acc = _dot(xb, w_ref[:, j * width:(j + 1) * width]) + b_ref[:, j * width:(j + 1) * width]
        o_ref[...] = acc.astype(o_ref.dtype)
    gt_ref[...] = _dot(xb, w_ref[:, 3 * width:]) + b_ref[:, 3 * width:]


def _input_projection(x2d, w, b, layer):
    n, d = x2d.shape
    width = (w.shape[2] - 2 * LANES) // 3
    tm = min(PROJ_TILE, n)
    row = lambda i: (i, 0)
    return pl.pallas_call(
        _inproj_kernel,
        grid=(n // tm,),
        in_specs=[pl.BlockSpec((tm, d), row), _layer_spec(w, layer), _layer_spec(b, layer)],
        out_specs=[pl.BlockSpec((tm, width), row)] * 3 + [pl.BlockSpec((tm, 2 * LANES), row)],
        out_shape=[jax.ShapeDtypeStruct((n, width), MXU_DT)] * 3
        + [jax.ShapeDtypeStruct((n, 2 * LANES), F32)],
        compiler_params=_params(("arbitrary",)),
        name="input_projection",
    )(x2d, w, b)


def _gelu_tanh(x):
    c = math.sqrt(2.0 / math.pi)
    return 0.5 * x * (1.0 + jnp.tanh(c * (x + 0.044715 * (x * x * x))))


def _compress_kernel(c_ref, pe_ref, w1_ref, w2_ref, o_ref):
    half = c_ref.shape[-1]
    w1 = w1_ref[0]
    pe = pe_ref[0]
    pe_hi = pe.astype(MXU_DT)
    pe_lo = (pe - pe_hi.astype(F32)).astype(MXU_DT)
    base = (_dot(pe_hi, w1) + _dot(pe_lo, w1))[0:1]
    for g in range(N_KV_GROUPS):
        c = c_ref[0, 0, g]
        top = _dot(c, w1[:half])
        bot = _dot(c, w1[half:])
        n_rows = bot.shape[0]
        hidden = top + pltpu.roll(bot, n_rows - 1, 0) + base
        act = _gelu_tanh(hidden)
        o_ref[0, 0, :, g * HEAD_DIM:(g + 1) * HEAD_DIM] = _dot(act.astype(MXU_DT), w2_ref[0])


def _compress(chunks, pe, w1, w2, layer):
    b, two, g, nch, half = chunks.shape
    return pl.pallas_call(
        _compress_kernel,
        grid=(b, two),
        in_specs=[pl.BlockSpec((1, 1, g, nch, half), lambda i, j: (i, j, 0, 0, 0)),
                  pl.BlockSpec((None, 1, 8, 2 * half), lambda i, j: (layer, j, 0, 0)),
                  pl.BlockSpec((None, 1, 2 * half, NSA_CMP_HIDDEN), lambda i, j: (layer, j, 0, 0)),
                  pl.BlockSpec((None, 1, NSA_CMP_HIDDEN, HEAD_DIM), lambda i, j: (layer, j, 0, 0))],
        out_specs=pl.BlockSpec((1, 1, nch, LANES), lambda i, j: (i, j, 0, 0)),
        out_shape=jax.ShapeDtypeStruct((b, two, nch, LANES), F32),
        compiler_params=_params(("arbitrary", "arbitrary")),
        name="nsa_compress",
    )(chunks, pe, w1, w2)


def _place_group(x, g_is_zero):
    lane = lax.broadcasted_iota(jnp.int32, x.shape, 1)
    rolled = pltpu.roll(x, HEAD_DIM, 1)
    lo_src, hi_src = (x, rolled) if g_is_zero else (rolled, x)
    lo = jnp.where(lane < HEAD_DIM, lo_src, 0.0)
    hi = jnp.where(lane >= HEAD_DIM, hi_src, 0.0)
    return lo, hi


def _build_keys(g, load_k, load_v, n_chunks, kf_ref, kaug, vt, n_pad, chunk):
    ones = jnp.ones((V_ROWS - HEAD_DIM, chunk), F32)
    for gval in range(N_KV_GROUPS):
        @pl.when(g == gval)
        def _():
            def body(c, carry):
                start = pl.multiple_of(c * chunk, chunk)
                klo, khi = _place_group(load_k(start), gval == 0)
                vlo, _ = _place_group(load_v(start), gval == 0)
                kaug[n_pad + c, 0, :, 0:LANES] = klo.astype(kaug.dtype)
                kaug[n_pad + c, 1, :, 0:LANES] = khi.astype(kaug.dtype)
                vt[n_pad + c] = jnp.concatenate([vlo.T[0:HEAD_DIM], ones], axis=0).astype(vt.dtype)
                return carry
            lax.fori_loop(0, n_chunks, body, 0)
    kaug[:, :, :, LANES:2 * LANES] = kf_ref[...]
    if n_pad:
        kaug[0:n_pad, :, :, 0:LANES] = jnp.zeros((n_pad, 2, chunk, LANES), kaug.dtype)
        vt[0:n_pad] = jnp.zeros((n_pad, V_ROWS, chunk), vt.dtype)


def _fill_queries(qaug, q_ref, r0, tq, feats):
    for p in range(PAIRS):
        qaug[p * tq:(p + 1) * tq, 0:LANES] = q_ref[0, r0:r0 + tq, p * LANES:(p + 1) * LANES]
        qaug[p * tq:(p + 1) * tq, LANES:2 * LANES] = feats[p].astype(qaug.dtype)


def _tile_pairs(x):
    return jnp.concatenate([x] * PAIRS, axis=1)


def _finish(o_ref, r0, tq, acct_e, acct_o, gate_ref, gate_col, normalized=False):
    lane = lax.broadcasted_iota(jnp.int32, (tq, LANES), 1)
    if gate_ref is not None:
        gsig = jax.nn.sigmoid(gate_ref[0, r0:r0 + tq, :])
    for p in range(PAIRS):
        cols = slice(p * tq, (p + 1) * tq)
        top = jnp.concatenate([acct_e[0:HEAD_DIM, cols], acct_o[0:HEAD_DIM, cols]], axis=0)
        if not normalized:
            den = jnp.concatenate(
                [jnp.broadcast_to(acct_e[HEAD_DIM:HEAD_DIM + 1, cols], (HEAD_DIM, tq)),
                 jnp.broadcast_to(acct_o[HEAD_DIM:HEAD_DIM + 1, cols], (HEAD_DIM, tq))], axis=0)
            top = top / den
        out = top.T
        if gate_ref is not None:
            ce = 3 * (2 * p) + gate_col
            co = 3 * (2 * p + 1) + gate_col
            out = out * jnp.where(lane < HEAD_DIM, gsig[:, ce:ce + 1], gsig[:, co:co + 1])
        o_ref[0, r0:r0 + tq, p * LANES:(p + 1) * LANES] = out.astype(o_ref.dtype)


def _banded_kernel(*refs, n_chunks, window, has_sink, gate_col, layer):
    refs = list(refs)
    if has_sink:
        sink_ref, slope_ref = refs[0], refs[1]
        refs = refs[2:]
    q_ref, k_ref, v_ref, qf_ref, kf_ref = refs[:5]
    refs = refs[5:]
    gate_ref = None
    if gate_col is not None:
        gate_ref = refs[0]
        refs = refs[1:]
    o_ref, kaug, vt, qaug = refs
    g = pl.program_id(1)
    tq = Q_SUB
    rows = PAIRS * tq
    ch = BAND_CHUNK
    n_pad = n_chunks - 1

    @pl.when(pl.program_id(2) == 0)
    def _():
        _build_keys(g, lambda s0: k_ref[0, pl.ds(s0, ch), :].astype(F32),
                    lambda s0: v_ref[0, pl.ds(s0, ch), :].astype(F32),
                    k_ref.shape[1] // ch, kf_ref, kaug, vt, n_pad, ch)

    krow = lax.broadcasted_iota(jnp.int32, (ch, tq), 0)
    qcol = lax.broadcasted_iota(jnp.int32, (ch, tq), 1)
    vrow = lax.broadcasted_iota(jnp.int32, (V_ROWS, rows), 0)
    for sub in range(q_ref.shape[1] // tq):
        i = pl.program_id(2) * (q_ref.shape[1] // tq) + sub
        _fill_queries(qaug.at[sub], q_ref, sub * tq, tq,
                      [jnp.broadcast_to(qf_ref[0, p:p + 1, :], (tq, LANES)) for p in range(PAIRS)])
        kc = kaug[pl.ds(i, n_chunks)].reshape(n_chunks * 2 * ch, 2 * LANES)
        st = _dot_nt(kc, qaug[sub])

        t = i * tq + qcol
        ev, od = [], []
        for u in range(n_chunks):
            spos = (i - n_pad + u) * ch + krow
            dist = t - spos
            vis = jnp.where(dist >= 0, jnp.where(dist < window, jnp.where(spos >= 0, 1.0, 0.0), 0.0), 0.0)
            bias = _tile_pairs(jnp.where(vis > 0.5, 0.0, NEG))
            ev.append(st[(2 * u) * ch:(2 * u + 1) * ch] + bias)
            od.append(st[(2 * u + 1) * ch:(2 * u + 2) * ch] + bias)

        tpos = (i * tq + lax.broadcasted_iota(jnp.int32, (1, tq), 1)).astype(F32)
        vmat = jnp.concatenate([vt[i + u] for u in range(n_chunks)], axis=1)
        accs = []
        for h, slabs in enumerate((ev, od)):
            m = slabs[0].max(axis=0, keepdims=True)
            for x in slabs[1:]:
                m = jnp.maximum(m, x.max(axis=0, keepdims=True))
            if has_sink:
                sk = jnp.concatenate(
                    [sink_ref[layer, g * HEADS_PER_GROUP + 2 * p + h] * LOG2E
                     + slope_ref[g * HEADS_PER_GROUP + 2 * p + h] * tpos for p in range(PAIRS)], axis=1)
                m = jnp.maximum(m, sk)
            pmat = jnp.concatenate([jnp.exp2(x - m).astype(vt.dtype) for x in slabs], axis=0)
            acc = _dot(vmat, pmat)
            if has_sink:
                acc = acc + jnp.where(vrow >= HEAD_DIM, jnp.exp2(sk - m), 0.0)
            accs.append(acc)
        _finish(o_ref, sub * tq, tq, accs[0], accs[1], gate_ref, gate_col)


def _banded_attention(q, kv, k_col, v_col, qfeat, kfeat, window, sinks=None, slopes=None,
                      gates=None, gate_col=None, layer=0):
    b, s, _ = q.shape
    tq = Q_BLOCK
    n_sub = tq // Q_SUB
    n_chunks = window // BAND_CHUNK + 1
    n_tot = kfeat.shape[0]
    has_sink = sinks is not None
    kern = functools.partial(_banded_kernel, n_chunks=n_chunks, window=window,
                             has_sink=has_sink, gate_col=gate_col, layer=layer)
    in_specs, args = [], []
    if has_sink:
        in_specs += [pl.BlockSpec(memory_space=pltpu.SMEM)] * 2
        args += [sinks, slopes]
    in_specs += [pl.BlockSpec((1, tq, PAIRS * LANES), lambda bi, g, i: (bi, i, g)),
                 pl.BlockSpec((1, s, LANES), lambda bi, g, i: (bi, 0, k_col)),
                 pl.BlockSpec((1, s, LANES), lambda bi, g, i: (bi, 0, v_col)),
                 pl.BlockSpec((1, PAIRS, LANES), lambda bi, g, i: (g, 0, 0)),
                 _const_spec(kfeat.shape)]
    args += [q, kv, kv, qfeat, kfeat]
    if gate_col is not None:
        in_specs.append(pl.BlockSpec((1, tq, LANES), lambda bi, g, i: (bi, i, g)))
        args.append(gates)
    return pl.pallas_call(
        kern,
        grid=(b, N_KV_GROUPS, s // tq),
        in_specs=in_specs,
        out_specs=pl.BlockSpec((1, tq, PAIRS * LANES), lambda bi, g, i: (bi, i, g)),
        out_shape=jax.ShapeDtypeStruct(q.shape, MXU_DT),
        scratch_shapes=[pltpu.VMEM((n_tot, 2, BAND_CHUNK, 2 * LANES), MXU_DT),
                        pltpu.VMEM((n_tot, V_ROWS, BAND_CHUNK), MXU_DT),
                        pltpu.VMEM((n_sub, PAIRS * Q_SUB, 2 * LANES), MXU_DT)],
        compiler_params=_params(("arbitrary", "arbitrary", "arbitrary")),
        name="banded_attention_w%d" % window,
    )(*args)


def _cmpsel_kernel(q_ref, k_ref, v_ref, qf_ref, kf_ref, ovt_ref, gate_ref, o_ref, mf_ref,
                   kaug, vt, qaug):
    g = pl.program_id(1)
    ncp = k_ref.shape[2]

    @pl.when(pl.program_id(2) == 0)
    def _():
        _build_keys(g, lambda s0: k_ref[0, 0], lambda s0: v_ref[0, 0], 1, kf_ref, kaug, vt, 0, ncp)

    n_sub = q_ref.shape[1] // Q_SUB
    for sub in range(n_sub):
        _cmpsel_block(pl.program_id(2) * n_sub + sub, sub, ncp, q_ref, qf_ref, ovt_ref, gate_ref,
                      o_ref, mf_ref, kaug, vt, qaug)


def _cmpsel_block(i, sub, ncp, q_ref, qf_ref, ovt_ref, gate_ref, o_ref, mf_ref, kaug, vt, qaug):
    tq = Q_SUB
    n_slc = NSA_SLC_LEN
    _fill_queries(qaug.at[sub], q_ref, sub * tq, tq,
                  [jnp.broadcast_to(qf_ref[0, p:p + 1, :], (tq, LANES)) for p in range(PAIRS)])
    st = _dot_nt(kaug[0].reshape(2 * ncp, 2 * LANES), qaug[sub])

    krow = lax.broadcasted_iota(jnp.int32, (ncp, tq), 0)
    qcol = lax.broadcasted_iota(jnp.int32, (ncp, tq), 1)
    t = i * tq + qcol
    cmp_end = krow * NSA_CMP_STRIDE + (NSA_CMP_LEN - 1)
    vis = jnp.where(t >= cmp_end, jnp.where(krow < ncp - 1, 1.0, 0.0), 0.0)
    bias = _tile_pairs(jnp.where(vis > 0.5, 0.0, NEG))

    pn = []
    for h in range(2):
        x = st[h * ncp:(h + 1) * ncp] + bias
        m = x.max(axis=0, keepdims=True)
        m = jnp.where(m > 0.5 * NEG, m, 0.0)
        e = jnp.exp2(x - m)
        d = e.sum(axis=0, keepdims=True)
        pn.append(e * (1.0 / jnp.where(d > 0.0, d, 1.0)))
    _finish(o_ref, sub * tq, tq, _dot(vt[0], pn[0].astype(vt.dtype)), _dot(vt[0], pn[1].astype(vt.dtype)),
            gate_ref, 0, normalized=True)

    ps = pn[0] + pn[1]
    psum = ps[:, 0:tq]
    for p in range(1, PAIRS):
        psum = psum + ps[:, p * tq:(p + 1) * tq]
    p_hi = psum.astype(MXU_DT)
    p_lo = (psum - p_hi.astype(F32)).astype(MXU_DT)
    imp = (_dot(ovt_ref[...], p_hi) + _dot(ovt_ref[...], p_lo))[0:n_slc]
    blk = lax.broadcasted_iota(jnp.int32, (n_slc, tq), 0)
    tq_pos = i * tq + lax.broadcasted_iota(jnp.int32, (n_slc, tq), 1)
    cur = lax.shift_right_logical(tq_pos, int(math.log2(NSA_SLC_LEN)))
    forced = jnp.where(blk == 0, 1.0, jnp.where(blk == cur, 1.0, jnp.where(blk == cur - 1, 1.0, 0.0)))
    valid = blk <= cur
    score = jnp.where(valid, jnp.where(forced > 0.5, jnp.inf, imp), -jnp.inf)
    groups = [score[8 * v:8 * v + 8] for v in range(n_slc // 8)]
    ranks = [jnp.zeros((8, tq), F32) for _ in groups]
    sublane = lax.broadcasted_iota(jnp.int32, (8, tq), 0)
    for j in range(n_slc):
        rj = score[j:j + 1, :]
        for v in range(n_slc // 8):
            if 8 * v > j:
                beats = jnp.where(rj >= groups[v], 1.0, 0.0)
            elif 8 * v + 7 < j:
                beats = jnp.where(rj > groups[v], 1.0, 0.0)
            else:
                beats = jnp.where(sublane > (j - 8 * v), jnp.where(rj >= groups[v], 1.0, 0.0),
                                  jnp.where(rj > groups[v], 1.0, 0.0))
            ranks[v] = ranks[v] + beats
    rank = jnp.concatenate(ranks, axis=0)
    keep = jnp.where(valid, jnp.where(rank < float(NSA_TOPK), 1.0, 0.0), 0.0)
    feat = jnp.where(keep > 0.5, 0.0, NEG)
    feat = jnp.concatenate([feat, jnp.zeros((LANES - n_slc, tq), F32)], axis=0)
    mf_ref[0, 0, sub * tq:(sub + 1) * tq, :] = feat.T.astype(mf_ref.dtype)


def _compressed_and_select(q, kvcmp, qfeat, kfeat, ovt, gates):
    b, s, _ = q.shape
    tq = Q_BLOCK
    ncp = kvcmp.shape[2]
    return pl.pallas_call(
        _cmpsel_kernel,
        grid=(b, N_KV_GROUPS, s // tq),
        in_specs=[pl.BlockSpec((1, tq, PAIRS * LANES), lambda bi, g, i: (bi, i, g)),
                  pl.BlockSpec((1, 1, ncp, LANES), lambda bi, g, i: (bi, 0, 0, 0)),
                  pl.BlockSpec((1, 1, ncp, LANES), lambda bi, g, i: (bi, 1, 0, 0)),
                  pl.BlockSpec((1, PAIRS, LANES), lambda bi, g, i: (g, 0, 0)),
                  _const_spec(kfeat.shape), _const_spec(ovt.shape),
                  pl.BlockSpec((1, tq, LANES), lambda bi, g, i: (bi, i, g))],
        out_specs=[pl.BlockSpec((1, tq, PAIRS * LANES), lambda bi, g, i: (bi, i, g)),
                   pl.BlockSpec((1, 1, tq, LANES), lambda bi, g, i: (bi, g, i, 0))],
        out_shape=[jax.ShapeDtypeStruct(q.shape, MXU_DT),
                   jax.ShapeDtypeStruct((b, N_KV_GROUPS, s, LANES), MXU_DT)],
        scratch_shapes=[pltpu.VMEM((1, 2, ncp, 2 * LANES), MXU_DT),
                        pltpu.VMEM((1, V_ROWS, ncp), MXU_DT),
                        pltpu.VMEM((tq // Q_SUB, PAIRS * Q_SUB, 2 * LANES), MXU_DT)],
        compiler_params=_params(("arbitrary", "arbitrary", "arbitrary")),
        name="nsa_compressed_select",
    )(q, kvcmp, kvcmp, qfeat, kfeat, ovt, gates)


def _selected_kernel(q_ref, k_ref, v_ref, qf_ref, mf_ref, kf_ref, gate_ref, o_ref,
                     kaug, vt, qaug, acc_e, acc_o, st_a, st_b):
    g = pl.program_id(1)
    i = pl.program_id(2)
    tq = q_ref.shape[1]
    ch = SLC_CHUNK
    rows = PAIRS * tq

    @pl.when(i == 0)
    def _():
        _build_keys(g, lambda s0: k_ref[0, pl.ds(s0, ch), :].astype(F32),
                    lambda s0: v_ref[0, pl.ds(s0, ch), :].astype(F32),
                    k_ref.shape[1] // ch, kf_ref, kaug, vt, 0, ch)

    mask_feat = mf_ref[0, 0].astype(F32)
    _fill_queries(qaug, q_ref, 0, tq, [mask_feat + qf_ref[0, p:p + 1, :] for p in range(PAIRS)])
    acc_e[...] = jnp.zeros((V_ROWS, rows), F32)
    acc_o[...] = jnp.zeros((V_ROWS, rows), F32)

    def update(x, m_old, acc_ref, vt_c):
        m_new = jnp.maximum(m_old, x.max(axis=0, keepdims=True))
        alpha = jnp.exp2(m_old - m_new)
        p = jnp.exp2(x - m_new).astype(vt_c.dtype)
        acc_ref[...] = alpha * acc_ref[...] + _dot(vt_c, p)
        return m_new

    def scores_into(buf, c):
        buf[...] = _dot_nt(kaug[c].reshape(2 * ch, 2 * LANES), qaug[...])

    def consume(buf, c, ms, causal):
        xe, xo = buf[0:ch, :], buf[ch:2 * ch, :]
        if causal:
            krow = lax.broadcasted_iota(jnp.int32, (ch, tq), 0)
            qcol = lax.broadcasted_iota(jnp.int32, (ch, tq), 1)
            bias = _tile_pairs(jnp.where(c * ch + krow <= i * tq + qcol, 0.0, NEG))
            xe, xo = xe + bias, xo + bias
        return update(xe, ms[0], acc_e, vt[c]), update(xo, ms[1], acc_o, vt[c])

    def pair(k, ms):
        c = 2 * k
        scores_into(st_b, c + 1)
        ms = consume(st_a, c, ms, False)
        scores_into(st_a, c + 2)
        return consume(st_b, c + 1, ms, False)

    assert tq == ch
    m0 = jnp.full((1, rows), M_INIT, F32)
    scores_into(st_a, 0)
    ms = lax.fori_loop(0, i // 2, pair, (m0, m0))

    @pl.when(i % 2 == 0)
    def _():
        consume(st_a, i, ms, True)

    @pl.when(i % 2 == 1)
    def _():
        scores_into(st_b, i)
        consume(st_b, i, consume(st_a, i - 1, ms, False), True)

    _finish(o_ref, 0, tq, acc_e[...], acc_o[...], gate_ref, 1)


def _selected_attention(q, kv, k_col, v_col, qfeat, maskfeat, kfeat, gates):
    b, s, _ = q.shape
    tq = Q_BLOCK
    n_tot = kfeat.shape[0]
    return pl.pallas_call(
        _selected_kernel,
        grid=(b, N_KV_GROUPS, s // tq),
        in_specs=[pl.BlockSpec((1, tq, PAIRS * LANES), lambda bi, g, i: (bi, i, g)),
                  pl.BlockSpec((1, s, LANES), lambda bi, g, i: (bi, 0, k_col)),
                  pl.BlockSpec((1, s, LANES), lambda bi, g, i: (bi, 0, v_col)),
                  pl.BlockSpec((1, PAIRS, LANES), lambda bi, g, i: (g, 0, 0)),
                  pl.BlockSpec((1, 1, tq, LANES), lambda bi, g, i: (bi, g, i, 0)),
                  _const_spec(kfeat.shape),
                  pl.BlockSpec((1, tq, LANES), lambda bi, g, i: (bi, i, g))],
        out_specs=pl.BlockSpec((1, tq, PAIRS * LANES), lambda bi, g, i: (bi, i, g)),
        out_shape=jax.ShapeDtypeStruct(q.shape, MXU_DT),
        scratch_shapes=[pltpu.VMEM((n_tot, 2, SLC_CHUNK, 2 * LANES), MXU_DT),
                        pltpu.VMEM((n_tot, V_ROWS, SLC_CHUNK), MXU_DT),
                        pltpu.VMEM((PAIRS * tq, 2 * LANES), MXU_DT),
                        pltpu.VMEM((V_ROWS, PAIRS * tq), F32),
                        pltpu.VMEM((V_ROWS, PAIRS * tq), F32),
                        pltpu.VMEM((2 * SLC_CHUNK, PAIRS * tq), F32),
                        pltpu.VMEM((2 * SLC_CHUNK, PAIRS * tq), F32)],
        compiler_params=_params(("arbitrary", "arbitrary", "arbitrary")),
        name="nsa_selected",
    )(q, kv, kv, qfeat, maskfeat, kfeat, gates)


def _outproj_kernel(oa_ref, oc_ref, os_ref, ow_ref, x_ref, w_ref, b_ref, g_ref, beta_ref,
                    wr_ref, br_ref, x1_ref, route_ref, *, alpha):
    half = oa_ref.shape[1]
    on = (oc_ref[...].astype(F32) + os_ref[...].astype(F32) + ow_ref[...].astype(F32)).astype(MXU_DT)
    mix = _dot(oa_ref[...], w_ref[0:half]) + _dot(on, w_ref[half:]) + b_ref[...]
    x1 = _layer_norm(alpha * x_ref[...] + mix, g_ref[...], beta_ref[...])
    x1_ref[...] = x1

    x_hi = x1.astype(MXU_DT)
    x_lo = (x1 - x_hi.astype(F32)).astype(MXU_DT)
    logits = (_dot(x_hi, wr_ref[0]) + _dot(x_hi, wr_ref[1]) + _dot(x_lo, wr_ref[0])) + br_ref[...]
    lane = lax.broadcasted_iota(jnp.int32, logits.shape, 1)
    lanef = lane.astype(F32)
    big = float(4 * LANES)
    gl = jnp.where(lane < N_GROUPS, logits, -jnp.inf)
    gmax = gl.max(axis=1, keepdims=True)
    gsel = jnp.where(gl == gmax, lanef, big).min(axis=1, keepdims=True)
    g_w = 1.0 / jnp.exp(gl - gmax).sum(axis=1, keepdims=True)
    lo = N_GROUPS + gsel * EXPERTS_PER_GROUP
    el = jnp.where(lanef >= lo, jnp.where(lanef < lo + EXPERTS_PER_GROUP, logits, -jnp.inf), -jnp.inf)
    v1 = el.max(axis=1, keepdims=True)
    i1 = jnp.where(el == v1, lanef, big).min(axis=1, keepdims=True)
    el2 = jnp.where(lanef == i1, -jnp.inf, el)
    v2 = el2.max(axis=1, keepdims=True)
    i2 = jnp.where(el2 == v2, lanef, big).min(axis=1, keepdims=True)
    e2 = jnp.exp(v2 - v1)
    w1 = g_w / (1.0 + e2)
    w2 = g_w * e2 / (1.0 + e2)
    route_ref[...] = jnp.where(lane == 0, i1 - N_GROUPS,
                               jnp.where(lane == 1, i2 - N_GROUPS,
                                         jnp.where(lane == 2, w1, jnp.where(lane == 3, w2, 0.0))))


def _output_projection(oa, oc, os_, ow, x2d, w, b, ln_g, ln_b, wr, br, alpha, layer):
    n, d = x2d.shape
    half = oa.shape[1]
    tm = min(PROJ_TILE, n)
    row = lambda i: (i, 0)
    return pl.pallas_call(
        functools.partial(_outproj_kernel, alpha=alpha),
        grid=(n // tm,),
        in_specs=[pl.BlockSpec((tm, half), row)] * 4 + [pl.BlockSpec((tm, d), row)]
        + [_layer_spec(a, layer) for a in (w, b, ln_g, ln_b, wr, br)],
        out_specs=[pl.BlockSpec((tm, d), row), pl.BlockSpec((tm, LANES), row)],
        out_shape=[jax.ShapeDtypeStruct((n, d), F32), jax.ShapeDtypeStruct((n, LANES), F32)],
        compiler_params=_params(("arbitrary",)),
        name="output_projection_ln_route",
    )(oa, oc, os_, ow, x2d, w, b, ln_g, ln_b, wr, br)


def _row_copy(src_hbm, src_row, dst, j, u, sem):
    return pltpu.make_async_copy(src_hbm.at[pl.ds(src_row, 1)], dst.at[j, pl.ds(u, 1)], sem)


def _rows_start(src_hbm, idx_ref, base, dst, sem, n_groups):
    def body(j, carry):
        for u in range(SUBLANES):
            _row_copy(src_hbm, idx_ref[base + j * SUBLANES + u], dst, j, u, sem).start(priority=u % 2)
        return carry
    lax.fori_loop(0, n_groups, body, 0)


def _rows_wait(src_hbm, dst, sem, n_groups):
    def body(j, carry):
        for u in range(SUBLANES):
            _row_copy(src_hbm, 0, dst, j, u, sem).wait()
        return carry
    lax.fori_loop(0, n_groups, body, 0)


def _expert_kernel(blk_e_ref, n_used_ref, first_ref, wslot_ref, next_e_ref, row_tok_ref,
                   x_hbm, wg_hbm, wu_hbm, wd_hbm, o_ref,
                   xbuf, xsem, wg_f, wu_f, wd_f, wsem, wg_b, wu_b, wd_b, *, layer):
    i = pl.program_id(0)
    n_used = n_used_ref[0]
    groups, d = xbuf.shape[1], xbuf.shape[3]
    t = groups * SUBLANES
    hid = wg_b.shape[1]
    used = i < n_used
    slot = lax.rem(i, 2)

    def weight_copies(e, s):
        return [pltpu.make_async_copy(w.at[layer, e], buf.at[s], wsem.at[s])
                for w, buf in ((wg_hbm, wg_f), (wu_hbm, wu_f), (wd_hbm, wd_f))]

    def rows_start(blk, s, group_range):
        for j in group_range:
            for u in range(SUBLANES):
                _row_copy(x_hbm, row_tok_ref[blk * t + j * SUBLANES + u], xbuf.at[s], j, u,
                          xsem.at[s]).start()

    def rows_wait(s):
        for j in range(groups):
            for u in range(SUBLANES):
                _row_copy(x_hbm, 0, xbuf.at[s], j, u, xsem.at[s]).wait()

    @pl.when(i == 0)
    def _():
        for c in weight_copies(blk_e_ref[0], 0):
            c.start(priority=1)
        rows_start(0, 0, range(groups))

    @pl.when(used)
    def _():
        rows_wait(slot)
        ws = wslot_ref[i]

        @pl.when(first_ref[i] == 1)
        def _():
            for c in weight_copies(0, ws):
                c.wait()

            @pl.when(next_e_ref[i] >= 0)
            def _():
                for c in weight_copies(next_e_ref[i], 1 - ws):
                    c.start(priority=1)

            wg_b[...] = wg_f[ws].astype(wg_b.dtype)
            wu_b[...] = wu_f[ws].astype(wu_b.dtype)
            wd_b[...] = wd_f[ws].astype(wd_b.dtype)

        nxt = jnp.minimum(i + 1, n_used - 1)
        xb = xbuf[slot].reshape(t, d).astype(MXU_DT)
        pieces = 2
        y = None
        for piece in range(pieces):
            cols = slice(piece * hid // pieces, (piece + 1) * hid // pieces)
            gate = _dot(xb, wg_b[:, cols])
            up = _dot(xb, wu_b[:, cols])
            rows_start(nxt, 1 - slot, range(piece * groups // pieces, (piece + 1) * groups // pieces))
            hidden = (gate * jax.nn.sigmoid(gate) * up).astype(MXU_DT)
            part = _dot(hidden, wd_b[cols, :])
            y = part if y is None else y + part
        o_ref[...] = y

        @pl.when(i == n_used - 1)
        def _():
            rows_wait(1 - slot)

    @pl.when(jnp.logical_not(used))
    def _():
        o_ref[...] = jnp.zeros(o_ref.shape, o_ref.dtype)


def _expert_ffn(x1, row_tok, blk_e, n_used, first, wslot, next_e, wg, wu, wd, layer):
    d = x1.shape[1]
    p = row_tok.shape[0]
    t = EXPERT_ROWS
    hid = wg.shape[3]
    any_spec = pl.BlockSpec(memory_space=pl.ANY)
    return pl.pallas_call(
        functools.partial(_expert_kernel, layer=layer),
        grid_spec=pltpu.PrefetchScalarGridSpec(
            num_scalar_prefetch=6,
            grid=(p // t,),
            in_specs=[any_spec, any_spec, any_spec, any_spec],
            out_specs=pl.BlockSpec((t, d), lambda i, *_: (i, 0)),
            scratch_shapes=[pltpu.VMEM((2, t // SUBLANES, SUBLANES, d), F32), pltpu.SemaphoreType.DMA((2,)),
                            pltpu.VMEM((2, d, hid), F32), pltpu.VMEM((2, d, hid), F32),
                            pltpu.VMEM((2, hid, d), F32), pltpu.SemaphoreType.DMA((2,)),
                            pltpu.VMEM((d, hid), MXU_DT), pltpu.VMEM((d, hid), MXU_DT),
                            pltpu.VMEM((hid, d), MXU_DT)]),
        out_shape=jax.ShapeDtypeStruct((p, d), F32),
        compiler_params=_params(("arbitrary",)),
        name="expert_ffn",
    )(blk_e, n_used, first, wslot, next_e, row_tok, x1, wg, wu, wd)


def _combine_kernel(dest_ref, ys_hbm, route_ref, x_ref, g_ref, b_ref, o_ref, ybuf, sem, *, alpha, n_tok):
    i = pl.program_id(0)
    tm, d = x_ref.shape
    groups = tm // SUBLANES
    slot = lax.rem(i, 2)

    def start(blk, s):
        for k in range(2):
            _rows_start(ys_hbm, dest_ref, k * n_tok + blk * tm, ybuf.at[s, k], sem.at[s], groups)

    @pl.when(i == 0)
    def _():
        start(0, 0)

    @pl.when(i + 1 < pl.num_programs(0))
    def _():
        start(i + 1, 1 - slot)

    for k in range(2):
        _rows_wait(ys_hbm, ybuf.at[slot, k], sem.at[slot], groups)
    r = route_ref[...]
    y = r[:, 2:3] * ybuf[slot, 0].reshape(tm, d) + r[:, 3:4] * ybuf[slot, 1].reshape(tm, d)
    o_ref[...] = _layer_norm(alpha * x_ref[...] + y, g_ref[...], b_ref[...])


def _combine(ys, dest_kmajor, route, x1, ln_g, ln_b, alpha, layer):
    n, d = x1.shape
    tm = min(ROW_TILE, n)
    row = lambda i, dst: (i, 0)
    return pl.pallas_call(
        functools.partial(_combine_kernel, alpha=alpha, n_tok=n),
        grid_spec=pltpu.PrefetchScalarGridSpec(
            num_scalar_prefetch=1,
            grid=(n // tm,),
            in_specs=[pl.BlockSpec(memory_space=pl.ANY),
                      pl.BlockSpec((tm, LANES), row), pl.BlockSpec((tm, d), row),
                      _layer_spec(ln_g, layer), _layer_spec(ln_b, layer)],
            out_specs=pl.BlockSpec((tm, d), row),
            scratch_shapes=[pltpu.VMEM((2, 2, tm // SUBLANES, SUBLANES, d), F32),
                            pltpu.SemaphoreType.DMA((2,))]),
        out_shape=jax.ShapeDtypeStruct((n, d), F32),
        compiler_params=_params(("arbitrary",)),
        name="combine_ln",
    )(dest_kmajor, ys, route, x1, ln_g, ln_b)


def _dispatch_plan(route, n_tokens):
    t = EXPERT_ROWS
    a = 2 * n_tokens
    eid = route[:, 0:2].astype(jnp.int32).reshape(a)
    onehot = (eid[:, None] == jnp.arange(N_EXPERTS, dtype=jnp.int32)[None, :]).astype(jnp.int32)
    counts = onehot.sum(axis=0)
    before = jnp.cumsum(onehot, axis=0) - onehot
    rank = jnp.take_along_axis(before, eid[:, None], axis=1)[:, 0]
    padded = (counts + t - 1) // t * t
    pend = jnp.cumsum(padded)
    pstart = pend - padded
    dest = (pstart[eid] + rank).astype(jnp.int32)
    p_rows = a + N_EXPERTS * t
    filler = jnp.arange(p_rows, dtype=jnp.int32) % n_tokens
    row_tok = filler.at[dest].set(jnp.arange(a, dtype=jnp.int32) // 2)
    blk_start = jnp.arange(p_rows // t, dtype=jnp.int32) * t
    blk_e = jnp.minimum((pend[None, :] <= blk_start[:, None]).astype(jnp.int32).sum(axis=1), N_EXPERTS - 1)
    blk_e = blk_e.astype(jnp.int32)
    n_used = (pend[-1] // t).astype(jnp.int32)
    blk = jnp.arange(p_rows // t, dtype=jnp.int32)
    first = jnp.concatenate([jnp.ones((1,), jnp.int32), (blk_e[1:] != blk_e[:-1]).astype(jnp.int32)])
    wslot = (jnp.cumsum(first) - 1) % 2
    is_used = blk < n_used
    n_le = ((blk_e[None, :] <= blk_e[:, None]) & is_used[None, :]).astype(jnp.int32).sum(axis=1)
    next_e = jnp.where(n_le < n_used, blk_e[jnp.minimum(n_le, p_rows // t - 1)], -1)
    dest_kmajor = dest.reshape(n_tokens, 2).T.reshape(a)
    return (dest_kmajor, row_tok, blk_e, n_used.reshape(1), first, wslot.astype(jnp.int32),
            next_e.astype(jnp.int32))


def kernel(x, w_in, b_in, swa_sinks, cmp_pe_k, cmp_w1_k, cmp_w2_k, cmp_pe_v, cmp_w1_v, cmp_w2_v,
           w_out, b_out, ln1_g, ln1_b, w_group, b_group, w_expert, b_expert, we_gate, we_up,
           we_down, ln2_g, ln2_b):
    bsz, seq, d_model = x.shape
    depth = w_in.shape[0]
    n_tok = bsz * seq
    n_heads = d_model // HEAD_DIM
    q_w = n_heads // 2 * HEAD_DIM
    kv_w = N_KV_GROUPS * HEAD_DIM
    alpha = (2.0 * depth) ** 0.25
    scale = HEAD_DIM ** -0.5 * LOG2E
    assert q_w == N_KV_GROUPS * PAIRS * LANES and seq % SLC_CHUNK == 0

    slopes_a, slopes_n = _alibi_slopes(n_heads)
    slopes_a = (slopes_a * np.float32(LOG2E)).astype(np.float32)
    slopes_n = (slopes_n * np.float32(LOG2E)).astype(np.float32)
    qfeat_a = jnp.asarray(_query_feats(slopes_a), F32)
    qfeat_n = jnp.asarray(_query_feats(slopes_n), F32)
    pos = np.arange(seq)
    kfeat_swa = jnp.asarray(_key_feats(pos, SWA_WINDOW // BAND_CHUNK, BAND_CHUNK, False), MXU_DT)
    kfeat_win = jnp.asarray(_key_feats(pos, NSA_WINDOW // BAND_CHUNK, BAND_CHUNK, False), MXU_DT)
    kfeat_slc = jnp.asarray(_key_feats(pos, 0, SLC_CHUNK, True), MXU_DT)
    n_cmp = (seq - NSA_CMP_LEN) // NSA_CMP_STRIDE + 1
    n_cmp_pad = seq // NSA_CMP_STRIDE
    cmp_end = np.arange(n_cmp_pad) * NSA_CMP_STRIDE + NSA_CMP_LEN - 1
    kfeat_cmp = jnp.asarray(_key_feats(cmp_end, 0, n_cmp_pad, False), MXU_DT)
    n_slc = seq // NSA_SLC_LEN
    cs = np.arange(n_cmp_pad)[None, :] * NSA_CMP_STRIDE
    ss = np.arange(LANES)[:, None] * NSA_SLC_LEN
    ovt = ((cs < ss + NSA_SLC_LEN) & (cs + NSA_CMP_LEN - 1 >= ss)
           & (np.arange(n_cmp_pad)[None, :] < n_cmp) & (np.arange(LANES)[:, None] < n_slc))
    ovt = jnp.asarray(ovt.astype(np.float32), MXU_DT)
    slopes_a_s = jnp.asarray(slopes_a, F32)

    o_qa, o_ka = 0, q_w
    o_qn = q_w + 2 * kv_w
    o_kvn = o_qn + q_w
    o_gn = o_kvn + 6 * kv_w
    gate_w = 3 * HEADS_PER_GROUP

    def seg(a, lo, hi):
        return a[..., lo:hi]

    def gate_cols(a):
        pad = [(0, 0)] * (a.ndim - 1) + [(0, LANES - gate_w)]
        return jnp.concatenate([jnp.pad(seg(a, o_gn + g * gate_w, o_gn + (g + 1) * gate_w), pad)
                                for g in range(N_KV_GROUPS)], axis=-1)

    def regroup(a):
        return jnp.concatenate([seg(a, o_qa, o_qa + q_w) * scale, seg(a, o_qn, o_qn + q_w) * scale,
                                seg(a, o_ka, o_ka + 2 * kv_w), seg(a, o_kvn, o_kvn + 6 * kv_w),
                                gate_cols(a)], axis=-1)

    w_r = regroup(w_in).astype(MXU_DT)
    b_r = regroup(b_in)[:, None, :]
    flat = NSA_CMP_LEN * HEAD_DIM
    pe = jnp.broadcast_to(jnp.stack([cmp_pe_k, cmp_pe_v], axis=1).reshape(depth, 2, 1, flat),
                          (depth, 2, 8, flat))
    w1 = jnp.stack([cmp_w1_k, cmp_w1_v], axis=1).astype(MXU_DT)
    w2 = jnp.stack([cmp_w2_k, cmp_w2_v], axis=1).astype(MXU_DT)
    n_route_pad = LANES - N_GROUPS - N_EXPERTS
    wr = jnp.concatenate([w_group, w_expert, jnp.zeros((depth, d_model, n_route_pad), F32)], axis=2)
    br = jnp.concatenate([b_group, b_expert, jnp.zeros((depth, n_route_pad), F32)], axis=1)[:, None, :]
    wr_hi = wr.astype(MXU_DT)
    wr = jnp.stack([wr_hi, (wr - wr_hi.astype(F32)).astype(MXU_DT)], axis=1)
    w_o = w_out.astype(MXU_DT)
    b_o, g1, be1, g2, be2 = (a[:, None, :] for a in (b_out, ln1_g, ln1_b, ln2_g, ln2_b))

    h = x.reshape(n_tok, d_model)
    for l in range(depth):
        qa, qn, kv, gates = _input_projection(h, w_r, b_r, l)
        qa = qa.reshape(bsz, seq, q_w)
        qn = qn.reshape(bsz, seq, q_w)
        kv = kv.reshape(bsz, seq, 8 * kv_w)
        gates = gates.reshape(bsz, seq, 2 * LANES)

        o_a = _banded_attention(qa, kv, 0, 1, qfeat_a, kfeat_swa, SWA_WINDOW,
                                sinks=swa_sinks, slopes=slopes_a_s, layer=l)

        kvc = kv[:, :, 2 * kv_w:4 * kv_w].reshape(bsz, n_cmp_pad, NSA_CMP_STRIDE, 2, N_KV_GROUPS, HEAD_DIM)
        kvc = kvc.transpose(0, 3, 4, 1, 2, 5).reshape(bsz, 2, N_KV_GROUPS, n_cmp_pad, NSA_CMP_STRIDE * HEAD_DIM)
        kvcmp = _compress(kvc, pe, w1, w2, l)

        o_cmp, maskfeat = _compressed_and_select(qn, kvcmp, qfeat_n, kfeat_cmp, ovt, gates)
        o_slc = _selected_attention(qn, kv, 4, 5, qfeat_n, maskfeat, kfeat_slc, gates)
        o_win = _banded_attention(qn, kv, 6, 7, qfeat_n, kfeat_win, NSA_WINDOW,
                                  gates=gates, gate_col=2)

        x1, route = _output_projection(
            o_a.reshape(n_tok, q_w), o_cmp.reshape(n_tok, q_w), o_slc.reshape(n_tok, q_w),
            o_win.reshape(n_tok, q_w), h, w_o, b_o, g1, be1, wr, br, alpha, l)

        dest_kmajor, row_tok, blk_e, n_used, first, wslot, next_e = _dispatch_plan(route, n_tok)
        ys = _expert_ffn(x1, row_tok, blk_e, n_used, first, wslot, next_e, we_gate, we_up, we_down, l)
        h = _combine(ys, dest_kmajor, route, x1, g2, be2, alpha, l)
    return h.reshape(bsz, seq, d_model)
```

```python
import functools
import math

import numpy as np
import jax
import jax.numpy as jnp
from jax import lax
from jax.experimental import pallas as pl
from jax.experimental.pallas import tpu as pltpu

F32 = jnp.float32
MXU_DT = jnp.bfloat16

HEAD_DIM = 64
HEADS_PER_GROUP = 8
PAIRS = HEADS_PER_GROUP // 2
N_KV_GROUPS = 2
SWA_WINDOW = 128
NSA_WINDOW = 512
NSA_CMP_LEN = 32
NSA_CMP_STRIDE = 16
NSA_CMP_HIDDEN = 256
NSA_SLC_LEN = 64
NSA_TOPK = 16
N_GROUPS = 8
EXPERTS_PER_GROUP = 8
N_EXPERTS = N_GROUPS * EXPERTS_PER_GROUP
EXPERT_HIDDEN = 512
LN_EPS = 1e-5

LANES = 128
SUBLANES = 8
V_ROWS = 80
Q_BLOCK = 512
Q_SUB = 128
BAND_CHUNK = 128
SLC_CHUNK = 512
EXPERT_ROWS = 128
ROW_TILE = 256
PROJ_TILE = 512
LOG2E = 1.4426950408889634
NEG = -1e30
M_INIT = -5e29
VMEM_LIMIT = 56 * 1024 * 1024

FEAT_EVEN = 64
FEAT_ODD = 72
FEAT_PAD = 127


def _alibi_slopes(n_heads_total):
    n = n_heads_total
    s = np.exp2(-8.0 * np.arange(1, n + 1, dtype=np.float32) / np.float32(n)).astype(np.float32)
    return s[0::2], s[1::2]


def _bf16_parts(v):
    v = np.asarray(v, np.float32)
    a = v.astype(jnp.bfloat16).astype(np.float32)
    r = (v - a).astype(np.float32)
    b = r.astype(jnp.bfloat16).astype(np.float32)
    c = (r - b).astype(np.float32)
    return a, b, c


def _query_feats(slopes):
    out = np.zeros((N_KV_GROUPS, PAIRS, LANES), np.float32)
    a, b, c = _bf16_parts(slopes)
    for g in range(N_KV_GROUPS):
        for p in range(PAIRS):
            for h, base in ((0, FEAT_EVEN), (1, FEAT_ODD)):
                idx = g * HEADS_PER_GROUP + 2 * p + h
                out[g, p, base:base + 6] = [a[idx], b[idx], c[idx], a[idx], b[idx], c[idx]]
            out[g, p, FEAT_PAD] = 1.0
    return out


def _key_feats(positions, n_pad_chunks, chunk, with_blocks):
    pos = np.asarray(positions, np.int64)
    n = pos.shape[0]
    assert n % chunk == 0
    hi = (pos // 64 * 64).astype(np.float32)
    lo = (pos % 64).astype(np.float32)
    f = np.zeros((n, 2, LANES), np.float32)
    for v, base in ((0, FEAT_EVEN), (1, FEAT_ODD)):
        f[:, v, base:base + 3] = hi[:, None]
        f[:, v, base + 3:base + 6] = lo[:, None]
    if with_blocks:
        blk = pos // NSA_SLC_LEN
        for v in range(2):
            f[np.arange(n), v, blk] = 1.0
    f = f.reshape(n // chunk, chunk, 2, LANES).transpose(0, 2, 1, 3)
    if n_pad_chunks:
        pad = np.zeros((n_pad_chunks,) + f.shape[1:], np.float32)
        pad[..., FEAT_PAD] = NEG
        f = np.concatenate([pad, f], axis=0)
    return f


def _params(sem, vmem=VMEM_LIMIT):
    return pltpu.CompilerParams(dimension_semantics=sem, vmem_limit_bytes=vmem)


def _const_spec(shape):
    nd = len(shape)
    return pl.BlockSpec(shape, lambda *_: (0,) * nd, pipeline_mode=pl.Buffered(1))


def _layer_spec(arr, layer):
    nd = arr.ndim
    return pl.BlockSpec((None,) + arr.shape[1:], lambda *_: (layer,) + (0,) * (nd - 1),
                        pipeline_mode=pl.Buffered(1))


def _dot(a, b):
    return jnp.dot(a, b, preferred_element_type=F32)


def _dot_nt(a, b):
    return lax.dot_general(a, b, (((1,), (1,)), ((), ())), preferred_element_type=F32)


def _layer_norm(h, g, b):
    mu = jnp.mean(h, axis=-1, keepdims=True)
    d = h - mu
    var = jnp.mean(d * d, axis=-1, keepdims=True)
    return d * lax.rsqrt(var + LN_EPS) * g + b


def _inproj_kernel(x_ref, w_ref, b_ref, qa_ref, qn_ref, kv_ref, gt_ref):
    xb = x_ref[...].astype(MXU_DT)
    width = qa_ref.shape[1]
    for j, o_ref in enumerate((qa_ref, qn_ref, kv_ref)):
        acc = _dot(xb, w_ref[:, j * width:(j + 1) * width]) + b_ref[:, j * width:(j + 1) * width]
        o_ref[...] = acc.astype(o_ref.dtype)
    gt_ref[...] = _dot(xb, w_ref[:, 3 * width:]) + b_ref[:, 3 * width:]


def _input_projection(x2d, w, b, layer):
    n, d = x2d.shape
    width = (w.shape[2] - 2 * LANES) // 3
    tm = min(PROJ_TILE, n)
    row = lambda i: (i, 0)
    return pl.pallas_call(
        _inproj_kernel,
        grid=(n // tm,),
        in_specs=[pl.BlockSpec((tm, d), row), _layer_spec(w, layer), _layer_spec(b, layer)],
        out_specs=[pl.BlockSpec((tm, width), row)] * 3 + [pl.BlockSpec((tm, 2 * LANES), row)],
        out_shape=[jax.ShapeDtypeStruct((n, width), MXU_DT)] * 3
        + [jax.ShapeDtypeStruct((n, 2 * LANES), F32)],
        compiler_params=_params(("arbitrary",)),
        name="input_projection",
    )(x2d, w, b)


def _gelu_tanh(x):
    c = math.sqrt(2.0 / math.pi)
    return 0.5 * x * (1.0 + jnp.tanh(c * (x + 0.044715 * (x * x * x))))


def _compress_kernel(c_ref, pe_ref, w1_ref, w2_ref, o_ref):
    half = c_ref.shape[-1]
    w1 = w1_ref[0]
    pe = pe_ref[0]
    pe_hi = pe.astype(MXU_DT)
    pe_lo = (pe - pe_hi.astype(F32)).astype(MXU_DT)
    base = (_dot(pe_hi, w1) + _dot(pe_lo, w1))[0:1]
    for g in range(N_KV_GROUPS):
        c = c_ref[0, 0, g]
        top = _dot(c, w1[:half])
        bot = _dot(c, w1[half:])
        n_rows = bot.shape[0]
        hidden = top + pltpu.roll(bot, n_rows - 1, 0) + base
        act = _gelu_tanh(hidden)
        o_ref[0, 0, :, g * HEAD_DIM:(g + 1) * HEAD_DIM] = _dot(act.astype(MXU_DT), w2_ref[0])


def _compress(chunks, pe, w1, w2, layer):
    b, two, g, nch, half = chunks.shape
    return pl.pallas_call(
        _compress_kernel,
        grid=(b, two),
        in_specs=[pl.BlockSpec((1, 1, g, nch, half), lambda i, j: (i, j, 0, 0, 0)),
                  pl.BlockSpec((None, 1, 8, 2 * half), lambda i, j: (layer, j, 0, 0)),
                  pl.BlockSpec((None, 1, 2 * half, NSA_CMP_HIDDEN), lambda i, j: (layer, j, 0, 0)),
                  pl.BlockSpec((None, 1, NSA_CMP_HIDDEN, HEAD_DIM), lambda i, j: (layer, j, 0, 0))],
        out_specs=pl.BlockSpec((1, 1, nch, LANES), lambda i, j: (i, j, 0, 0)),
        out_shape=jax.ShapeDtypeStruct((b, two, nch, LANES), F32),
        compiler_params=_params(("arbitrary", "arbitrary")),
        name="nsa_compress",
    )(chunks, pe, w1, w2)


def _place_group(x, g_is_zero):
    lane = lax.broadcasted_iota(jnp.int32, x.shape, 1)
    rolled = pltpu.roll(x, HEAD_DIM, 1)
    lo_src, hi_src = (x, rolled) if g_is_zero else (rolled, x)
    lo = jnp.where(lane < HEAD_DIM, lo_src, 0.0)
    hi = jnp.where(lane >= HEAD_DIM, hi_src, 0.0)
    return lo, hi


def _build_keys(g, load_k, load_v, n_chunks, kf_ref, kaug, vt, n_pad, chunk):
    ones = jnp.ones((V_ROWS - HEAD_DIM, chunk), F32)
    for gval in range(N_KV_GROUPS):
        @pl.when(g == gval)
        def _():
            def body(c, carry):
                start = pl.multiple_of(c * chunk, chunk)
                klo, khi = _place_group(load_k(start), gval == 0)
                vlo, _ = _place_group(load_v(start), gval == 0)
                kaug[n_pad + c, 0, :, 0:LANES] = klo.astype(kaug.dtype)
                kaug[n_pad + c, 1, :, 0:LANES] = khi.astype(kaug.dtype)
                vt[n_pad + c] = jnp.concatenate([vlo.T[0:HEAD_DIM], ones], axis=0).astype(vt.dtype)
                return carry
            lax.fori_loop(0, n_chunks, body, 0)
    kaug[:, :, :, LANES:2 * LANES] = kf_ref[...]
    if n_pad:
        kaug[0:n_pad, :, :, 0:LANES] = jnp.zeros((n_pad, 2, chunk, LANES), kaug.dtype)
        vt[0:n_pad] = jnp.zeros((n_pad, V_ROWS, chunk), vt.dtype)


def _fill_queries(qaug, q_ref, r0, tq, feats):
    for p in range(PAIRS):
        qaug[p * tq:(p + 1) * tq, 0:LANES] = q_ref[0, r0:r0 + tq, p * LANES:(p + 1) * LANES]
        qaug[p * tq:(p + 1) * tq, LANES:2 * LANES] = feats[p].astype(qaug.dtype)


def _tile_pairs(x):
    return jnp.concatenate([x] * PAIRS, axis=1)


def _finish(o_ref, r0, tq, acct_e, acct_o, gate_ref, gate_col, normalized=False):
    lane = lax.broadcasted_iota(jnp.int32, (tq, LANES), 1)
    if gate_ref is not None:
        gsig = jax.nn.sigmoid(gate_ref[0, r0:r0 + tq, :])
    for p in range(PAIRS):
        cols = slice(p * tq, (p + 1) * tq)
        top = jnp.concatenate([acct_e[0:HEAD_DIM, cols], acct_o[0:HEAD_DIM, cols]], axis=0)
        if not normalized:
            den = jnp.concatenate(
                [jnp.broadcast_to(acct_e[HEAD_DIM:HEAD_DIM + 1, cols], (HEAD_DIM, tq)),
                 jnp.broadcast_to(acct_o[HEAD_DIM:HEAD_DIM + 1, cols], (HEAD_DIM, tq))], axis=0)
            top = top / den
        out = top.T
        if gate_ref is not None:
            ce = 3 * (2 * p) + gate_col
            co = 3 * (2 * p + 1) + gate_col
            out = out * jnp.where(lane < HEAD_DIM, gsig[:, ce:ce + 1], gsig[:, co:co + 1])
        o_ref[0, r0:r0 + tq, p * LANES:(p + 1) * LANES] = out.astype(o_ref.dtype)


def _banded_kernel(*refs, n_chunks, window, has_sink, gate_col, layer):
    refs = list(refs)
    if has_sink:
        sink_ref, slope_ref = refs[0], refs[1]
        refs = refs[2:]
    q_ref, k_ref, v_ref, qf_ref, kf_ref = refs[:5]
    refs = refs[5:]
    gate_ref = None
    if gate_col is not None:
        gate_ref = refs[0]
        refs = refs[1:]
    o_ref, kaug, vt, qaug = refs
    g = pl.program_id(1)
    tq = Q_SUB
    rows = PAIRS * tq
    ch = BAND_CHUNK
    n_pad = n_chunks - 1

    @pl.when(pl.program_id(2) == 0)
    def _():
        _build_keys(g, lambda s0: k_ref[0, pl.ds(s0, ch), :].astype(F32),
                    lambda s0: v_ref[0, pl.ds(s0, ch), :].astype(F32),
                    k_ref.shape[1] // ch, kf_ref, kaug, vt, n_pad, ch)

    krow = lax.broadcasted_iota(jnp.int32, (ch, tq), 0)
    qcol = lax.broadcasted_iota(jnp.int32, (ch, tq), 1)
    vrow = lax.broadcasted_iota(jnp.int32, (V_ROWS, rows), 0)
    for sub in range(q_ref.shape[1] // tq):
        i = pl.program_id(2) * (q_ref.shape[1] // tq) + sub
        _fill_queries(qaug.at[sub], q_ref, sub * tq, tq,
                      [jnp.broadcast_to(qf_ref[0, p:p + 1, :], (tq, LANES)) for p in range(PAIRS)])
        kc = kaug[pl.ds(i, n_chunks)].reshape(n_chunks * 2 * ch, 2 * LANES)
        st = _dot_nt(kc, qaug[sub])

        t = i * tq + qcol
        ev, od = [], []
        for u in range(n_chunks):
            xe, xo = st[(2 * u) * ch:(2 * u + 1) * ch], st[(2 * u + 1) * ch:(2 * u + 2) * ch]
            if u == 0 or u == n_chunks - 1:
                spos = (i - n_pad + u) * ch + krow
                dist = t - spos
                vis = jnp.where(dist >= 0, jnp.where(dist < window, jnp.where(spos >= 0, 1.0, 0.0), 0.0), 0.0)
                bias = _tile_pairs(jnp.where(vis > 0.5, 0.0, NEG))
                xe, xo = xe + bias, xo + bias
            ev.append(xe)
            od.append(xo)

        tpos = (i * tq + lax.broadcasted_iota(jnp.int32, (1, tq), 1)).astype(F32)
        vmat = jnp.concatenate([vt[i + u] for u in range(n_chunks)], axis=1)
        accs = []
        for h, slabs in enumerate((ev, od)):
            m = slabs[0].max(axis=0, keepdims=True)
            for x in slabs[1:]:
                m = jnp.maximum(m, x.max(axis=0, keepdims=True))
            if has_sink:
                sk = jnp.concatenate(
                    [sink_ref[layer, g * HEADS_PER_GROUP + 2 * p + h] * LOG2E
                     + slope_ref[g * HEADS_PER_GROUP + 2 * p + h] * tpos for p in range(PAIRS)], axis=1)
                m = jnp.maximum(m, sk)
            pmat = jnp.concatenate([jnp.exp2(x - m).astype(vt.dtype) for x in slabs], axis=0)
            acc = _dot(vmat, pmat)
            if has_sink:
                acc = acc + jnp.where(vrow >= HEAD_DIM, jnp.exp2(sk - m), 0.0)
            accs.append(acc)
        _finish(o_ref, sub * tq, tq, accs[0], accs[1], gate_ref, gate_col)


def _banded_attention(q, kv, k_col, v_col, qfeat, kfeat, window, sinks=None, slopes=None,
                      gates=None, gate_col=None, layer=0):
    b, s, _ = q.shape
    tq = Q_BLOCK
    n_sub = tq // Q_SUB
    n_chunks = window // BAND_CHUNK + 1
    n_tot = kfeat.shape[0]
    has_sink = sinks is not None
    kern = functools.partial(_banded_kernel, n_chunks=n_chunks, window=window,
                             has_sink=has_sink, gate_col=gate_col, layer=layer)
    in_specs, args = [], []
    if has_sink:
        in_specs += [pl.BlockSpec(memory_space=pltpu.SMEM)] * 2
        args += [sinks, slopes]
    in_specs += [pl.BlockSpec((1, tq, PAIRS * LANES), lambda bi, g, i: (bi, i, g)),
                 pl.BlockSpec((1, s, LANES), lambda bi, g, i: (bi, 0, k_col)),
                 pl.BlockSpec((1, s, LANES), lambda bi, g, i: (bi, 0, v_col)),
                 pl.BlockSpec((1, PAIRS, LANES), lambda bi, g, i: (g, 0, 0)),
                 _const_spec(kfeat.shape)]
    args += [q, kv, kv, qfeat, kfeat]
    if gate_col is not None:
        in_specs.append(pl.BlockSpec((1, tq, LANES), lambda bi, g, i: (bi, i, g)))
        args.append(gates)
    return pl.pallas_call(
        kern,
        grid=(b, N_KV_GROUPS, s // tq),
        in_specs=in_specs,
        out_specs=pl.BlockSpec((1, tq, PAIRS * LANES), lambda bi, g, i: (bi, i, g)),
        out_shape=jax.ShapeDtypeStruct(q.shape, MXU_DT),
        scratch_shapes=[pltpu.VMEM((n_tot, 2, BAND_CHUNK, 2 * LANES), MXU_DT),
                        pltpu.VMEM((n_tot, V_ROWS, BAND_CHUNK), MXU_DT),
                        pltpu.VMEM((n_sub, PAIRS * Q_SUB, 2 * LANES), MXU_DT)],
        compiler_params=_params(("arbitrary", "arbitrary", "arbitrary")),
        name="banded_attention_w%d" % window,
    )(*args)


def _cmpsel_kernel(q_ref, k_ref, v_ref, qf_ref, kf_ref, ovt_ref, gate_ref, o_ref, mf_ref,
                   kaug, vt, qaug):
    g = pl.program_id(1)
    ncp = k_ref.shape[2]

    @pl.when(pl.program_id(2) == 0)
    def _():
        _build_keys(g, lambda s0: k_ref[0, 0], lambda s0: v_ref[0, 0], 1, kf_ref, kaug, vt, 0, ncp)

    n_sub = q_ref.shape[1] // Q_SUB
    for sub in range(n_sub):
        _cmpsel_block(pl.program_id(2) * n_sub + sub, sub, ncp, q_ref, qf_ref, ovt_ref, gate_ref,
                      o_ref, mf_ref, kaug, vt, qaug)


def _cmpsel_block(i, sub, ncp, q_ref, qf_ref, ovt_ref, gate_ref, o_ref, mf_ref, kaug, vt, qaug):
    tq = Q_SUB
    n_slc = NSA_SLC_LEN
    _fill_queries(qaug.at[sub], q_ref, sub * tq, tq,
                  [jnp.broadcast_to(qf_ref[0, p:p + 1, :], (tq, LANES)) for p in range(PAIRS)])
    st = _dot_nt(kaug[0].reshape(2 * ncp, 2 * LANES), qaug[sub])

    krow = lax.broadcasted_iota(jnp.int32, (ncp, tq), 0)
    qcol = lax.broadcasted_iota(jnp.int32, (ncp, tq), 1)
    t = i * tq + qcol
    cmp_end = krow * NSA_CMP_STRIDE + (NSA_CMP_LEN - 1)
    vis = jnp.where(t >= cmp_end, jnp.where(krow < ncp - 1, 1.0, 0.0), 0.0)
    bias = _tile_pairs(jnp.where(vis > 0.5, 0.0, NEG))

    pn = []
    for h in range(2):
        x = st[h * ncp:(h + 1) * ncp] + bias
        m = x.max(axis=0, keepdims=True)
        m = jnp.where(m > 0.5 * NEG, m, 0.0)
        e = jnp.exp2(x - m)
        d = e.sum(axis=0, keepdims=True)
        pn.append(e * (1.0 / jnp.where(d > 0.0, d, 1.0)))
    _finish(o_ref, sub * tq, tq, _dot(vt[0], pn[0].astype(vt.dtype)), _dot(vt[0], pn[1].astype(vt.dtype)),
            gate_ref, 0, normalized=True)

    ps = pn[0] + pn[1]
    psum = ps[:, 0:tq]
    for p in range(1, PAIRS):
        psum = psum + ps[:, p * tq:(p + 1) * tq]
    p_hi = psum.astype(MXU_DT)
    p_lo = (psum - p_hi.astype(F32)).astype(MXU_DT)
    imp = (_dot(ovt_ref[...], p_hi) + _dot(ovt_ref[...], p_lo))[0:n_slc]
    blk = lax.broadcasted_iota(jnp.int32, (n_slc, tq), 0)
    tq_pos = i * tq + lax.broadcasted_iota(jnp.int32, (n_slc, tq), 1)
    cur = lax.shift_right_logical(tq_pos, int(math.log2(NSA_SLC_LEN)))
    forced = jnp.where(blk == 0, 1.0, jnp.where(blk == cur, 1.0, jnp.where(blk == cur - 1, 1.0, 0.0)))
    valid = blk <= cur
    score = jnp.where(valid, jnp.where(forced > 0.5, jnp.inf, imp), -jnp.inf)
    groups = [score[8 * v:8 * v + 8] for v in range(n_slc // 8)]
    ranks = [jnp.zeros((8, tq), F32) for _ in groups]
    sublane = lax.broadcasted_iota(jnp.int32, (8, tq), 0)
    for j in range(n_slc):
        rj = score[j:j + 1, :]
        for v in range(n_slc // 8):
            if 8 * v > j:
                beats = jnp.where(rj >= groups[v], 1.0, 0.0)
            elif 8 * v + 7 < j:
                beats = jnp.where(rj > groups[v], 1.0, 0.0)
            else:
                beats = jnp.where(sublane > (j - 8 * v), jnp.where(rj >= groups[v], 1.0, 0.0),
                                  jnp.where(rj > groups[v], 1.0, 0.0))
            ranks[v] = ranks[v] + beats
    rank = jnp.concatenate(ranks, axis=0)
    keep = jnp.where(valid, jnp.where(rank < float(NSA_TOPK), 1.0, 0.0), 0.0)
    feat = jnp.where(keep > 0.5, 0.0, NEG)
    feat = jnp.concatenate([feat, jnp.zeros((LANES - n_slc, tq), F32)], axis=0)
    mf_ref[0, 0, sub * tq:(sub + 1) * tq, :] = feat.T.astype(mf_ref.dtype)


def _compressed_and_select(q, kvcmp, qfeat, kfeat, ovt, gates):
    b, s, _ = q.shape
    tq = Q_BLOCK
    ncp = kvcmp.shape[2]
    return pl.pallas_call(
        _cmpsel_kernel,
        grid=(b, N_KV_GROUPS, s // tq),
        in_specs=[pl.BlockSpec((1, tq, PAIRS * LANES), lambda bi, g, i: (bi, i, g)),
                  pl.BlockSpec((1, 1, ncp, LANES), lambda bi, g, i: (bi, 0, 0, 0)),
                  pl.BlockSpec((1, 1, ncp, LANES), lambda bi, g, i: (bi, 1, 0, 0)),
                  pl.BlockSpec((1, PAIRS, LANES), lambda bi, g, i: (g, 0, 0)),
                  _const_spec(kfeat.shape), _const_spec(ovt.shape),
                  pl.BlockSpec((1, tq, LANES), lambda bi, g, i: (bi, i, g))],
        out_specs=[pl.BlockSpec((1, tq, PAIRS * LANES), lambda bi, g, i: (bi, i, g)),
                   pl.BlockSpec((1, 1, tq, LANES), lambda bi, g, i: (bi, g, i, 0))],
        out_shape=[jax.ShapeDtypeStruct(q.shape, MXU_DT),
                   jax.ShapeDtypeStruct((b, N_KV_GROUPS, s, LANES), MXU_DT)],
        scratch_shapes=[pltpu.VMEM((1, 2, ncp, 2 * LANES), MXU_DT),
                        pltpu.VMEM((1, V_ROWS, ncp), MXU_DT),
                        pltpu.VMEM((tq // Q_SUB, PAIRS * Q_SUB, 2 * LANES), MXU_DT)],
        compiler_params=_params(("arbitrary", "arbitrary", "arbitrary")),
        name="nsa_compressed_select",
    )(q, kvcmp, kvcmp, qfeat, kfeat, ovt, gates)


def _selected_kernel(q_ref, k_ref, v_ref, qf_ref, mf_ref, kf_ref, gate_ref, o_ref,
                     kaug, vt, qaug, acc_e, acc_o, st_a, st_b):
    g = pl.program_id(1)
    i = pl.program_id(2)
    tq = q_ref.shape[1]
    ch = SLC_CHUNK
    rows = PAIRS * tq

    @pl.when(i == 0)
    def _():
        _build_keys(g, lambda s0: k_ref[0, pl.ds(s0, ch), :].astype(F32),
                    lambda s0: v_ref[0, pl.ds(s0, ch), :].astype(F32),
                    k_ref.shape[1] // ch, kf_ref, kaug, vt, 0, ch)

    mask_feat = mf_ref[0, 0].astype(F32)
    _fill_queries(qaug, q_ref, 0, tq, [mask_feat + qf_ref[0, p:p + 1, :] for p in range(PAIRS)])
    acc_e[...] = jnp.zeros((V_ROWS, rows), F32)
    acc_o[...] = jnp.zeros((V_ROWS, rows), F32)

    def update(x, m_old, acc_ref, vt_c):
        m_new = jnp.maximum(m_old, x.max(axis=0, keepdims=True))
        alpha = jnp.exp2(m_old - m_new)
        p = jnp.exp2(x - m_new).astype(vt_c.dtype)
        acc_ref[...] = alpha * acc_ref[...] + _dot(vt_c, p)
        return m_new

    def scores_into(buf, c):
        buf[...] = _dot_nt(kaug[c].reshape(2 * ch, 2 * LANES), qaug[...])

    def consume(buf, c, ms, causal):
        xe, xo = buf[0:ch, :], buf[ch:2 * ch, :]
        if causal:
            krow = lax.broadcasted_iota(jnp.int32, (ch, tq), 0)
            qcol = lax.broadcasted_iota(jnp.int32, (ch, tq), 1)
            bias = _tile_pairs(jnp.where(c * ch + krow <= i * tq + qcol, 0.0, NEG))
            xe, xo = xe + bias, xo + bias
        return update(xe, ms[0], acc_e, vt[c]), update(xo, ms[1], acc_o, vt[c])

    def pair(k, ms):
        c = 2 * k
        scores_into(st_b, c + 1)
        ms = consume(st_a, c, ms, False)
        scores_into(st_a, c + 2)
        return consume(st_b, c + 1, ms, False)

    assert tq == ch
    m0 = jnp.full((1, rows), M_INIT, F32)
    scores_into(st_a, 0)
    ms = lax.fori_loop(0, i // 2, pair, (m0, m0))

    @pl.when(i % 2 == 0)
    def _():
        consume(st_a, i, ms, True)

    @pl.when(i % 2 == 1)
    def _():
        scores_into(st_b, i)
        consume(st_b, i, consume(st_a, i - 1, ms, False), True)

    _finish(o_ref, 0, tq, acc_e[...], acc_o[...], gate_ref, 1)


def _selected_attention(q, kv, k_col, v_col, qfeat, maskfeat, kfeat, gates):
    b, s, _ = q.shape
    tq = Q_BLOCK
    n_tot = kfeat.shape[0]
    return pl.pallas_call(
        _selected_kernel,
        grid=(b, N_KV_GROUPS, s // tq),
        in_specs=[pl.BlockSpec((1, tq, PAIRS * LANES), lambda bi, g, i: (bi, i, g)),
                  pl.BlockSpec((1, s, LANES), lambda bi, g, i: (bi, 0, k_col)),
                  pl.BlockSpec((1, s, LANES), lambda bi, g, i: (bi, 0, v_col)),
                  pl.BlockSpec((1, PAIRS, LANES), lambda bi, g, i: (g, 0, 0)),
                  pl.BlockSpec((1, 1, tq, LANES), lambda bi, g, i: (bi, g, i, 0)),
                  _const_spec(kfeat.shape),
                  pl.BlockSpec((1, tq, LANES), lambda bi, g, i: (bi, i, g))],
        out_specs=pl.BlockSpec((1, tq, PAIRS * LANES), lambda bi, g, i: (bi, i, g)),
        out_shape=jax.ShapeDtypeStruct(q.shape, MXU_DT),
        scratch_shapes=[pltpu.VMEM((n_tot, 2, SLC_CHUNK, 2 * LANES), MXU_DT),
                        pltpu.VMEM((n_tot, V_ROWS, SLC_CHUNK), MXU_DT),
                        pltpu.VMEM((PAIRS * tq, 2 * LANES), MXU_DT),
                        pltpu.VMEM((V_ROWS, PAIRS * tq), F32),
                        pltpu.VMEM((V_ROWS, PAIRS * tq), F32),
                        pltpu.VMEM((2 * SLC_CHUNK, PAIRS * tq), F32),
                        pltpu.VMEM((2 * SLC_CHUNK, PAIRS * tq), F32)],
        compiler_params=_params(("arbitrary", "arbitrary", "arbitrary")),
        name="nsa_selected",
    )(q, kv, kv, qfeat, maskfeat, kfeat, gates)


def _outproj_kernel(oa_ref, oc_ref, os_ref, ow_ref, x_ref, w_ref, b_ref, g_ref, beta_ref,
                    wr_ref, br_ref, x1_ref, route_ref, *, alpha):
    half = oa_ref.shape[1]
    on = (oc_ref[...].astype(F32) + os_ref[...].astype(F32) + ow_ref[...].astype(F32)).astype(MXU_DT)
    mix = _dot(oa_ref[...], w_ref[0:half]) + _dot(on, w_ref[half:]) + b_ref[...]
    x1 = _layer_norm(alpha * x_ref[...] + mix, g_ref[...], beta_ref[...])
    x1_ref[...] = x1

    x_hi = x1.astype(MXU_DT)
    x_lo = (x1 - x_hi.astype(F32)).astype(MXU_DT)
    logits = (_dot(x_hi, wr_ref[0]) + _dot(x_hi, wr_ref[1]) + _dot(x_lo, wr_ref[0])) + br_ref[...]
    lane = lax.broadcasted_iota(jnp.int32, logits.shape, 1)
    lanef = lane.astype(F32)
    big = float(4 * LANES)
    gl = jnp.where(lane < N_GROUPS, logits, -jnp.inf)
    gmax = gl.max(axis=1, keepdims=True)
    gsel = jnp.where(gl == gmax, lanef, big).min(axis=1, keepdims=True)
    g_w = 1.0 / jnp.exp(gl - gmax).sum(axis=1, keepdims=True)
    lo = N_GROUPS + gsel * EXPERTS_PER_GROUP
    el = jnp.where(lanef >= lo, jnp.where(lanef < lo + EXPERTS_PER_GROUP, logits, -jnp.inf), -jnp.inf)
    v1 = el.max(axis=1, keepdims=True)
    i1 = jnp.where(el == v1, lanef, big).min(axis=1, keepdims=True)
    el2 = jnp.where(lanef == i1, -jnp.inf, el)
    v2 = el2.max(axis=1, keepdims=True)
    i2 = jnp.where(el2 == v2, lanef, big).min(axis=1, keepdims=True)
    e2 = jnp.exp(v2 - v1)
    w1 = g_w / (1.0 + e2)
    w2 = g_w * e2 / (1.0 + e2)
    route_ref[...] = jnp.where(lane == 0, i1 - N_GROUPS,
                               jnp.where(lane == 1, i2 - N_GROUPS,
                                         jnp.where(lane == 2, w1, jnp.where(lane == 3, w2, 0.0))))


def _output_projection(oa, oc, os_, ow, x2d, w, b, ln_g, ln_b, wr, br, alpha, layer):
    n, d = x2d.shape
    half = oa.shape[1]
    tm = min(PROJ_TILE, n)
    row = lambda i: (i, 0)
    return pl.pallas_call(
        functools.partial(_outproj_kernel, alpha=alpha),
        grid=(n // tm,),
        in_specs=[pl.BlockSpec((tm, half), row)] * 4 + [pl.BlockSpec((tm, d), row)]
        + [_layer_spec(a, layer) for a in (w, b, ln_g, ln_b, wr, br)],
        out_specs=[pl.BlockSpec((tm, d), row), pl.BlockSpec((tm, LANES), row)],
        out_shape=[jax.ShapeDtypeStruct((n, d), F32), jax.ShapeDtypeStruct((n, LANES), F32)],
        compiler_params=_params(("arbitrary",)),
        name="output_projection_ln_route",
    )(oa, oc, os_, ow, x2d, w, b, ln_g, ln_b, wr, br)


def _row_copy(src_hbm, src_row, dst, j, u, sem):
    return pltpu.make_async_copy(src_hbm.at[pl.ds(src_row, 1)], dst.at[j, pl.ds(u, 1)], sem)


def _rows_start(src_hbm, idx_ref, base, dst, sem, n_groups):
    def body(j, carry):
        for u in range(SUBLANES):
            _row_copy(src_hbm, idx_ref[base + j * SUBLANES + u], dst, j, u, sem).start(priority=u % 2)
        return carry
    lax.fori_loop(0, n_groups, body, 0)


def _rows_wait(src_hbm, dst, sem, n_groups):
    def body(j, carry):
        for u in range(SUBLANES):
            _row_copy(src_hbm, 0, dst, j, u, sem).wait()
        return carry
    lax.fori_loop(0, n_groups, body, 0)


def _expert_kernel(blk_e_ref, n_used_ref, first_ref, wslot_ref, next_e_ref, row_tok_ref,
                   x_hbm, wg_hbm, wu_hbm, wd_hbm, o_ref,
                   xbuf, xsem, wg_f, wu_f, wd_f, wsem, wg_b, wu_b, wd_b, *, layer):
    i = pl.program_id(0)
    n_used = n_used_ref[0]
    groups, d = xbuf.shape[1], xbuf.shape[3]
    t = groups * SUBLANES
    hid = wg_b.shape[1]
    used = i < n_used
    slot = lax.rem(i, 2)

    def weight_copies(e, s):
        return [pltpu.make_async_copy(w.at[layer, e], buf.at[s], wsem.at[s])
                for w, buf in ((wg_hbm, wg_f), (wu_hbm, wu_f), (wd_hbm, wd_f))]

    def rows_start(blk, s, group_range):
        for j in group_range:
            for u in range(SUBLANES):
                _row_copy(x_hbm, row_tok_ref[blk * t + j * SUBLANES + u], xbuf.at[s], j, u,
                          xsem.at[s]).start()

    def rows_wait(s):
        for j in range(groups):
            for u in range(SUBLANES):
                _row_copy(x_hbm, 0, xbuf.at[s], j, u, xsem.at[s]).wait()

    @pl.when(i == 0)
    def _():
        for c in weight_copies(blk_e_ref[0], 0):
            c.start(priority=1)
        rows_start(0, 0, range(groups))

    @pl.when(used)
    def _():
        rows_wait(slot)
        ws = wslot_ref[i]

        @pl.when(first_ref[i] == 1)
        def _():
            for c in weight_copies(0, ws):
                c.wait()

            @pl.when(next_e_ref[i] >= 0)
            def _():
                for c in weight_copies(next_e_ref[i], 1 - ws):
                    c.start(priority=1)

            wg_b[...] = wg_f[ws].astype(wg_b.dtype)
            wu_b[...] = wu_f[ws].astype(wu_b.dtype)
            wd_b[...] = wd_f[ws].astype(wd_b.dtype)

        nxt = jnp.minimum(i + 1, n_used - 1)
        xb = xbuf[slot].reshape(t, d).astype(MXU_DT)
        pieces = 2
        y = None
        for piece in range(pieces):
            cols = slice(piece * hid // pieces, (piece + 1) * hid // pieces)
            gate = _dot(xb, wg_b[:, cols])
            up = _dot(xb, wu_b[:, cols])
            rows_start(nxt, 1 - slot, range(piece * groups // pieces, (piece + 1) * groups // pieces))
            hidden = (gate * jax.nn.sigmoid(gate) * up).astype(MXU_DT)
            part = _dot(hidden, wd_b[cols, :])
            y = part if y is None else y + part
        o_ref[...] = y

        @pl.when(i == n_used - 1)
        def _():
            rows_wait(1 - slot)

    @pl.when(jnp.logical_not(used))
    def _():
        o_ref[...] = jnp.zeros(o_ref.shape, o_ref.dtype)


def _expert_ffn(x1, row_tok, blk_e, n_used, first, wslot, next_e, wg, wu, wd, layer):
    d = x1.shape[1]
    p = row_tok.shape[0]
    t = EXPERT_ROWS
    hid = wg.shape[3]
    any_spec = pl.BlockSpec(memory_space=pl.ANY)
    return pl.pallas_call(
        functools.partial(_expert_kernel, layer=layer),
        grid_spec=pltpu.PrefetchScalarGridSpec(
            num_scalar_prefetch=6,
            grid=(p // t,),
            in_specs=[any_spec, any_spec, any_spec, any_spec],
            out_specs=pl.BlockSpec((t, d), lambda i, *_: (i, 0)),
            scratch_shapes=[pltpu.VMEM((2, t // SUBLANES, SUBLANES, d), F32), pltpu.SemaphoreType.DMA((2,)),
                            pltpu.VMEM((2, d, hid), F32), pltpu.VMEM((2, d, hid), F32),
                            pltpu.VMEM((2, hid, d), F32), pltpu.SemaphoreType.DMA((2,)),
                            pltpu.VMEM((d, hid), MXU_DT), pltpu.VMEM((d, hid), MXU_DT),
                            pltpu.VMEM((hid, d), MXU_DT)]),
        out_shape=jax.ShapeDtypeStruct((p, d), F32),
        compiler_params=_params(("arbitrary",)),
        name="expert_ffn",
    )(blk_e, n_used, first, wslot, next_e, row_tok, x1, wg, wu, wd)


def _combine_kernel(dest_ref, ys_hbm, route_ref, x_ref, g_ref, b_ref, o_ref, ybuf, sem, *, alpha, n_tok):
    i = pl.program_id(0)
    tm, d = x_ref.shape
    groups = tm // SUBLANES
    slot = lax.rem(i, 2)

    def start(blk, s):
        for k in range(2):
            _rows_start(ys_hbm, dest_ref, k * n_tok + blk * tm, ybuf.at[s, k], sem.at[s], groups)

    @pl.when(i == 0)
    def _():
        start(0, 0)

    @pl.when(i + 1 < pl.num_programs(0))
    def _():
        start(i + 1, 1 - slot)

    for k in range(2):
        _rows_wait(ys_hbm, ybuf.at[slot, k], sem.at[slot], groups)
    r = route_ref[...]
    y = r[:, 2:3] * ybuf[slot, 0].reshape(tm, d) + r[:, 3:4] * ybuf[slot, 1].reshape(tm, d)
    o_ref[...] = _layer_norm(alpha * x_ref[...] + y, g_ref[...], b_ref[...])


def _combine(ys, dest_kmajor, route, x1, ln_g, ln_b, alpha, layer):
    n, d = x1.shape
    tm = min(ROW_TILE, n)
    row = lambda i, dst: (i, 0)
    return pl.pallas_call(
        functools.partial(_combine_kernel, alpha=alpha, n_tok=n),
        grid_spec=pltpu.PrefetchScalarGridSpec(
            num_scalar_prefetch=1,
            grid=(n // tm,),
            in_specs=[pl.BlockSpec(memory_space=pl.ANY),
                      pl.BlockSpec((tm, LANES), row), pl.BlockSpec((tm, d), row),
                      _layer_spec(ln_g, layer), _layer_spec(ln_b, layer)],
            out_specs=pl.BlockSpec((tm, d), row),
            scratch_shapes=[pltpu.VMEM((2, 2, tm // SUBLANES, SUBLANES, d), F32),
                            pltpu.SemaphoreType.DMA((2,))]),
        out_shape=jax.ShapeDtypeStruct((n, d), F32),
        compiler_params=_params(("arbitrary",)),
        name="combine_ln",
    )(dest_kmajor, ys, route, x1, ln_g, ln_b)


def _dispatch_plan(route, n_tokens):
    t = EXPERT_ROWS
    a = 2 * n_tokens
    eid = route[:, 0:2].astype(jnp.int32).reshape(a)
    onehot = (eid[:, None] == jnp.arange(N_EXPERTS, dtype=jnp.int32)[None, :]).astype(jnp.int32)
    counts = onehot.sum(axis=0)
    before = jnp.cumsum(onehot, axis=0) - onehot
    rank = jnp.take_along_axis(before, eid[:, None], axis=1)[:, 0]
    padded = (counts + t - 1) // t * t
    pend = jnp.cumsum(padded)
    pstart = pend - padded
    dest = (pstart[eid] + rank).astype(jnp.int32)
    p_rows = a + N_EXPERTS * t
    filler = jnp.arange(p_rows, dtype=jnp.int32) % n_tokens
    row_tok = filler.at[dest].set(jnp.arange(a, dtype=jnp.int32) // 2)
    blk_start = jnp.arange(p_rows // t, dtype=jnp.int32) * t
    blk_e = jnp.minimum((pend[None, :] <= blk_start[:, None]).astype(jnp.int32).sum(axis=1), N_EXPERTS - 1)
    blk_e = blk_e.astype(jnp.int32)
    n_used = (pend[-1] // t).astype(jnp.int32)
    blk = jnp.arange(p_rows // t, dtype=jnp.int32)
    first = jnp.concatenate([jnp.ones((1,), jnp.int32), (blk_e[1:] != blk_e[:-1]).astype(jnp.int32)])
    wslot = (jnp.cumsum(first) - 1) % 2
    is_used = blk < n_used
    n_le = ((blk_e[None, :] <= blk_e[:, None]) & is_used[None, :]).astype(jnp.int32).sum(axis=1)
    next_e = jnp.where(n_le < n_used, blk_e[jnp.minimum(n_le, p_rows // t - 1)], -1)
    dest_kmajor = dest.reshape(n_tokens, 2).T.reshape(a)
    return (dest_kmajor, row_tok, blk_e, n_used.reshape(1), first, wslot.astype(jnp.int32),
            next_e.astype(jnp.int32))


def kernel(x, w_in, b_in, swa_sinks, cmp_pe_k, cmp_w1_k, cmp_w2_k, cmp_pe_v, cmp_w1_v, cmp_w2_v,
           w_out, b_out, ln1_g, ln1_b, w_group, b_group, w_expert, b_expert, we_gate, we_up,
           we_down, ln2_g, ln2_b):
    bsz, seq, d_model = x.shape
    depth = w_in.shape[0]
    n_tok = bsz * seq
    n_heads = d_model // HEAD_DIM
    q_w = n_heads // 2 * HEAD_DIM
    kv_w = N_KV_GROUPS * HEAD_DIM
    alpha = (2.0 * depth) ** 0.25
    scale = HEAD_DIM ** -0.5 * LOG2E
    assert q_w == N_KV_GROUPS * PAIRS * LANES and seq % SLC_CHUNK == 0

    slopes_a, slopes_n = _alibi_slopes(n_heads)
    slopes_a = (slopes_a * np.float32(LOG2E)).astype(np.float32)
    slopes_n = (slopes_n * np.float32(LOG2E)).astype(np.float32)
    qfeat_a = jnp.asarray(_query_feats(slopes_a), F32)
    qfeat_n = jnp.asarray(_query_feats(slopes_n), F32)
    pos = np.arange(seq)
    kfeat_swa = jnp.asarray(_key_feats(pos, SWA_WINDOW // BAND_CHUNK, BAND_CHUNK, False), MXU_DT)
    kfeat_win = jnp.asarray(_key_feats(pos, NSA_WINDOW // BAND_CHUNK, BAND_CHUNK, False), MXU_DT)
    kfeat_slc = jnp.asarray(_key_feats(pos, 0, SLC_CHUNK, True), MXU_DT)
    n_cmp = (seq - NSA_CMP_LEN) // NSA_CMP_STRIDE + 1
    n_cmp_pad = seq // NSA_CMP_STRIDE
    cmp_end = np.arange(n_cmp_pad) * NSA_CMP_STRIDE + NSA_CMP_LEN - 1
    kfeat_cmp = jnp.asarray(_key_feats(cmp_end, 0, n_cmp_pad, False), MXU_DT)
    n_slc = seq // NSA_SLC_LEN
    cs = np.arange(n_cmp_pad)[None, :] * NSA_CMP_STRIDE
    ss = np.arange(LANES)[:, None] * NSA_SLC_LEN
    ovt = ((cs < ss + NSA_SLC_LEN) & (cs + NSA_CMP_LEN - 1 >= ss)
           & (np.arange(n_cmp_pad)[None, :] < n_cmp) & (np.arange(LANES)[:, None] < n_slc))
    ovt = jnp.asarray(ovt.astype(np.float32), MXU_DT)
    slopes_a_s = jnp.asarray(slopes_a, F32)

    o_qa, o_ka = 0, q_w
    o_qn = q_w + 2 * kv_w
    o_kvn = o_qn + q_w
    o_gn = o_kvn + 6 * kv_w
    gate_w = 3 * HEADS_PER_GROUP

    def seg(a, lo, hi):
        return a[..., lo:hi]

    def gate_cols(a):
        pad = [(0, 0)] * (a.ndim - 1) + [(0, LANES - gate_w)]
        return jnp.concatenate([jnp.pad(seg(a, o_gn + g * gate_w, o_gn + (g + 1) * gate_w), pad)
                                for g in range(N_KV_GROUPS)], axis=-1)

    def regroup(a):
        return jnp.concatenate([seg(a, o_qa, o_qa + q_w) * scale, seg(a, o_qn, o_qn + q_w) * scale,
                                seg(a, o_ka, o_ka + 2 * kv_w), seg(a, o_kvn, o_kvn + 6 * kv_w),
                                gate_cols(a)], axis=-1)

    w_r = regroup(w_in).astype(MXU_DT)
    b_r = regroup(b_in)[:, None, :]
    flat = NSA_CMP_LEN * HEAD_DIM
    pe = jnp.broadcast_to(jnp.stack([cmp_pe_k, cmp_pe_v], axis=1).reshape(depth, 2, 1, flat),
                          (depth, 2, 8, flat))
    w1 = jnp.stack([cmp_w1_k, cmp_w1_v], axis=1).astype(MXU_DT)
    w2 = jnp.stack([cmp_w2_k, cmp_w2_v], axis=1).astype(MXU_DT)
    n_route_pad = LANES - N_GROUPS - N_EXPERTS
    wr = jnp.concatenate([w_group, w_expert, jnp.zeros((depth, d_model, n_route_pad), F32)], axis=2)
    br = jnp.concatenate([b_group, b_expert, jnp.zeros((depth, n_route_pad), F32)], axis=1)[:, None, :]
    wr_hi = wr.astype(MXU_DT)
    wr = jnp.stack([wr_hi, (wr - wr_hi.astype(F32)).astype(MXU_DT)], axis=1)
    w_o = w_out.astype(MXU_DT)
    b_o, g1, be1, g2, be2 = (a[:, None, :] for a in (b_out, ln1_g, ln1_b, ln2_g, ln2_b))

    h = x.reshape(n_tok, d_model)
    for l in range(depth):
        qa, qn, kv, gates = _input_projection(h, w_r, b_r, l)
        qa = qa.reshape(bsz, seq, q_w)
        qn = qn.reshape(bsz, seq, q_w)
        kv = kv.reshape(bsz, seq, 8 * kv_w)
        gates = gates.reshape(bsz, seq, 2 * LANES)

        o_a = _banded_attention(qa, kv, 0, 1, qfeat_a, kfeat_swa, SWA_WINDOW,
                                sinks=swa_sinks, slopes=slopes_a_s, layer=l)

        kvc = kv[:, :, 2 * kv_w:4 * kv_w].reshape(bsz, n_cmp_pad, NSA_CMP_STRIDE, 2, N_KV_GROUPS, HEAD_DIM)
        kvc = kvc.transpose(0, 3, 4, 1, 2, 5).reshape(bsz, 2, N_KV_GROUPS, n_cmp_pad, NSA_CMP_STRIDE * HEAD_DIM)
        kvcmp = _compress(kvc, pe, w1, w2, l)

        o_cmp, maskfeat = _compressed_and_select(qn, kvcmp, qfeat_n, kfeat_cmp, ovt, gates)
        o_slc = _selected_attention(qn, kv, 4, 5, qfeat_n, maskfeat, kfeat_slc, gates)
        o_win = _banded_attention(qn, kv, 6, 7, qfeat_n, kfeat_win, NSA_WINDOW,
                                  gates=gates, gate_col=2)

        x1, route = _output_projection(
            o_a.reshape(n_tok, q_w), o_cmp.reshape(n_tok, q_w), o_slc.reshape(n_tok, q_w),
            o_win.reshape(n_tok, q_w), h, w_o, b_o, g1, be1, wr, br, alpha, l)

        dest_kmajor, row_tok, blk_e, n_used, first, wslot, next_e = _dispatch_plan(route, n_tok)
        ys = _expert_ffn(x1, row_tok, blk_e, n_used, first, wslot, next_e, we_gate, we_up, we_down, l)
        h = _combine(ys, dest_kmajor, route, x1, g2, be2, alpha, l)
    return h.reshape(bsz, seq, d_model)
```

```python
import functools
import math

import numpy as np
import jax
import jax.numpy as jnp
from jax import lax
from jax.experimental import pallas as pl
from jax.experimental.pallas import tpu as pltpu

F32 = jnp.float32
MXU_DT = jnp.bfloat16

HEAD_DIM = 64
HEADS_PER_GROUP = 8
PAIRS = HEADS_PER_GROUP // 2
N_KV_GROUPS = 2
SWA_WINDOW = 128
NSA_WINDOW = 512
NSA_CMP_LEN = 32
NSA_CMP_STRIDE = 16
NSA_CMP_HIDDEN = 256
NSA_SLC_LEN = 64
NSA_TOPK = 16
N_GROUPS = 8
EXPERTS_PER_GROUP = 8
N_EXPERTS = N_GROUPS * EXPERTS_PER_GROUP
EXPERT_HIDDEN = 512
LN_EPS = 1e-5

LANES = 128
SUBLANES = 8
V_ROWS = 80
Q_BLOCK = 512
Q_SUB = 128
BAND_CHUNK = 128
SLC_CHUNK = 512
EXPERT_ROWS = 512
ROW_TILE = 256
PROJ_TILE = 512
LOG2E = 1.4426950408889634
NEG = -1e30
M_INIT = -5e29
VMEM_LIMIT = 56 * 1024 * 1024

FEAT_EVEN = 64
FEAT_ODD = 72
FEAT_PAD = 127


def _alibi_slopes(n_heads_total):
    n = n_heads_total
    s = np.exp2(-8.0 * np.arange(1, n + 1, dtype=np.float32) / np.float32(n)).astype(np.float32)
    return s[0::2], s[1::2]


def _bf16_parts(v):
    v = np.asarray(v, np.float32)
    a = v.astype(jnp.bfloat16).astype(np.float32)
    r = (v - a).astype(np.float32)
    b = r.astype(jnp.bfloat16).astype(np.float32)
    c = (r - b).astype(np.float32)
    return a, b, c


def _query_feats(slopes):
    out = np.zeros((N_KV_GROUPS, PAIRS, LANES), np.float32)
    a, b, c = _bf16_parts(slopes)
    for g in range(N_KV_GROUPS):
        for p in range(PAIRS):
            for h, base in ((0, FEAT_EVEN), (1, FEAT_ODD)):
                idx = g * HEADS_PER_GROUP + 2 * p + h
                out[g, p, base:base + 6] = [a[idx], b[idx], c[idx], a[idx], b[idx], c[idx]]
            out[g, p, FEAT_PAD] = 1.0
    return out


def _key_feats(positions, n_pad_chunks, chunk, with_blocks):
    pos = np.asarray(positions, np.int64)
    n = pos.shape[0]
    assert n % chunk == 0
    hi = (pos // 64 * 64).astype(np.float32)
    lo = (pos % 64).astype(np.float32)
    f = np.zeros((n, 2, LANES), np.float32)
    for v, base in ((0, FEAT_EVEN), (1, FEAT_ODD)):
        f[:, v, base:base + 3] = hi[:, None]
        f[:, v, base + 3:base + 6] = lo[:, None]
    if with_blocks:
        blk = pos // NSA_SLC_LEN
        for v in range(2):
            f[np.arange(n), v, blk] = 1.0
    f = f.reshape(n // chunk, chunk, 2, LANES).transpose(0, 2, 1, 3)
    if n_pad_chunks:
        pad = np.zeros((n_pad_chunks,) + f.shape[1:], np.float32)
        pad[..., FEAT_PAD] = NEG
        f = np.concatenate([pad, f], axis=0)
    return f


def _params(sem, vmem=VMEM_LIMIT):
    return pltpu.CompilerParams(dimension_semantics=sem, vmem_limit_bytes=vmem)


def _const_spec(shape):
    nd = len(shape)
    return pl.BlockSpec(shape, lambda *_: (0,) * nd, pipeline_mode=pl.Buffered(1))


def _layer_spec(arr, layer):
    nd = arr.ndim
    return pl.BlockSpec((None,) + arr.shape[1:], lambda *_: (layer,) + (0,) * (nd - 1),
                        pipeline_mode=pl.Buffered(1))


def _dot(a, b):
    return jnp.dot(a, b, preferred_element_type=F32)


def _dot_nt(a, b):
    return lax.dot_general(a, b, (((1,), (1,)), ((), ())), preferred_element_type=F32)


def _layer_norm(h, g, b):
    mu = jnp.mean(h, axis=-1, keepdims=True)
    d = h - mu
    var = jnp.mean(d * d, axis=-1, keepdims=True)
    return d * lax.rsqrt(var + LN_EPS) * g + b


def _inproj_kernel(x_ref, w_ref, b_ref, qa_ref, qn_ref, kv_ref, gt_ref):
    xb = x_ref[...].astype(MXU_DT)
    width = qa_ref.shape[1]
    for j, o_ref in enumerate((qa_ref, qn_ref, kv_ref)):
        acc = _dot(xb, w_ref[:, j * width:(j + 1) * width]) + b_ref[:, j * width:(j + 1) * width]
        o_ref[...] = acc.astype(o_ref.dtype)
    gt_ref[...] = _dot(xb, w_ref[:, 3 * width:]) + b_ref[:, 3 * width:]


def _input_projection(x2d, w, b, layer):
    n, d = x2d.shape
    width = (w.shape[2] - 2 * LANES) // 3
    tm = min(PROJ_TILE, n)
    row = lambda i: (i, 0)
    return pl.pallas_call(
        _inproj_kernel,
        grid=(n // tm,),
        in_specs=[pl.BlockSpec((tm, d), row), _layer_spec(w, layer), _layer_spec(b, layer)],
        out_specs=[pl.BlockSpec((tm, width), row)] * 3 + [pl.BlockSpec((tm, 2 * LANES), row)],
        out_shape=[jax.ShapeDtypeStruct((n, width), MXU_DT)] * 3
        + [jax.ShapeDtypeStruct((n, 2 * LANES), F32)],
        compiler_params=_params(("arbitrary",)),
        name="input_projection",
    )(x2d, w, b)


def _gelu_tanh(x):
    c = math.sqrt(2.0 / math.pi)
    return 0.5 * x * (1.0 + jnp.tanh(c * (x + 0.044715 * (x * x * x))))


def _compress_kernel(c_ref, pe_ref, w1_ref, w2_ref, o_ref):
    half = c_ref.shape[-1]
    w1 = w1_ref[0]
    pe = pe_ref[0]
    pe_hi = pe.astype(MXU_DT)
    pe_lo = (pe - pe_hi.astype(F32)).astype(MXU_DT)
    base = (_dot(pe_hi, w1) + _dot(pe_lo, w1))[0:1]
    for g in range(N_KV_GROUPS):
        c = c_ref[0, 0, g]
        top = _dot(c, w1[:half])
        bot = _dot(c, w1[half:])
        n_rows = bot.shape[0]
        hidden = top + pltpu.roll(bot, n_rows - 1, 0) + base
        act = _gelu_tanh(hidden)
        o_ref[0, 0, :, g * HEAD_DIM:(g + 1) * HEAD_DIM] = _dot(act.astype(MXU_DT), w2_ref[0])


def _compress(chunks, pe, w1, w2, layer):
    b, two, g, nch, half = chunks.shape
    return pl.pallas_call(
        _compress_kernel,
        grid=(b, two),
        in_specs=[pl.BlockSpec((1, 1, g, nch, half), lambda i, j: (i, j, 0, 0, 0)),
                  pl.BlockSpec((None, 1, 8, 2 * half), lambda i, j: (layer, j, 0, 0)),
                  pl.BlockSpec((None, 1, 2 * half, NSA_CMP_HIDDEN), lambda i, j: (layer, j, 0, 0)),
                  pl.BlockSpec((None, 1, NSA_CMP_HIDDEN, HEAD_DIM), lambda i, j: (layer, j, 0, 0))],
        out_specs=pl.BlockSpec((1, 1, nch, LANES), lambda i, j: (i, j, 0, 0)),
        out_shape=jax.ShapeDtypeStruct((b, two, nch, LANES), F32),
        compiler_params=_params(("arbitrary", "arbitrary")),
        name="nsa_compress",
    )(chunks, pe, w1, w2)


def _place_group(x, g_is_zero):
    lane = lax.broadcasted_iota(jnp.int32, x.shape, 1)
    rolled = pltpu.roll(x, HEAD_DIM, 1)
    lo_src, hi_src = (x, rolled) if g_is_zero else (rolled, x)
    lo = jnp.where(lane < HEAD_DIM, lo_src, 0.0)
    hi = jnp.where(lane >= HEAD_DIM, hi_src, 0.0)
    return lo, hi


def _build_keys(g, load_k, load_v, n_chunks, kf_ref, kaug, vt, n_pad, chunk):
    ones = jnp.ones((V_ROWS - HEAD_DIM, chunk), F32)
    for gval in range(N_KV_GROUPS):
        @pl.when(g == gval)
        def _():
            def body(c, carry):
                start = pl.multiple_of(c * chunk, chunk)
                klo, khi = _place_group(load_k(start), gval == 0)
                vlo, _ = _place_group(load_v(start), gval == 0)
                kaug[n_pad + c, 0, :, 0:LANES] = klo.astype(kaug.dtype)
                kaug[n_pad + c, 1, :, 0:LANES] = khi.astype(kaug.dtype)
                vt[n_pad + c] = jnp.concatenate([vlo.T[0:HEAD_DIM], ones], axis=0).astype(vt.dtype)
                return carry
            lax.fori_loop(0, n_chunks, body, 0)
    kaug[:, :, :, LANES:2 * LANES] = kf_ref[...]
    if n_pad:
        kaug[0:n_pad, :, :, 0:LANES] = jnp.zeros((n_pad, 2, chunk, LANES), kaug.dtype)
        vt[0:n_pad] = jnp.zeros((n_pad, V_ROWS, chunk), vt.dtype)


def _fill_queries(qaug, q_ref, r0, tq, feats):
    for p in range(PAIRS):
        qaug[p * tq:(p + 1) * tq, 0:LANES] = q_ref[0, r0:r0 + tq, p * LANES:(p + 1) * LANES]
        qaug[p * tq:(p + 1) * tq, LANES:2 * LANES] = feats[p].astype(qaug.dtype)


def _tile_pairs(x):
    return jnp.concatenate([x] * PAIRS, axis=1)


def _finish(o_ref, r0, tq, acct_e, acct_o, gate_ref, gate_col, normalized=False):
    lane = lax.broadcasted_iota(jnp.int32, (tq, LANES), 1)
    if gate_ref is not None:
        gsig = jax.nn.sigmoid(gate_ref[0, r0:r0 + tq, :])
    for p in range(PAIRS):
        cols = slice(p * tq, (p + 1) * tq)
        top = jnp.concatenate([acct_e[0:HEAD_DIM, cols], acct_o[0:HEAD_DIM, cols]], axis=0)
        if not normalized:
            den = jnp.concatenate(
                [jnp.broadcast_to(acct_e[HEAD_DIM:HEAD_DIM + 1, cols], (HEAD_DIM, tq)),
                 jnp.broadcast_to(acct_o[HEAD_DIM:HEAD_DIM + 1, cols], (HEAD_DIM, tq))], axis=0)
            top = top / den
        out = top.T
        if gate_ref is not None:
            ce = 3 * (2 * p) + gate_col
            co = 3 * (2 * p + 1) + gate_col
            out = out * jnp.where(lane < HEAD_DIM, gsig[:, ce:ce + 1], gsig[:, co:co + 1])
        o_ref[0, r0:r0 + tq, p * LANES:(p + 1) * LANES] = out.astype(o_ref.dtype)


def _banded_kernel(*refs, n_chunks, window, has_sink, gate_col, layer):
    refs = list(refs)
    if has_sink:
        sink_ref, slope_ref = refs[0], refs[1]
        refs = refs[2:]
    q_ref, k_ref, v_ref, qf_ref, kf_ref = refs[:5]
    refs = refs[5:]
    gate_ref = None
    if gate_col is not None:
        gate_ref = refs[0]
        refs = refs[1:]
    o_ref, kaug, vt, qaug = refs
    g = pl.program_id(1)
    tq = Q_SUB
    rows = PAIRS * tq
    ch = BAND_CHUNK
    n_pad = n_chunks - 1

    @pl.when(pl.program_id(2) == 0)
    def _():
        _build_keys(g, lambda s0: k_ref[0, pl.ds(s0, ch), :].astype(F32),
                    lambda s0: v_ref[0, pl.ds(s0, ch), :].astype(F32),
                    k_ref.shape[1] // ch, kf_ref, kaug, vt, n_pad, ch)

    krow = lax.broadcasted_iota(jnp.int32, (ch, tq), 0)
    qcol = lax.broadcasted_iota(jnp.int32, (ch, tq), 1)
    vrow = lax.broadcasted_iota(jnp.int32, (V_ROWS, rows), 0)
    for sub in range(q_ref.shape[1] // tq):
        i = pl.program_id(2) * (q_ref.shape[1] // tq) + sub
        _fill_queries(qaug.at[sub], q_ref, sub * tq, tq,
                      [jnp.broadcast_to(qf_ref[0, p:p + 1, :], (tq, LANES)) for p in range(PAIRS)])
        kc = kaug[pl.ds(i, n_chunks)].reshape(n_chunks * 2 * ch, 2 * LANES)
        st = _dot_nt(kc, qaug[sub])

        t = i * tq + qcol
        ev, od = [], []
        for u in range(n_chunks):
            xe, xo = st[(2 * u) * ch:(2 * u + 1) * ch], st[(2 * u + 1) * ch:(2 * u + 2) * ch]
            if u == 0 or u == n_chunks - 1:
                spos = (i - n_pad + u) * ch + krow
                dist = t - spos
                vis = jnp.where(dist >= 0, jnp.where(dist < window, jnp.where(spos >= 0, 1.0, 0.0), 0.0), 0.0)
                bias = _tile_pairs(jnp.where(vis > 0.5, 0.0, NEG))
                xe, xo = xe + bias, xo + bias
            ev.append(xe)
            od.append(xo)

        tpos = (i * tq + lax.broadcasted_iota(jnp.int32, (1, tq), 1)).astype(F32)
        vmat = jnp.concatenate([vt[i + u] for u in range(n_chunks)], axis=1)
        accs = []
        for h, slabs in enumerate((ev, od)):
            m = slabs[0].max(axis=0, keepdims=True)
            for x in slabs[1:]:
                m = jnp.maximum(m, x.max(axis=0, keepdims=True))
            if has_sink:
                sk = jnp.concatenate(
                    [sink_ref[layer, g * HEADS_PER_GROUP + 2 * p + h] * LOG2E
                     + slope_ref[g * HEADS_PER_GROUP + 2 * p + h] * tpos for p in range(PAIRS)], axis=1)
                m = jnp.maximum(m, sk)
            pmat = jnp.concatenate([jnp.exp2(x - m).astype(vt.dtype) for x in slabs], axis=0)
            acc = _dot(vmat, pmat)
            if has_sink:
                acc = acc + jnp.where(vrow >= HEAD_DIM, jnp.exp2(sk - m), 0.0)
            accs.append(acc)
        _finish(o_ref, sub * tq, tq, accs[0], accs[1], gate_ref, gate_col)


def _banded_attention(q, kv, k_col, v_col, qfeat, kfeat, window, sinks=None, slopes=None,
                      gates=None, gate_col=None, layer=0):
    b, s, _ = q.shape
    tq = Q_BLOCK
    n_sub = tq // Q_SUB
    n_chunks = window // BAND_CHUNK + 1
    n_tot = kfeat.shape[0]
    has_sink = sinks is not None
    kern = functools.partial(_banded_kernel, n_chunks=n_chunks, window=window,
                             has_sink=has_sink, gate_col=gate_col, layer=layer)
    in_specs, args = [], []
    if has_sink:
        in_specs += [pl.BlockSpec(memory_space=pltpu.SMEM)] * 2
        args += [sinks, slopes]
    in_specs += [pl.BlockSpec((1, tq, PAIRS * LANES), lambda bi, g, i: (bi, i, g)),
                 pl.BlockSpec((1, s, LANES), lambda bi, g, i: (bi, 0, k_col)),
                 pl.BlockSpec((1, s, LANES), lambda bi, g, i: (bi, 0, v_col)),
                 pl.BlockSpec((1, PAIRS, LANES), lambda bi, g, i: (g, 0, 0)),
                 _const_spec(kfeat.shape)]
    args += [q, kv, kv, qfeat, kfeat]
    if gate_col is not None:
        in_specs.append(pl.BlockSpec((1, tq, LANES), lambda bi, g, i: (bi, i, g)))
        args.append(gates)
    return pl.pallas_call(
        kern,
        grid=(b, N_KV_GROUPS, s // tq),
        in_specs=in_specs,
        out_specs=pl.BlockSpec((1, tq, PAIRS * LANES), lambda bi, g, i: (bi, i, g)),
        out_shape=jax.ShapeDtypeStruct(q.shape, MXU_DT),
        scratch_shapes=[pltpu.VMEM((n_tot, 2, BAND_CHUNK, 2 * LANES), MXU_DT),
                        pltpu.VMEM((n_tot, V_ROWS, BAND_CHUNK), MXU_DT),
                        pltpu.VMEM((n_sub, PAIRS * Q_SUB, 2 * LANES), MXU_DT)],
        compiler_params=_params(("arbitrary", "arbitrary", "arbitrary")),
        name="banded_attention_w%d" % window,
    )(*args)


def _cmpsel_kernel(q_ref, k_ref, v_ref, qf_ref, kf_ref, ovt_ref, gate_ref, o_ref, mf_ref,
                   kaug, vt, qaug):
    g = pl.program_id(1)
    ncp = k_ref.shape[2]

    @pl.when(pl.program_id(2) == 0)
    def _():
        _build_keys(g, lambda s0: k_ref[0, 0], lambda s0: v_ref[0, 0], 1, kf_ref, kaug, vt, 0, ncp)

    n_sub = q_ref.shape[1] // Q_SUB
    for sub in range(n_sub):
        _cmpsel_block(pl.program_id(2) * n_sub + sub, sub, ncp, q_ref, qf_ref, ovt_ref, gate_ref,
                      o_ref, mf_ref, kaug, vt, qaug)


def _cmpsel_block(i, sub, ncp, q_ref, qf_ref, ovt_ref, gate_ref, o_ref, mf_ref, kaug, vt, qaug):
    tq = Q_SUB
    n_slc = NSA_SLC_LEN
    _fill_queries(qaug.at[sub], q_ref, sub * tq, tq,
                  [jnp.broadcast_to(qf_ref[0, p:p + 1, :], (tq, LANES)) for p in range(PAIRS)])
    st = _dot_nt(kaug[0].reshape(2 * ncp, 2 * LANES), qaug[sub])

    krow = lax.broadcasted_iota(jnp.int32, (ncp, tq), 0)
    qcol = lax.broadcasted_iota(jnp.int32, (ncp, tq), 1)
    t = i * tq + qcol
    cmp_end = krow * NSA_CMP_STRIDE + (NSA_CMP_LEN - 1)
    vis = jnp.where(t >= cmp_end, jnp.where(krow < ncp - 1, 1.0, 0.0), 0.0)
    bias = _tile_pairs(jnp.where(vis > 0.5, 0.0, NEG))

    pn = []
    for h in range(2):
        x = st[h * ncp:(h + 1) * ncp] + bias
        m = x.max(axis=0, keepdims=True)
        m = jnp.where(m > 0.5 * NEG, m, 0.0)
        e = jnp.exp2(x - m)
        d = e.sum(axis=0, keepdims=True)
        pn.append(e * (1.0 / jnp.where(d > 0.0, d, 1.0)))
    _finish(o_ref, sub * tq, tq, _dot(vt[0], pn[0].astype(vt.dtype)), _dot(vt[0], pn[1].astype(vt.dtype)),
            gate_ref, 0, normalized=True)

    ps = pn[0] + pn[1]
    psum = ps[:, 0:tq]
    for p in range(1, PAIRS):
        psum = psum + ps[:, p * tq:(p + 1) * tq]
    p_hi = psum.astype(MXU_DT)
    p_lo = (psum - p_hi.astype(F32)).astype(MXU_DT)
    imp = (_dot(ovt_ref[...], p_hi) + _dot(ovt_ref[...], p_lo))[0:n_slc]
    blk = lax.broadcasted_iota(jnp.int32, (n_slc, tq), 0)
    tq_pos = i * tq + lax.broadcasted_iota(jnp.int32, (n_slc, tq), 1)
    cur = lax.shift_right_logical(tq_pos, int(math.log2(NSA_SLC_LEN)))
    forced = jnp.where(blk == 0, 1.0, jnp.where(blk == cur, 1.0, jnp.where(blk == cur - 1, 1.0, 0.0)))
    valid = blk <= cur
    score = jnp.where(valid, jnp.where(forced > 0.5, jnp.inf, imp), -jnp.inf)
    groups = [score[8 * v:8 * v + 8] for v in range(n_slc // 8)]
    ranks = [jnp.zeros((8, tq), F32) for _ in groups]
    sublane = lax.broadcasted_iota(jnp.int32, (8, tq), 0)
    for j in range(n_slc):
        rj = score[j:j + 1, :]
        for v in range(n_slc // 8):
            if 8 * v > j:
                beats = jnp.where(rj >= groups[v], 1.0, 0.0)
            elif 8 * v + 7 < j:
                beats = jnp.where(rj > groups[v], 1.0, 0.0)
            else:
                beats = jnp.where(sublane > (j - 8 * v), jnp.where(rj >= groups[v], 1.0, 0.0),
                                  jnp.where(rj > groups[v], 1.0, 0.0))
            ranks[v] = ranks[v] + beats
    rank = jnp.concatenate(ranks, axis=0)
    keep = jnp.where(valid, jnp.where(rank < float(NSA_TOPK), 1.0, 0.0), 0.0)
    feat = jnp.where(keep > 0.5, 0.0, NEG)
    feat = jnp.concatenate([feat, jnp.zeros((LANES - n_slc, tq), F32)], axis=0)
    mf_ref[0, 0, sub * tq:(sub + 1) * tq, :] = feat.T.astype(mf_ref.dtype)


def _compressed_and_select(q, kvcmp, qfeat, kfeat, ovt, gates):
    b, s, _ = q.shape
    tq = Q_BLOCK
    ncp = kvcmp.shape[2]
    return pl.pallas_call(
        _cmpsel_kernel,
        grid=(b, N_KV_GROUPS, s // tq),
        in_specs=[pl.BlockSpec((1, tq, PAIRS * LANES), lambda bi, g, i: (bi, i, g)),
                  pl.BlockSpec((1, 1, ncp, LANES), lambda bi, g, i: (bi, 0, 0, 0)),
                  pl.BlockSpec((1, 1, ncp, LANES), lambda bi, g, i: (bi, 1, 0, 0)),
                  pl.BlockSpec((1, PAIRS, LANES), lambda bi, g, i: (g, 0, 0)),
                  _const_spec(kfeat.shape), _const_spec(ovt.shape),
                  pl.BlockSpec((1, tq, LANES), lambda bi, g, i: (bi, i, g))],
        out_specs=[pl.BlockSpec((1, tq, PAIRS * LANES), lambda bi, g, i: (bi, i, g)),
                   pl.BlockSpec((1, 1, tq, LANES), lambda bi, g, i: (bi, g, i, 0))],
        out_shape=[jax.ShapeDtypeStruct(q.shape, MXU_DT),
                   jax.ShapeDtypeStruct((b, N_KV_GROUPS, s, LANES), MXU_DT)],
        scratch_shapes=[pltpu.VMEM((1, 2, ncp, 2 * LANES), MXU_DT),
                        pltpu.VMEM((1, V_ROWS, ncp), MXU_DT),
                        pltpu.VMEM((tq // Q_SUB, PAIRS * Q_SUB, 2 * LANES), MXU_DT)],
        compiler_params=_params(("arbitrary", "arbitrary", "arbitrary")),
        name="nsa_compressed_select",
    )(q, kvcmp, kvcmp, qfeat, kfeat, ovt, gates)


def _selected_kernel(q_ref, k_ref, v_ref, qf_ref, mf_ref, kf_ref, gate_ref, o_ref,
                     kaug, vt, qaug, acc_e, acc_o, st_a, st_b):
    g = pl.program_id(1)
    i = pl.program_id(2)
    tq = q_ref.shape[1]
    ch = SLC_CHUNK
    rows = PAIRS * tq

    @pl.when(i == 0)
    def _():
        _build_keys(g, lambda s0: k_ref[0, pl.ds(s0, ch), :].astype(F32),
                    lambda s0: v_ref[0, pl.ds(s0, ch), :].astype(F32),
                    k_ref.shape[1] // ch, kf_ref, kaug, vt, 0, ch)

    mask_feat = mf_ref[0, 0].astype(F32)
    _fill_queries(qaug, q_ref, 0, tq, [mask_feat + qf_ref[0, p:p + 1, :] for p in range(PAIRS)])
    acc_e[...] = jnp.zeros((V_ROWS, rows), F32)
    acc_o[...] = jnp.zeros((V_ROWS, rows), F32)

    def update(x, m_old, acc_ref, vt_c):
        m_new = jnp.maximum(m_old, x.max(axis=0, keepdims=True))
        alpha = jnp.exp2(m_old - m_new)
        p = jnp.exp2(x - m_new).astype(vt_c.dtype)
        acc_ref[...] = alpha * acc_ref[...] + _dot(vt_c, p)
        return m_new

    def scores_into(buf, c):
        buf[...] = _dot_nt(kaug[c].reshape(2 * ch, 2 * LANES), qaug[...])

    def consume(buf, c, ms, causal):
        xe, xo = buf[0:ch, :], buf[ch:2 * ch, :]
        if causal:
            krow = lax.broadcasted_iota(jnp.int32, (ch, tq), 0)
            qcol = lax.broadcasted_iota(jnp.int32, (ch, tq), 1)
            bias = _tile_pairs(jnp.where(c * ch + krow <= i * tq + qcol, 0.0, NEG))
            xe, xo = xe + bias, xo + bias
        return update(xe, ms[0], acc_e, vt[c]), update(xo, ms[1], acc_o, vt[c])

    def pair(k, ms):
        c = 2 * k
        scores_into(st_b, c + 1)
        ms = consume(st_a, c, ms, False)
        scores_into(st_a, c + 2)
        return consume(st_b, c + 1, ms, False)

    assert tq == ch
    m0 = jnp.full((1, rows), M_INIT, F32)
    scores_into(st_a, 0)
    ms = lax.fori_loop(0, i // 2, pair, (m0, m0))

    @pl.when(i % 2 == 0)
    def _():
        consume(st_a, i, ms, True)

    @pl.when(i % 2 == 1)
    def _():
        scores_into(st_b, i)
        consume(st_b, i, consume(st_a, i - 1, ms, False), True)

    _finish(o_ref, 0, tq, acc_e[...], acc_o[...], gate_ref, 1)


def _selected_attention(q, kv, k_col, v_col, qfeat, maskfeat, kfeat, gates):
    b, s, _ = q.shape
    tq = Q_BLOCK
    n_tot = kfeat.shape[0]
    return pl.pallas_call(
        _selected_kernel,
        grid=(b, N_KV_GROUPS, s // tq),
        in_specs=[pl.BlockSpec((1, tq, PAIRS * LANES), lambda bi, g, i: (bi, i, g)),
                  pl.BlockSpec((1, s, LANES), lambda bi, g, i: (bi, 0, k_col)),
                  pl.BlockSpec((1, s, LANES), lambda bi, g, i: (bi, 0, v_col)),
                  pl.BlockSpec((1, PAIRS, LANES), lambda bi, g, i: (g, 0, 0)),
                  pl.BlockSpec((1, 1, tq, LANES), lambda bi, g, i: (bi, g, i, 0)),
                  _const_spec(kfeat.shape),
                  pl.BlockSpec((1, tq, LANES), lambda bi, g, i: (bi, i, g))],
        out_specs=pl.BlockSpec((1, tq, PAIRS * LANES), lambda bi, g, i: (bi, i, g)),
        out_shape=jax.ShapeDtypeStruct(q.shape, MXU_DT),
        scratch_shapes=[pltpu.VMEM((n_tot, 2, SLC_CHUNK, 2 * LANES), MXU_DT),
                        pltpu.VMEM((n_tot, V_ROWS, SLC_CHUNK), MXU_DT),
                        pltpu.VMEM((PAIRS * tq, 2 * LANES), MXU_DT),
                        pltpu.VMEM((V_ROWS, PAIRS * tq), F32),
                        pltpu.VMEM((V_ROWS, PAIRS * tq), F32),
                        pltpu.VMEM((2 * SLC_CHUNK, PAIRS * tq), F32),
                        pltpu.VMEM((2 * SLC_CHUNK, PAIRS * tq), F32)],
        compiler_params=_params(("arbitrary", "arbitrary", "arbitrary")),
        name="nsa_selected",
    )(q, kv, kv, qfeat, maskfeat, kfeat, gates)


def _outproj_kernel(oa_ref, oc_ref, os_ref, ow_ref, x_ref, w_ref, b_ref, g_ref, beta_ref,
                    wr_ref, br_ref, x1_ref, route_ref, *, alpha):
    half = oa_ref.shape[1]
    on = (oc_ref[...].astype(F32) + os_ref[...].astype(F32) + ow_ref[...].astype(F32)).astype(MXU_DT)
    mix = _dot(oa_ref[...], w_ref[0:half]) + _dot(on, w_ref[half:]) + b_ref[...]
    x1 = _layer_norm(alpha * x_ref[...] + mix, g_ref[...], beta_ref[...])
    x1_ref[...] = x1

    x_hi = x1.astype(MXU_DT)
    x_lo = (x1 - x_hi.astype(F32)).astype(MXU_DT)
    logits = (_dot(x_hi, wr_ref[0]) + _dot(x_hi, wr_ref[1]) + _dot(x_lo, wr_ref[0])) + br_ref[...]
    lane = lax.broadcasted_iota(jnp.int32, logits.shape, 1)
    lanef = lane.astype(F32)
    big = float(4 * LANES)
    gl = jnp.where(lane < N_GROUPS, logits, -jnp.inf)
    gmax = gl.max(axis=1, keepdims=True)
    gsel = jnp.where(gl == gmax, lanef, big).min(axis=1, keepdims=True)
    g_w = 1.0 / jnp.exp(gl - gmax).sum(axis=1, keepdims=True)
    lo = N_GROUPS + gsel * EXPERTS_PER_GROUP
    el = jnp.where(lanef >= lo, jnp.where(lanef < lo + EXPERTS_PER_GROUP, logits, -jnp.inf), -jnp.inf)
    v1 = el.max(axis=1, keepdims=True)
    i1 = jnp.where(el == v1, lanef, big).min(axis=1, keepdims=True)
    el2 = jnp.where(lanef == i1, -jnp.inf, el)
    v2 = el2.max(axis=1, keepdims=True)
    i2 = jnp.where(el2 == v2, lanef, big).min(axis=1, keepdims=True)
    e2 = jnp.exp(v2 - v1)
    w1 = g_w / (1.0 + e2)
    w2 = g_w * e2 / (1.0 + e2)
    route_ref[...] = jnp.where(lane == 0, i1 - N_GROUPS,
                               jnp.where(lane == 1, i2 - N_GROUPS,
                                         jnp.where(lane == 2, w1, jnp.where(lane == 3, w2, 0.0))))


def _output_projection(oa, oc, os_, ow, x2d, w, b, ln_g, ln_b, wr, br, alpha, layer):
    n, d = x2d.shape
    half = oa.shape[1]
    tm = min(PROJ_TILE, n)
    row = lambda i: (i, 0)
    return pl.pallas_call(
        functools.partial(_outproj_kernel, alpha=alpha),
        grid=(n // tm,),
        in_specs=[pl.BlockSpec((tm, half), row)] * 4 + [pl.BlockSpec((tm, d), row)]
        + [_layer_spec(a, layer) for a in (w, b, ln_g, ln_b, wr, br)],
        out_specs=[pl.BlockSpec((tm, d), row), pl.BlockSpec((tm, LANES), row)],
        out_shape=[jax.ShapeDtypeStruct((n, d), F32), jax.ShapeDtypeStruct((n, LANES), F32)],
        compiler_params=_params(("arbitrary",)),
        name="output_projection_ln_route",
    )(oa, oc, os_, ow, x2d, w, b, ln_g, ln_b, wr, br)


def _row_copy(src_hbm, src_row, dst, j, u, sem):
    return pltpu.make_async_copy(src_hbm.at[pl.ds(src_row, 1)], dst.at[j, pl.ds(u, 1)], sem)


def _rows_start(src_hbm, idx_ref, base, dst, sem, n_groups):
    def body(j, carry):
        for u in range(SUBLANES):
            _row_copy(src_hbm, idx_ref[base + j * SUBLANES + u], dst, j, u, sem).start(priority=u % 2)
        return carry
    lax.fori_loop(0, n_groups, body, 0)


def _rows_wait(src_hbm, dst, sem, n_groups):
    def body(j, carry):
        for u in range(SUBLANES):
            _row_copy(src_hbm, 0, dst, j, u, sem).wait()
        return carry
    lax.fori_loop(0, n_groups, body, 0)


def _expert_kernel(blk_e_ref, n_used_ref, first_ref, wslot_ref, next_e_ref, row_tok_ref,
                   x_hbm, wg_hbm, wu_hbm, wd_hbm, o_ref,
                   xbuf, xsem, wg_f, wu_f, wd_f, wsem, wg_b, wu_b, wd_b, *, layer):
    i = pl.program_id(0)
    n_used = n_used_ref[0]
    groups, d = xbuf.shape[1], xbuf.shape[3]
    t = groups * SUBLANES
    hid = wg_b.shape[1]
    used = i < n_used
    slot = lax.rem(i, 2)

    def weight_copies(e, s):
        return [pltpu.make_async_copy(w.at[layer, e], buf.at[s], wsem.at[s])
                for w, buf in ((wg_hbm, wg_f), (wu_hbm, wu_f), (wd_hbm, wd_f))]

    def rows_start(blk, s, group_range):
        for j in group_range:
            for u in range(SUBLANES):
                _row_copy(x_hbm, row_tok_ref[blk * t + j * SUBLANES + u], xbuf.at[s], j, u,
                          xsem.at[s]).start()

    def rows_wait(s):
        for j in range(groups):
            for u in range(SUBLANES):
                _row_copy(x_hbm, 0, xbuf.at[s], j, u, xsem.at[s]).wait()

    @pl.when(i == 0)
    def _():
        for c in weight_copies(blk_e_ref[0], 0):
            c.start(priority=1)
        rows_start(0, 0, range(groups))

    @pl.when(used)
    def _():
        rows_wait(slot)
        ws = wslot_ref[i]

        @pl.when(first_ref[i] == 1)
        def _():
            for c in weight_copies(0, ws):
                c.wait()

            @pl.when(next_e_ref[i] >= 0)
            def _():
                for c in weight_copies(next_e_ref[i], 1 - ws):
                    c.start(priority=1)

            wg_b[...] = wg_f[ws].astype(wg_b.dtype)
            wu_b[...] = wu_f[ws].astype(wu_b.dtype)
            wd_b[...] = wd_f[ws].astype(wd_b.dtype)

        nxt = jnp.minimum(i + 1, n_used - 1)
        xb = xbuf[slot].reshape(t, d).astype(MXU_DT)
        pieces = 2
        y = None
        for piece in range(pieces):
            cols = slice(piece * hid // pieces, (piece + 1) * hid // pieces)
            gate = _dot(xb, wg_b[:, cols])
            up = _dot(xb, wu_b[:, cols])
            rows_start(nxt, 1 - slot, range(piece * groups // pieces, (piece + 1) * groups // pieces))
            hidden = (gate * jax.nn.sigmoid(gate) * up).astype(MXU_DT)
            part = _dot(hidden, wd_b[cols, :])
            y = part if y is None else y + part
        o_ref[...] = y

        @pl.when(i == n_used - 1)
        def _():
            rows_wait(1 - slot)

    @pl.when(jnp.logical_not(used))
    def _():
        o_ref[...] = jnp.zeros(o_ref.shape, o_ref.dtype)


def _expert_ffn(x1, row_tok, blk_e, n_used, first, wslot, next_e, wg, wu, wd, layer):
    d = x1.shape[1]
    p = row_tok.shape[0]
    t = EXPERT_ROWS
    hid = wg.shape[3]
    any_spec = pl.BlockSpec(memory_space=pl.ANY)
    return pl.pallas_call(
        functools.partial(_expert_kernel, layer=layer),
        grid_spec=pltpu.PrefetchScalarGridSpec(
            num_scalar_prefetch=6,
            grid=(p // t,),
            in_specs=[any_spec, any_spec, any_spec, any_spec],
            out_specs=pl.BlockSpec((t, d), lambda i, *_: (i, 0)),
            scratch_shapes=[pltpu.VMEM((2, t // SUBLANES, SUBLANES, d), F32), pltpu.SemaphoreType.DMA((2,)),
                            pltpu.VMEM((2, d, hid), F32), pltpu.VMEM((2, d, hid), F32),
                            pltpu.VMEM((2, hid, d), F32), pltpu.SemaphoreType.DMA((2,)),
                            pltpu.VMEM((d, hid), MXU_DT), pltpu.VMEM((d, hid), MXU_DT),
                            pltpu.VMEM((hid, d), MXU_DT)]),
        out_shape=jax.ShapeDtypeStruct((p, d), F32),
        compiler_params=_params(("arbitrary",)),
        name="expert_ffn",
    )(blk_e, n_used, first, wslot, next_e, row_tok, x1, wg, wu, wd)


def _combine_kernel(dest_ref, ys_hbm, route_ref, x_ref, g_ref, b_ref, o_ref, ybuf, sem, *, alpha, n_tok):
    i = pl.program_id(0)
    tm, d = x_ref.shape
    groups = tm // SUBLANES
    slot = lax.rem(i, 2)

    def start(blk, s):
        for k in range(2):
            _rows_start(ys_hbm, dest_ref, k * n_tok + blk * tm, ybuf.at[s, k], sem.at[s], groups)

    @pl.when(i == 0)
    def _():
        start(0, 0)

    @pl.when(i + 1 < pl.num_programs(0))
    def _():
        start(i + 1, 1 - slot)

    for k in range(2):
        _rows_wait(ys_hbm, ybuf.at[slot, k], sem.at[slot], groups)
    r = route_ref[...]
    y = r[:, 2:3] * ybuf[slot, 0].reshape(tm, d) + r[:, 3:4] * ybuf[slot, 1].reshape(tm, d)
    o_ref[...] = _layer_norm(alpha * x_ref[...] + y, g_ref[...], b_ref[...])


def _combine(ys, dest_kmajor, route, x1, ln_g, ln_b, alpha, layer):
    n, d = x1.shape
    tm = min(ROW_TILE, n)
    row = lambda i, dst: (i, 0)
    return pl.pallas_call(
        functools.partial(_combine_kernel, alpha=alpha, n_tok=n),
        grid_spec=pltpu.PrefetchScalarGridSpec(
            num_scalar_prefetch=1,
            grid=(n // tm,),
            in_specs=[pl.BlockSpec(memory_space=pl.ANY),
                      pl.BlockSpec((tm, LANES), row), pl.BlockSpec((tm, d), row),
                      _layer_spec(ln_g, layer), _layer_spec(ln_b, layer)],
            out_specs=pl.BlockSpec((tm, d), row),
            scratch_shapes=[pltpu.VMEM((2, 2, tm // SUBLANES, SUBLANES, d), F32),
                            pltpu.SemaphoreType.DMA((2,))]),
        out_shape=jax.ShapeDtypeStruct((n, d), F32),
        compiler_params=_params(("arbitrary",)),
        name="combine_ln",
    )(dest_kmajor, ys, route, x1, ln_g, ln_b)


def _dispatch_plan(route, n_tokens):
    t = EXPERT_ROWS
    a = 2 * n_tokens
    eid = route[:, 0:2].astype(jnp.int32).reshape(a)
    onehot = (eid[:, None] == jnp.arange(N_EXPERTS, dtype=jnp.int32)[None, :]).astype(jnp.int32)
    counts = onehot.sum(axis=0)
    before = jnp.cumsum(onehot, axis=0) - onehot
    rank = jnp.take_along_axis(before, eid[:, None], axis=1)[:, 0]
    padded = (counts + t - 1) // t * t
    pend = jnp.cumsum(padded)
    pstart = pend - padded
    dest = (pstart[eid] + rank).astype(jnp.int32)
    p_rows = a + N_EXPERTS * t
    filler = jnp.arange(p_rows, dtype=jnp.int32) % n_tokens
    row_tok = filler.at[dest].set(jnp.arange(a, dtype=jnp.int32) // 2)
    blk_start = jnp.arange(p_rows // t, dtype=jnp.int32) * t
    blk_e = jnp.minimum((pend[None, :] <= blk_start[:, None]).astype(jnp.int32).sum(axis=1), N_EXPERTS - 1)
    blk_e = blk_e.astype(jnp.int32)
    n_used = (pend[-1] // t).astype(jnp.int32)
    blk = jnp.arange(p_rows // t, dtype=jnp.int32)
    first = jnp.concatenate([jnp.ones((1,), jnp.int32), (blk_e[1:] != blk_e[:-1]).astype(jnp.int32)])
    wslot = (jnp.cumsum(first) - 1) % 2
    is_used = blk < n_used
    n_le = ((blk_e[None, :] <= blk_e[:, None]) & is_used[None, :]).astype(jnp.int32).sum(axis=1)
    next_e = jnp.where(n_le < n_used, blk_e[jnp.minimum(n_le, p_rows // t - 1)], -1)
    dest_kmajor = dest.reshape(n_tokens, 2).T.reshape(a)
    return (dest_kmajor, row_tok, blk_e, n_used.reshape(1), first, wslot.astype(jnp.int32),
            next_e.astype(jnp.int32))


def kernel(x, w_in, b_in, swa_sinks, cmp_pe_k, cmp_w1_k, cmp_w2_k, cmp_pe_v, cmp_w1_v, cmp_w2_v,
           w_out, b_out, ln1_g, ln1_b, w_group, b_group, w_expert, b_expert, we_gate, we_up,
           we_down, ln2_g, ln2_b):
    bsz, seq, d_model = x.shape
    depth = w_in.shape[0]
    n_tok = bsz * seq
    n_heads = d_model // HEAD_DIM
    q_w = n_heads // 2 * HEAD_DIM
    kv_w = N_KV_GROUPS * HEAD_DIM
    alpha = (2.0 * depth) ** 0.25
    scale = HEAD_DIM ** -0.5 * LOG2E
    assert q_w == N_KV_GROUPS * PAIRS * LANES and seq % SLC_CHUNK == 0

    slopes_a, slopes_n = _alibi_slopes(n_heads)
    slopes_a = (slopes_a * np.float32(LOG2E)).astype(np.float32)
    slopes_n = (slopes_n * np.float32(LOG2E)).astype(np.float32)
    qfeat_a = jnp.asarray(_query_feats(slopes_a), F32)
    qfeat_n = jnp.asarray(_query_feats(slopes_n), F32)
    pos = np.arange(seq)
    kfeat_swa = jnp.asarray(_key_feats(pos, SWA_WINDOW // BAND_CHUNK, BAND_CHUNK, False), MXU_DT)
    kfeat_win = jnp.asarray(_key_feats(pos, NSA_WINDOW // BAND_CHUNK, BAND_CHUNK, False), MXU_DT)
    kfeat_slc = jnp.asarray(_key_feats(pos, 0, SLC_CHUNK, True), MXU_DT)
    n_cmp = (seq - NSA_CMP_LEN) // NSA_CMP_STRIDE + 1
    n_cmp_pad = seq // NSA_CMP_STRIDE
    cmp_end = np.arange(n_cmp_pad) * NSA_CMP_STRIDE + NSA_CMP_LEN - 1
    kfeat_cmp = jnp.asarray(_key_feats(cmp_end, 0, n_cmp_pad, False), MXU_DT)
    n_slc = seq // NSA_SLC_LEN
    cs = np.arange(n_cmp_pad)[None, :] * NSA_CMP_STRIDE
    ss = np.arange(LANES)[:, None] * NSA_SLC_LEN
    ovt = ((cs < ss + NSA_SLC_LEN) & (cs + NSA_CMP_LEN - 1 >= ss)
           & (np.arange(n_cmp_pad)[None, :] < n_cmp) & (np.arange(LANES)[:, None] < n_slc))
    ovt = jnp.asarray(ovt.astype(np.float32), MXU_DT)
    slopes_a_s = jnp.asarray(slopes_a, F32)

    o_qa, o_ka = 0, q_w
    o_qn = q_w + 2 * kv_w
    o_kvn = o_qn + q_w
    o_gn = o_kvn + 6 * kv_w
    gate_w = 3 * HEADS_PER_GROUP

    def seg(a, lo, hi):
        return a[..., lo:hi]

    def gate_cols(a):
        pad = [(0, 0)] * (a.ndim - 1) + [(0, LANES - gate_w)]
        return jnp.concatenate([jnp.pad(seg(a, o_gn + g * gate_w, o_gn + (g + 1) * gate_w), pad)
                                for g in range(N_KV_GROUPS)], axis=-1)

    def regroup(a):
        return jnp.concatenate([seg(a, o_qa, o_qa + q_w) * scale, seg(a, o_qn, o_qn + q_w) * scale,
                                seg(a, o_ka, o_ka + 2 * kv_w), seg(a, o_kvn, o_kvn + 6 * kv_w),
                                gate_cols(a)], axis=-1)

    w_r = regroup(w_in).astype(MXU_DT)
    b_r = regroup(b_in)[:, None, :]
    flat = NSA_CMP_LEN * HEAD_DIM
    pe = jnp.broadcast_to(jnp.stack([cmp_pe_k, cmp_pe_v], axis=1).reshape(depth, 2, 1, flat),
                          (depth, 2, 8, flat))
    w1 = jnp.stack([cmp_w1_k, cmp_w1_v], axis=1).astype(MXU_DT)
    w2 = jnp.stack([cmp_w2_k, cmp_w2_v], axis=1).astype(MXU_DT)
    n_route_pad = LANES - N_GROUPS - N_EXPERTS
    wr = jnp.concatenate([w_group, w_expert, jnp.zeros((depth, d_model, n_route_pad), F32)], axis=2)
    br = jnp.concatenate([b_group, b_expert, jnp.zeros((depth, n_route_pad), F32)], axis=1)[:, None, :]
    wr_hi = wr.astype(MXU_DT)
    wr = jnp.stack([wr_hi, (wr - wr_hi.astype(F32)).astype(MXU_DT)], axis=1)
    w_o = w_out.astype(MXU_DT)
    b_o, g1, be1, g2, be2 = (a[:, None, :] for a in (b_out, ln1_g, ln1_b, ln2_g, ln2_b))

    h = x.reshape(n_tok, d_model)
    for l in range(depth):
        qa, qn, kv, gates = _input_projection(h, w_r, b_r, l)
        qa = qa.reshape(bsz, seq, q_w)
        qn = qn.reshape(bsz, seq, q_w)
        kv = kv.reshape(bsz, seq, 8 * kv_w)
        gates = gates.reshape(bsz, seq, 2 * LANES)

        o_a = _banded_attention(qa, kv, 0, 1, qfeat_a, kfeat_swa, SWA_WINDOW,
                                sinks=swa_sinks, slopes=slopes_a_s, layer=l)

        kvc = kv[:, :, 2 * kv_w:4 * kv_w].reshape(bsz, n_cmp_pad, NSA_CMP_STRIDE, 2, N_KV_GROUPS, HEAD_DIM)
        kvc = kvc.transpose(0, 3, 4, 1, 2, 5).reshape(bsz, 2, N_KV_GROUPS, n_cmp_pad, NSA_CMP_STRIDE * HEAD_DIM)
        kvcmp = _compress(kvc, pe, w1, w2, l)

        o_cmp, maskfeat = _compressed_and_select(qn, kvcmp, qfeat_n, kfeat_cmp, ovt, gates)
        o_slc = _selected_attention(qn, kv, 4, 5, qfeat_n, maskfeat, kfeat_slc, gates)
        o_win = _banded_attention(qn, kv, 6, 7, qfeat_n, kfeat_win, NSA_WINDOW,
                                  gates=gates, gate_col=2)

        x1, route = _output_projection(
            o_a.reshape(n_tok, q_w), o_cmp.reshape(n_tok, q_w), o_slc.reshape(n_tok, q_w),
            o_win.reshape(n_tok, q_w), h, w_o, b_o, g1, be1, wr, br, alpha, l)

        dest_kmajor, row_tok, blk_e, n_used, first, wslot, next_e = _dispatch_plan(route, n_tok)
        ys = _expert_ffn(x1, row_tok, blk_e, n_used, first, wslot, next_e, we_gate, we_up, we_down, l)
        h = _combine(ys, dest_kmajor, route, x1, g2, be2, alpha, l)
    return h.reshape(bsz, seq, d_model)
```

```python
import functools
import math

import numpy as np
import jax
import jax.numpy as jnp
from jax import lax
from jax.experimental import pallas as pl
from jax.experimental.pallas import tpu as pltpu

F32 = jnp.float32
MXU_DT = jnp.bfloat16

HEAD_DIM = 64
HEADS_PER_GROUP = 8
PAIRS = HEADS_PER_GROUP // 2
N_KV_GROUPS = 2
SWA_WINDOW = 128
NSA_WINDOW = 512
NSA_CMP_LEN = 32
NSA_CMP_STRIDE = 16
NSA_CMP_HIDDEN = 256
NSA_SLC_LEN = 64
NSA_TOPK = 16
N_GROUPS = 8
EXPERTS_PER_GROUP = 8
N_EXPERTS = N_GROUPS * EXPERTS_PER_GROUP
EXPERT_HIDDEN = 512
LN_EPS = 1e-5

LANES = 128
SUBLANES = 8
V_ROWS = 80
Q_BLOCK = 512
Q_SUB = 128
BAND_CHUNK = 128
SLC_CHUNK = 512
EXPERT_ROWS = 256
ROW_TILE = 256
PROJ_TILE = 512
LOG2E = 1.4426950408889634
NEG = -1e30
M_INIT = -5e29
VMEM_LIMIT = 56 * 1024 * 1024

FEAT_EVEN = 64
FEAT_ODD = 72


def _alibi_slopes(n_heads_total):
    n = n_heads_total
    s = np.exp2(-8.0 * np.arange(1, n + 1, dtype=np.float32) / np.float32(n)).astype(np.float32)
    return s[0::2], s[1::2]


def _bf16_parts(v):
    v = np.asarray(v, np.float32)
    a = v.astype(jnp.bfloat16).astype(np.float32)
    r = (v - a).astype(np.float32)
    b = r.astype(jnp.bfloat16).astype(np.float32)
    c = (r - b).astype(np.float32)
    return a, b, c


def _query_feats(slopes):
    out = np.zeros((N_KV_GROUPS, PAIRS, LANES), np.float32)
    a, b, c = _bf16_parts(slopes)
    for g in range(N_KV_GROUPS):
        for p in range(PAIRS):
            for h, base in ((0, FEAT_EVEN), (1, FEAT_ODD)):
                idx = g * HEADS_PER_GROUP + 2 * p + h
                out[g, p, base:base + 6] = [a[idx], b[idx], c[idx], a[idx], b[idx], c[idx]]
    return out


def _key_feats(positions, n_pad_chunks, chunk, with_blocks):
    pos = np.asarray(positions, np.int64)
    n = pos.shape[0]
    assert n % chunk == 0
    hi = (pos // 64 * 64).astype(np.float32)
    lo = (pos % 64).astype(np.float32)
    f = np.zeros((n, 2, LANES), np.float32)
    for v, base in ((0, FEAT_EVEN), (1, FEAT_ODD)):
        f[:, v, base:base + 3] = hi[:, None]
        f[:, v, base + 3:base + 6] = lo[:, None]
    if with_blocks:
        blk = pos // NSA_SLC_LEN
        for v in range(2):
            f[np.arange(n), v, blk] = 1.0
    f = f.reshape(n // chunk, chunk, 2, LANES).transpose(0, 2, 1, 3)
    if n_pad_chunks:
        f = np.concatenate([np.zeros((n_pad_chunks,) + f.shape[1:], np.float32), f], axis=0)
    return f


def _params(sem, vmem=VMEM_LIMIT):
    return pltpu.CompilerParams(dimension_semantics=sem, vmem_limit_bytes=vmem)


def _const_spec(shape):
    nd = len(shape)
    return pl.BlockSpec(shape, lambda *_: (0,) * nd, pipeline_mode=pl.Buffered(1))


def _layer_spec(arr, layer):
    nd = arr.ndim
    return pl.BlockSpec((None,) + arr.shape[1:], lambda *_: (layer,) + (0,) * (nd - 1),
                        pipeline_mode=pl.Buffered(1))


def _dot(a, b):
    return jnp.dot(a, b, preferred_element_type=F32)


def _dot_nt(a, b):
    return lax.dot_general(a, b, (((1,), (1,)), ((), ())), preferred_element_type=F32)


def _layer_norm(h, g, b):
    mu = jnp.mean(h, axis=-1, keepdims=True)
    d = h - mu
    var = jnp.mean(d * d, axis=-1, keepdims=True)
    return d * lax.rsqrt(var + LN_EPS) * g + b


def _inproj_kernel(x_ref, w_ref, b_ref, qa_ref, qn_ref, kv_ref, gt_ref):
    xb = x_ref[...].astype(MXU_DT)
    width = qa_ref.shape[1]
    for j, o_ref in enumerate((qa_ref, qn_ref, kv_ref)):
        acc = _dot(xb, w_ref[:, j * width:(j + 1) * width]) + b_ref[:, j * width:(j + 1) * width]
        o_ref[...] = acc.astype(o_ref.dtype)
    gt_ref[...] = _dot(xb, w_ref[:, 3 * width:]) + b_ref[:, 3 * width:]


def _input_projection(x2d, w, b, layer):
    n, d = x2d.shape
    width = (w.shape[2] - 2 * LANES) // 3
    tm = min(PROJ_TILE, n)
    row = lambda i: (i, 0)
    return pl.pallas_call(
        _inproj_kernel,
        grid=(n // tm,),
        in_specs=[pl.BlockSpec((tm, d), row), _layer_spec(w, layer), _layer_spec(b, layer)],
        out_specs=[pl.BlockSpec((tm, width), row)] * 3 + [pl.BlockSpec((tm, 2 * LANES), row)],
        out_shape=[jax.ShapeDtypeStruct((n, width), MXU_DT)] * 3
        + [jax.ShapeDtypeStruct((n, 2 * LANES), F32)],
        compiler_params=_params(("arbitrary",)),
        name="input_projection",
    )(x2d, w, b)


def _gelu_tanh(x):
    c = math.sqrt(2.0 / math.pi)
    return 0.5 * x * (1.0 + jnp.tanh(c * (x + 0.044715 * (x * x * x))))


def _compress_kernel(c_ref, pe_ref, w1_ref, w2_ref, o_ref):
    half = c_ref.shape[-1]
    w1 = w1_ref[0]
    pe = pe_ref[0]
    pe_hi = pe.astype(MXU_DT)
    pe_lo = (pe - pe_hi.astype(F32)).astype(MXU_DT)
    base = (_dot(pe_hi, w1) + _dot(pe_lo, w1))[0:1]
    for g in range(N_KV_GROUPS):
        c = c_ref[0, 0, g]
        top = _dot(c, w1[:half])
        bot = _dot(c, w1[half:])
        n_rows = bot.shape[0]
        hidden = top + pltpu.roll(bot, n_rows - 1, 0) + base
        act = _gelu_tanh(hidden)
        o_ref[0, 0, :, g * HEAD_DIM:(g + 1) * HEAD_DIM] = _dot(act.astype(MXU_DT), w2_ref[0])


def _compress(chunks, pe, w1, w2, layer):
    b, two, g, nch, half = chunks.shape
    return pl.pallas_call(
        _compress_kernel,
        grid=(b, two),
        in_specs=[pl.BlockSpec((1, 1, g, nch, half), lambda i, j: (i, j, 0, 0, 0)),
                  pl.BlockSpec((None, 1, 8, 2 * half), lambda i, j: (layer, j, 0, 0)),
                  pl.BlockSpec((None, 1, 2 * half, NSA_CMP_HIDDEN), lambda i, j: (layer, j, 0, 0)),
                  pl.BlockSpec((None, 1, NSA_CMP_HIDDEN, HEAD_DIM), lambda i, j: (layer, j, 0, 0))],
        out_specs=pl.BlockSpec((1, 1, nch, LANES), lambda i, j: (i, j, 0, 0)),
        out_shape=jax.ShapeDtypeStruct((b, two, nch, LANES), F32),
        compiler_params=_params(("arbitrary", "arbitrary")),
        name="nsa_compress",
    )(chunks, pe, w1, w2)


def _place_group(x, g_is_zero):
    lane = lax.broadcasted_iota(jnp.int32, x.shape, 1)
    rolled = pltpu.roll(x, HEAD_DIM, 1)
    lo_src, hi_src = (x, rolled) if g_is_zero else (rolled, x)
    lo = jnp.where(lane < HEAD_DIM, lo_src, 0.0)
    hi = jnp.where(lane >= HEAD_DIM, hi_src, 0.0)
    return lo, hi


def _build_keys(g, load_k, load_v, n_chunks, kf_ref, kaug, vt, n_pad, chunk):
    ones = jnp.ones((V_ROWS - HEAD_DIM, chunk), F32)
    for gval in range(N_KV_GROUPS):
        @pl.when(g == gval)
        def _():
            def body(c, carry):
                start = pl.multiple_of(c * chunk, chunk)
                klo, khi = _place_group(load_k(start), gval == 0)
                vlo, _ = _place_group(load_v(start), gval == 0)
                kaug[n_pad + c, 0, :, 0:LANES] = klo.astype(kaug.dtype)
                kaug[n_pad + c, 1, :, 0:LANES] = khi.astype(kaug.dtype)
                vt[n_pad + c] = jnp.concatenate([vlo.T[0:HEAD_DIM], ones], axis=0).astype(vt.dtype)
                return carry
            lax.fori_loop(0, n_chunks, body, 0)
    kaug[:, :, :, LANES:2 * LANES] = kf_ref[...]
    if n_pad:
        kaug[0:n_pad, :, :, 0:LANES] = jnp.zeros((n_pad, 2, chunk, LANES), kaug.dtype)
        vt[0:n_pad] = jnp.zeros((n_pad, V_ROWS, chunk), vt.dtype)


def _fill_queries(qaug, q_ref, r0, tq, feats):
    for p in range(PAIRS):
        qaug[p * tq:(p + 1) * tq, 0:LANES] = q_ref[0, r0:r0 + tq, p * LANES:(p + 1) * LANES]
        qaug[p * tq:(p + 1) * tq, LANES:2 * LANES] = feats[p].astype(qaug.dtype)


def _tile_pairs(x):
    return jnp.concatenate([x] * PAIRS, axis=1)


def _finish(o_ref, r0, tq, acct_e, acct_o, gate_ref, gate_col, normalized=False):
    lane = lax.broadcasted_iota(jnp.int32, (tq, LANES), 1)
    if gate_ref is not None:
        gsig = jax.nn.sigmoid(gate_ref[0, r0:r0 + tq, :])
    for p in range(PAIRS):
        cols = slice(p * tq, (p + 1) * tq)
        top = jnp.concatenate([acct_e[0:HEAD_DIM, cols], acct_o[0:HEAD_DIM, cols]], axis=0)
        if not normalized:
            den = jnp.concatenate(
                [jnp.broadcast_to(acct_e[HEAD_DIM:HEAD_DIM + 1, cols], (HEAD_DIM, tq)),
                 jnp.broadcast_to(acct_o[HEAD_DIM:HEAD_DIM + 1, cols], (HEAD_DIM, tq))], axis=0)
            top = top / den
        out = top.T
        if gate_ref is not None:
            ce = 3 * (2 * p) + gate_col
            co = 3 * (2 * p + 1) + gate_col
            out = out * jnp.where(lane < HEAD_DIM, gsig[:, ce:ce + 1], gsig[:, co:co + 1])
        o_ref[0, r0:r0 + tq, p * LANES:(p + 1) * LANES] = out.astype(o_ref.dtype)


def _banded_kernel(*refs, n_chunks, window, has_sink, gate_col, layer):
    refs = list(refs)
    if has_sink:
        sink_ref, slope_ref = refs[0], refs[1]
        refs = refs[2:]
    q_ref, k_ref, v_ref, qf_ref, kf_ref = refs[:5]
    refs = refs[5:]
    gate_ref = None
    if gate_col is not None:
        gate_ref = refs[0]
        refs = refs[1:]
    o_ref, kaug, vt, qaug = refs
    g = pl.program_id(1)
    tq = Q_SUB
    rows = PAIRS * tq
    ch = BAND_CHUNK
    n_pad = n_chunks - 1

    @pl.when(pl.program_id(2) == 0)
    def _():
        _build_keys(g, lambda s0: k_ref[0, pl.ds(s0, ch), :].astype(F32),
                    lambda s0: v_ref[0, pl.ds(s0, ch), :].astype(F32),
                    k_ref.shape[1] // ch, kf_ref, kaug, vt, n_pad, ch)

    krow = lax.broadcasted_iota(jnp.int32, (ch, tq), 0)
    qcol = lax.broadcasted_iota(jnp.int32, (ch, tq), 1)
    vrow = lax.broadcasted_iota(jnp.int32, (V_ROWS, rows), 0)
    for sub in range(q_ref.shape[1] // tq):
        i = pl.program_id(2) * (q_ref.shape[1] // tq) + sub
        _fill_queries(qaug.at[sub], q_ref, sub * tq, tq,
                      [jnp.broadcast_to(qf_ref[0, p:p + 1, :], (tq, LANES)) for p in range(PAIRS)])
        kc = kaug[pl.ds(i, n_chunks)].reshape(n_chunks * 2 * ch, 2 * LANES)
        st = _dot_nt(kc, qaug[sub])

        t = i * tq + qcol
        ev, od = [], []
        for u in range(n_chunks):
            spos = (i - n_pad + u) * ch + krow
            dist = t - spos
            vis = jnp.where(dist >= 0, jnp.where(dist < window, jnp.where(spos >= 0, 1.0, 0.0), 0.0), 0.0)
            bias = _tile_pairs(jnp.where(vis > 0.5, 0.0, NEG))
            ev.append(st[(2 * u) * ch:(2 * u + 1) * ch] + bias)
            od.append(st[(2 * u + 1) * ch:(2 * u + 2) * ch] + bias)

        tpos = (i * tq + lax.broadcasted_iota(jnp.int32, (1, tq), 1)).astype(F32)
        vmat = jnp.concatenate([vt[i + u] for u in range(n_chunks)], axis=1)
        accs = []
        for h, slabs in enumerate((ev, od)):
            m = slabs[0].max(axis=0, keepdims=True)
            for x in slabs[1:]:
                m = jnp.maximum(m, x.max(axis=0, keepdims=True))
            if has_sink:
                sk = jnp.concatenate(
                    [sink_ref[layer, g * HEADS_PER_GROUP + 2 * p + h] * LOG2E
                     + slope_ref[g * HEADS_PER_GROUP + 2 * p + h] * tpos for p in range(PAIRS)], axis=1)
                m = jnp.maximum(m, sk)
            pmat = jnp.concatenate([jnp.exp2(x - m).astype(vt.dtype) for x in slabs], axis=0)
            acc = _dot(vmat, pmat)
            if has_sink:
                acc = acc + jnp.where(vrow >= HEAD_DIM, jnp.exp2(sk - m), 0.0)
            accs.append(acc)
        _finish(o_ref, sub * tq, tq, accs[0], accs[1], gate_ref, gate_col)


def _banded_attention(q, kv, k_col, v_col, qfeat, kfeat, window, sinks=None, slopes=None,
                      gates=None, gate_col=None, layer=0):
    b, s, _ = q.shape
    tq = Q_BLOCK
    n_sub = tq // Q_SUB
    n_chunks = window // BAND_CHUNK + 1
    n_tot = kfeat.shape[0]
    has_sink = sinks is not None
    kern = functools.partial(_banded_kernel, n_chunks=n_chunks, window=window,
                             has_sink=has_sink, gate_col=gate_col, layer=layer)
    in_specs, args = [], []
    if has_sink:
        in_specs += [pl.BlockSpec(memory_space=pltpu.SMEM)] * 2
        args += [sinks, slopes]
    in_specs += [pl.BlockSpec((1, tq, PAIRS * LANES), lambda bi, g, i: (bi, i, g)),
                 pl.BlockSpec((1, s, LANES), lambda bi, g, i: (bi, 0, k_col)),
                 pl.BlockSpec((1, s, LANES), lambda bi, g, i: (bi, 0, v_col)),
                 pl.BlockSpec((1, PAIRS, LANES), lambda bi, g, i: (g, 0, 0)),
                 _const_spec(kfeat.shape)]
    args += [q, kv, kv, qfeat, kfeat]
    if gate_col is not None:
        in_specs.append(pl.BlockSpec((1, tq, LANES), lambda bi, g, i: (bi, i, g)))
        args.append(gates)
    return pl.pallas_call(
        kern,
        grid=(b, N_KV_GROUPS, s // tq),
        in_specs=in_specs,
        out_specs=pl.BlockSpec((1, tq, PAIRS * LANES), lambda bi, g, i: (bi, i, g)),
        out_shape=jax.ShapeDtypeStruct(q.shape, MXU_DT),
        scratch_shapes=[pltpu.VMEM((n_tot, 2, BAND_CHUNK, 2 * LANES), MXU_DT),
                        pltpu.VMEM((n_tot, V_ROWS, BAND_CHUNK), MXU_DT),
                        pltpu.VMEM((n_sub, PAIRS * Q_SUB, 2 * LANES), MXU_DT)],
        compiler_params=_params(("arbitrary", "arbitrary", "arbitrary")),
        name="banded_attention_w%d" % window,
    )(*args)


def _cmpsel_kernel(q_ref, k_ref, v_ref, qf_ref, kf_ref, ovt_ref, gate_ref, o_ref, mf_ref,
                   kaug, vt, qaug):
    g = pl.program_id(1)
    ncp = k_ref.shape[2]

    @pl.when(pl.program_id(2) == 0)
    def _():
        _build_keys(g, lambda s0: k_ref[0, 0], lambda s0: v_ref[0, 0], 1, kf_ref, kaug, vt, 0, ncp)

    n_sub = q_ref.shape[1] // Q_SUB
    for sub in range(n_sub):
        _cmpsel_block(pl.program_id(2) * n_sub + sub, sub, ncp, q_ref, qf_ref, ovt_ref, gate_ref,
                      o_ref, mf_ref, kaug, vt, qaug)


def _cmpsel_block(i, sub, ncp, q_ref, qf_ref, ovt_ref, gate_ref, o_ref, mf_ref, kaug, vt, qaug):
    tq = Q_SUB
    n_slc = NSA_SLC_LEN
    _fill_queries(qaug.at[sub], q_ref, sub * tq, tq,
                  [jnp.broadcast_to(qf_ref[0, p:p + 1, :], (tq, LANES)) for p in range(PAIRS)])
    st = _dot_nt(kaug[0].reshape(2 * ncp, 2 * LANES), qaug[sub])

    krow = lax.broadcasted_iota(jnp.int32, (ncp, tq), 0)
    qcol = lax.broadcasted_iota(jnp.int32, (ncp, tq), 1)
    t = i * tq + qcol
    cmp_end = krow * NSA_CMP_STRIDE + (NSA_CMP_LEN - 1)
    vis = jnp.where(t >= cmp_end, jnp.where(krow < ncp - 1, 1.0, 0.0), 0.0)
    bias = _tile_pairs(jnp.where(vis > 0.5, 0.0, NEG))

    pn = []
    for h in range(2):
        x = st[h * ncp:(h + 1) * ncp] + bias
        m = x.max(axis=0, keepdims=True)
        m = jnp.where(m > 0.5 * NEG, m, 0.0)
        e = jnp.exp2(x - m)
        d = e.sum(axis=0, keepdims=True)
        pn.append(e * (1.0 / jnp.where(d > 0.0, d, 1.0)))
    _finish(o_ref, sub * tq, tq, _dot(vt[0], pn[0].astype(vt.dtype)), _dot(vt[0], pn[1].astype(vt.dtype)),
            gate_ref, 0, normalized=True)

    ps = pn[0] + pn[1]
    psum = ps[:, 0:tq]
    for p in range(1, PAIRS):
        psum = psum + ps[:, p * tq:(p + 1) * tq]
    p_hi = psum.astype(MXU_DT)
    p_lo = (psum - p_hi.astype(F32)).astype(MXU_DT)
    imp = (_dot(ovt_ref[...], p_hi) + _dot(ovt_ref[...], p_lo))[0:n_slc]
    blk = lax.broadcasted_iota(jnp.int32, (n_slc, tq), 0)
    tq_pos = i * tq + lax.broadcasted_iota(jnp.int32, (n_slc, tq), 1)
    cur = lax.shift_right_logical(tq_pos, int(math.log2(NSA_SLC_LEN)))
    forced = jnp.where(blk == 0, 1.0, jnp.where(blk == cur, 1.0, jnp.where(blk == cur - 1, 1.0, 0.0)))
    valid = blk <= cur
    score = jnp.where(valid, jnp.where(forced > 0.5, jnp.inf, imp), -jnp.inf)
    groups = [score[8 * v:8 * v + 8] for v in range(n_slc // 8)]
    ranks = [jnp.zeros((8, tq), F32) for _ in groups]
    sublane = lax.broadcasted_iota(jnp.int32, (8, tq), 0)
    for j in range(n_slc):
        rj = score[j:j + 1, :]
        for v in range(n_slc // 8):
            if 8 * v > j:
                beats = jnp.where(rj >= groups[v], 1.0, 0.0)
            elif 8 * v + 7 < j:
                beats = jnp.where(rj > groups[v], 1.0, 0.0)
            else:
                beats = jnp.where(sublane > (j - 8 * v), jnp.where(rj >= groups[v], 1.0, 0.0),
                                  jnp.where(rj > groups[v], 1.0, 0.0))
            ranks[v] = ranks[v] + beats
    rank = jnp.concatenate(ranks, axis=0)
    keep = jnp.where(valid, jnp.where(rank < float(NSA_TOPK), 1.0, 0.0), 0.0)
    feat = jnp.where(keep > 0.5, 0.0, NEG)
    feat = jnp.concatenate([feat, jnp.zeros((LANES - n_slc, tq), F32)], axis=0)
    mf_ref[0, 0, sub * tq:(sub + 1) * tq, :] = feat.T.astype(mf_ref.dtype)


def _compressed_and_select(q, kvcmp, qfeat, kfeat, ovt, gates):
    b, s, _ = q.shape
    tq = Q_BLOCK
    ncp = kvcmp.shape[2]
    return pl.pallas_call(
        _cmpsel_kernel,
        grid=(b, N_KV_GROUPS, s // tq),
        in_specs=[pl.BlockSpec((1, tq, PAIRS * LANES), lambda bi, g, i: (bi, i, g)),
                  pl.BlockSpec((1, 1, ncp, LANES), lambda bi, g, i: (bi, 0, 0, 0)),
                  pl.BlockSpec((1, 1, ncp, LANES), lambda bi, g, i: (bi, 1, 0, 0)),
                  pl.BlockSpec((1, PAIRS, LANES), lambda bi, g, i: (g, 0, 0)),
                  _const_spec(kfeat.shape), _const_spec(ovt.shape),
                  pl.BlockSpec((1, tq, LANES), lambda bi, g, i: (bi, i, g))],
        out_specs=[pl.BlockSpec((1, tq, PAIRS * LANES), lambda bi, g, i: (bi, i, g)),
                   pl.BlockSpec((1, 1, tq, LANES), lambda bi, g, i: (bi, g, i, 0))],
        out_shape=[jax.ShapeDtypeStruct(q.shape, MXU_DT),
                   jax.ShapeDtypeStruct((b, N_KV_GROUPS, s, LANES), MXU_DT)],
        scratch_shapes=[pltpu.VMEM((1, 2, ncp, 2 * LANES), MXU_DT),
                        pltpu.VMEM((1, V_ROWS, ncp), MXU_DT),
                        pltpu.VMEM((tq // Q_SUB, PAIRS * Q_SUB, 2 * LANES), MXU_DT)],
        compiler_params=_params(("arbitrary", "arbitrary", "arbitrary")),
        name="nsa_compressed_select",
    )(q, kvcmp, kvcmp, qfeat, kfeat, ovt, gates)


def _selected_kernel(q_ref, k_ref, v_ref, qf_ref, mf_ref, kf_ref, gate_ref, o_ref,
                     kaug, vt, qaug, acc_e, acc_o, st_a, st_b):
    g = pl.program_id(1)
    i = pl.program_id(2)
    tq = q_ref.shape[1]
    ch = SLC_CHUNK
    rows = PAIRS * tq

    @pl.when(i == 0)
    def _():
        _build_keys(g, lambda s0: k_ref[0, pl.ds(s0, ch), :].astype(F32),
                    lambda s0: v_ref[0, pl.ds(s0, ch), :].astype(F32),
                    k_ref.shape[1] // ch, kf_ref, kaug, vt, 0, ch)

    mask_feat = mf_ref[0, 0].astype(F32)
    _fill_queries(qaug, q_ref, 0, tq, [mask_feat + qf_ref[0, p:p + 1, :] for p in range(PAIRS)])
    acc_e[...] = jnp.zeros((V_ROWS, rows), F32)
    acc_o[...] = jnp.zeros((V_ROWS, rows), F32)

    def update(x, m_old, acc_ref, vt_c):
        m_new = jnp.maximum(m_old, x.max(axis=0, keepdims=True))
        alpha = jnp.exp2(m_old - m_new)
        p = jnp.exp2(x - m_new).astype(vt_c.dtype)
        acc_ref[...] = alpha * acc_ref[...] + _dot(vt_c, p)
        return m_new

    def scores_into(buf, c):
        buf[...] = _dot_nt(kaug[c].reshape(2 * ch, 2 * LANES), qaug[...])

    def consume(buf, c, ms, causal):
        xe, xo = buf[0:ch, :], buf[ch:2 * ch, :]
        if causal:
            krow = lax.broadcasted_iota(jnp.int32, (ch, tq), 0)
            qcol = lax.broadcasted_iota(jnp.int32, (ch, tq), 1)
            bias = _tile_pairs(jnp.where(c * ch + krow <= i * tq + qcol, 0.0, NEG))
            xe, xo = xe + bias, xo + bias
        return update(xe, ms[0], acc_e, vt[c]), update(xo, ms[1], acc_o, vt[c])

    def pair(k, ms):
        c = 2 * k
        scores_into(st_b, c + 1)
        ms = consume(st_a, c, ms, False)
        scores_into(st_a, c + 2)
        return consume(st_b, c + 1, ms, False)

    assert tq == ch
    m0 = jnp.full((1, rows), M_INIT, F32)
    scores_into(st_a, 0)
    ms = lax.fori_loop(0, i // 2, pair, (m0, m0))

    @pl.when(i % 2 == 0)
    def _():
        consume(st_a, i, ms, True)

    @pl.when(i % 2 == 1)
    def _():
        scores_into(st_b, i)
        consume(st_b, i, consume(st_a, i - 1, ms, False), True)

    _finish(o_ref, 0, tq, acc_e[...], acc_o[...], gate_ref, 1)


def _selected_attention(q, kv, k_col, v_col, qfeat, maskfeat, kfeat, gates):
    b, s, _ = q.shape
    tq = Q_BLOCK
    n_tot = kfeat.shape[0]
    return pl.pallas_call(
        _selected_kernel,
        grid=(b, N_KV_GROUPS, s // tq),
        in_specs=[pl.BlockSpec((1, tq, PAIRS * LANES), lambda bi, g, i: (bi, i, g)),
                  pl.BlockSpec((1, s, LANES), lambda bi, g, i: (bi, 0, k_col)),
                  pl.BlockSpec((1, s, LANES), lambda bi, g, i: (bi, 0, v_col)),
                  pl.BlockSpec((1, PAIRS, LANES), lambda bi, g, i: (g, 0, 0)),
                  pl.BlockSpec((1, 1, tq, LANES), lambda bi, g, i: (bi, g, i, 0)),
                  _const_spec(kfeat.shape),
                  pl.BlockSpec((1, tq, LANES), lambda bi, g, i: (bi, i, g))],
        out_specs=pl.BlockSpec((1, tq, PAIRS * LANES), lambda bi, g, i: (bi, i, g)),
        out_shape=jax.ShapeDtypeStruct(q.shape, MXU_DT),
        scratch_shapes=[pltpu.VMEM((n_tot, 2, SLC_CHUNK, 2 * LANES), MXU_DT),
                        pltpu.VMEM((n_tot, V_ROWS, SLC_CHUNK), MXU_DT),
                        pltpu.VMEM((PAIRS * tq, 2 * LANES), MXU_DT),
                        pltpu.VMEM((V_ROWS, PAIRS * tq), F32),
                        pltpu.VMEM((V_ROWS, PAIRS * tq), F32),
                        pltpu.VMEM((2 * SLC_CHUNK, PAIRS * tq), F32),
                        pltpu.VMEM((2 * SLC_CHUNK, PAIRS * tq), F32)],
        compiler_params=_params(("arbitrary", "arbitrary", "arbitrary")),
        name="nsa_selected",
    )(q, kv, kv, qfeat, maskfeat, kfeat, gates)


def _outproj_kernel(oa_ref, oc_ref, os_ref, ow_ref, x_ref, w_ref, b_ref, g_ref, beta_ref,
                    wr_ref, br_ref, x1_ref, route_ref, *, alpha):
    half = oa_ref.shape[1]
    on = (oc_ref[...].astype(F32) + os_ref[...].astype(F32) + ow_ref[...].astype(F32)).astype(MXU_DT)
    mix = _dot(oa_ref[...], w_ref[0:half]) + _dot(on, w_ref[half:]) + b_ref[...]
    x1 = _layer_norm(alpha * x_ref[...] + mix, g_ref[...], beta_ref[...])
    x1_ref[...] = x1

    tm = x1.shape[0]
    x_hi = x1.astype(MXU_DT)
    x_lo = (x1 - x_hi.astype(F32)).astype(MXU_DT)
    parts = _dot(jnp.concatenate([x_hi, x_lo], axis=0), wr_ref[...])
    logits = (parts[:tm, :LANES] + parts[:tm, LANES:]) + (parts[tm:, :LANES] + parts[tm:, LANES:]) + br_ref[...]
    lane = lax.broadcasted_iota(jnp.int32, logits.shape, 1)
    lanef = lane.astype(F32)
    big = float(4 * LANES)
    gl = jnp.where(lane < N_GROUPS, logits, -jnp.inf)
    gmax = gl.max(axis=1, keepdims=True)
    gsel = jnp.where(gl == gmax, lanef, big).min(axis=1, keepdims=True)
    g_w = 1.0 / jnp.exp(gl - gmax).sum(axis=1, keepdims=True)
    lo = N_GROUPS + gsel * EXPERTS_PER_GROUP
    el = jnp.where(lanef >= lo, jnp.where(lanef < lo + EXPERTS_PER_GROUP, logits, -jnp.inf), -jnp.inf)
    v1 = el.max(axis=1, keepdims=True)
    i1 = jnp.where(el == v1, lanef, big).min(axis=1, keepdims=True)
    el2 = jnp.where(lanef == i1, -jnp.inf, el)
    v2 = el2.max(axis=1, keepdims=True)
    i2 = jnp.where(el2 == v2, lanef, big).min(axis=1, keepdims=True)
    e2 = jnp.exp(v2 - v1)
    w1 = g_w / (1.0 + e2)
    w2 = g_w * e2 / (1.0 + e2)
    route_ref[...] = jnp.where(lane == 0, i1 - N_GROUPS,
                               jnp.where(lane == 1, i2 - N_GROUPS,
                                         jnp.where(lane == 2, w1, jnp.where(lane == 3, w2, 0.0))))


def _output_projection(oa, oc, os_, ow, x2d, w, b, ln_g, ln_b, wr, br, alpha, layer):
    n, d = x2d.shape
    half = oa.shape[1]
    tm = min(PROJ_TILE, n)
    row = lambda i: (i, 0)
    return pl.pallas_call(
        functools.partial(_outproj_kernel, alpha=alpha),
        grid=(n // tm,),
        in_specs=[pl.BlockSpec((tm, half), row)] * 4 + [pl.BlockSpec((tm, d), row)]
        + [_layer_spec(a, layer) for a in (w, b, ln_g, ln_b, wr, br)],
        out_specs=[pl.BlockSpec((tm, d), row), pl.BlockSpec((tm, LANES), row)],
        out_shape=[jax.ShapeDtypeStruct((n, d), F32), jax.ShapeDtypeStruct((n, LANES), F32)],
        compiler_params=_params(("arbitrary",)),
        name="output_projection_ln_route",
    )(oa, oc, os_, ow, x2d, w, b, ln_g, ln_b, wr, br)


def _row_copy(src_hbm, src_row, dst, j, u, sem):
    return pltpu.make_async_copy(src_hbm.at[pl.ds(src_row, 1)], dst.at[j, pl.ds(u, 1)], sem)


def _rows_start(src_hbm, idx_ref, base, dst, sem, n_groups):
    def body(j, carry):
        for u in range(SUBLANES):
            _row_copy(src_hbm, idx_ref[base + j * SUBLANES + u], dst, j, u, sem).start(priority=u % 2)
        return carry
    lax.fori_loop(0, n_groups, body, 0)


def _rows_wait(src_hbm, dst, sem, n_groups):
    def body(j, carry):
        for u in range(SUBLANES):
            _row_copy(src_hbm, 0, dst, j, u, sem).wait()
        return carry
    lax.fori_loop(0, n_groups, body, 0)


def _expert_kernel(blk_e_ref, n_used_ref, first_ref, wslot_ref, next_e_ref, row_tok_ref,
                   x_hbm, wg_hbm, wu_hbm, wd_hbm, o_ref,
                   xbuf, xsem, wg_f, wu_f, wd_f, wsem, wg_b, wu_b, wd_b, *, layer):
    i = pl.program_id(0)
    n_used = n_used_ref[0]
    groups, d = xbuf.shape[1], xbuf.shape[3]
    t = groups * SUBLANES
    hid = wg_b.shape[1]
    used = i < n_used
    slot = lax.rem(i, 2)

    def weight_copies(e, s):
        return [pltpu.make_async_copy(w.at[layer, e], buf.at[s], wsem.at[s])
                for w, buf in ((wg_hbm, wg_f), (wu_hbm, wu_f), (wd_hbm, wd_f))]

    def rows_start(blk, s, group_range):
        for j in group_range:
            for u in range(SUBLANES):
                _row_copy(x_hbm, row_tok_ref[blk * t + j * SUBLANES + u], xbuf.at[s], j, u,
                          xsem.at[s]).start()

    def rows_wait(s):
        for j in range(groups):
            for u in range(SUBLANES):
                _row_copy(x_hbm, 0, xbuf.at[s], j, u, xsem.at[s]).wait()

    @pl.when(i == 0)
    def _():
        for c in weight_copies(blk_e_ref[0], 0):
            c.start(priority=1)
        rows_start(0, 0, range(groups))

    @pl.when(used)
    def _():
        rows_wait(slot)
        ws = wslot_ref[i]

        @pl.when(first_ref[i] == 1)
        def _():
            for c in weight_copies(0, ws):
                c.wait()

            @pl.when(next_e_ref[i] >= 0)
            def _():
                for c in weight_copies(next_e_ref[i], 1 - ws):
                    c.start(priority=1)

            wg_b[...] = wg_f[ws].astype(wg_b.dtype)
            wu_b[...] = wu_f[ws].astype(wu_b.dtype)
            wd_b[...] = wd_f[ws].astype(wd_b.dtype)

        nxt = jnp.minimum(i + 1, n_used - 1)
        xb = xbuf[slot].reshape(t, d).astype(MXU_DT)
        pieces = 2
        y = None
        for piece in range(pieces):
            cols = slice(piece * hid // pieces, (piece + 1) * hid // pieces)
            gate = _dot(xb, wg_b[:, cols])
            up = _dot(xb, wu_b[:, cols])
            rows_start(nxt, 1 - slot, range(piece * groups // pieces, (piece + 1) * groups // pieces))
            hidden = (gate * jax.nn.sigmoid(gate) * up).astype(MXU_DT)
            part = _dot(hidden, wd_b[cols, :])
            y = part if y is None else y + part
        o_ref[...] = y

        @pl.when(i == n_used - 1)
        def _():
            rows_wait(1 - slot)

    @pl.when(jnp.logical_not(used))
    def _():
        o_ref[...] = jnp.zeros(o_ref.shape, o_ref.dtype)


def _expert_ffn(x1, row_tok, blk_e, n_used, first, wslot, next_e, wg, wu, wd, layer):
    d = x1.shape[1]
    p = row_tok.shape[0]
    t = EXPERT_ROWS
    hid = wg.shape[3]
    any_spec = pl.BlockSpec(memory_space=pl.ANY)
    return pl.pallas_call(
        functools.partial(_expert_kernel, layer=layer),
        grid_spec=pltpu.PrefetchScalarGridSpec(
            num_scalar_prefetch=6,
            grid=(p // t,),
            in_specs=[any_spec, any_spec, any_spec, any_spec],
            out_specs=pl.BlockSpec((t, d), lambda i, *_: (i, 0)),
            scratch_shapes=[pltpu.VMEM((2, t // SUBLANES, SUBLANES, d), F32), pltpu.SemaphoreType.DMA((2,)),
                            pltpu.VMEM((2, d, hid), F32), pltpu.VMEM((2, d, hid), F32),
                            pltpu.VMEM((2, hid, d), F32), pltpu.SemaphoreType.DMA((2,)),
                            pltpu.VMEM((d, hid), MXU_DT), pltpu.VMEM((d, hid), MXU_DT),
                            pltpu.VMEM((hid, d), MXU_DT)]),
        out_shape=jax.ShapeDtypeStruct((p, d), F32),
        compiler_params=_params(("arbitrary",)),
        name="expert_ffn",
    )(blk_e, n_used, first, wslot, next_e, row_tok, x1, wg, wu, wd)


def _combine_kernel(dest_ref, ys_hbm, route_ref, x_ref, g_ref, b_ref, o_ref, ybuf, sem, *, alpha, n_tok):
    i = pl.program_id(0)
    tm, d = x_ref.shape
    groups = tm // SUBLANES
    slot = lax.rem(i, 2)

    def start(blk, s):
        for k in range(2):
            _rows_start(ys_hbm, dest_ref, k * n_tok + blk * tm, ybuf.at[s, k], sem.at[s], groups)

    @pl.when(i == 0)
    def _():
        start(0, 0)

    @pl.when(i + 1 < pl.num_programs(0))
    def _():
        start(i + 1, 1 - slot)

    for k in range(2):
        _rows_wait(ys_hbm, ybuf.at[slot, k], sem.at[slot], groups)
    r = route_ref[...]
    y = r[:, 2:3] * ybuf[slot, 0].reshape(tm, d) + r[:, 3:4] * ybuf[slot, 1].reshape(tm, d)
    o_ref[...] = _layer_norm(alpha * x_ref[...] + y, g_ref[...], b_ref[...])


def _combine(ys, dest_kmajor, route, x1, ln_g, ln_b, alpha, layer):
    n, d = x1.shape
    tm = min(ROW_TILE, n)
    row = lambda i, dst: (i, 0)
    return pl.pallas_call(
        functools.partial(_combine_kernel, alpha=alpha, n_tok=n),
        grid_spec=pltpu.PrefetchScalarGridSpec(
            num_scalar_prefetch=1,
            grid=(n // tm,),
            in_specs=[pl.BlockSpec(memory_space=pl.ANY),
                      pl.BlockSpec((tm, LANES), row), pl.BlockSpec((tm, d), row),
                      _layer_spec(ln_g, layer), _layer_spec(ln_b, layer)],
            out_specs=pl.BlockSpec((tm, d), row),
            scratch_shapes=[pltpu.VMEM((2, 2, tm // SUBLANES, SUBLANES, d), F32),
                            pltpu.SemaphoreType.DMA((2,))]),
        out_shape=jax.ShapeDtypeStruct((n, d), F32),
        compiler_params=_params(("arbitrary",)),
        name="combine_ln",
    )(dest_kmajor, ys, route, x1, ln_g, ln_b)


def _dispatch_plan(route, n_tokens):
    t = EXPERT_ROWS
    a = 2 * n_tokens
    eid = route[:, 0:2].astype(jnp.int32).reshape(a)
    onehot = (eid[:, None] == jnp.arange(N_EXPERTS, dtype=jnp.int32)[None, :]).astype(jnp.int32)
    counts = onehot.sum(axis=0)
    before = jnp.cumsum(onehot, axis=0) - onehot
    rank = jnp.take_along_axis(before, eid[:, None], axis=1)[:, 0]
    padded = (counts + t - 1) // t * t
    pend = jnp.cumsum(padded)
    pstart = pend - padded
    dest = (pstart[eid] + rank).astype(jnp.int32)
    p_rows = a + N_EXPERTS * t
    filler = jnp.arange(p_rows, dtype=jnp.int32) % n_tokens
    row_tok = filler.at[dest].set(jnp.arange(a, dtype=jnp.int32) // 2)
    blk_start = jnp.arange(p_rows // t, dtype=jnp.int32) * t
    blk_e = jnp.minimum((pend[None, :] <= blk_start[:, None]).astype(jnp.int32).sum(axis=1), N_EXPERTS - 1)
    blk_e = blk_e.astype(jnp.int32)
    n_used = (pend[-1] // t).astype(jnp.int32)
    blk = jnp.arange(p_rows // t, dtype=jnp.int32)
    first = jnp.concatenate([jnp.ones((1,), jnp.int32), (blk_e[1:] != blk_e[:-1]).astype(jnp.int32)])
    wslot = (jnp.cumsum(first) - 1) % 2
    is_used = blk < n_used
    n_le = ((blk_e[None, :] <= blk_e[:, None]) & is_used[None, :]).astype(jnp.int32).sum(axis=1)
    next_e = jnp.where(n_le < n_used, blk_e[jnp.minimum(n_le, p_rows // t - 1)], -1)
    dest_kmajor = dest.reshape(n_tokens, 2).T.reshape(a)
    return (dest_kmajor, row_tok, blk_e, n_used.reshape(1), first, wslot.astype(jnp.int32),
            next_e.astype(jnp.int32))


def kernel(x, w_in, b_in, swa_sinks, cmp_pe_k, cmp_w1_k, cmp_w2_k, cmp_pe_v, cmp_w1_v, cmp_w2_v,
           w_out, b_out, ln1_g, ln1_b, w_group, b_group, w_expert, b_expert, we_gate, we_up,
           we_down, ln2_g, ln2_b):
    bsz, seq, d_model = x.shape
    depth = w_in.shape[0]
    n_tok = bsz * seq
    n_heads = d_model // HEAD_DIM
    q_w = n_heads // 2 * HEAD_DIM
    kv_w = N_KV_GROUPS * HEAD_DIM
    alpha = (2.0 * depth) ** 0.25
    scale = HEAD_DIM ** -0.5 * LOG2E
    assert q_w == N_KV_GROUPS * PAIRS * LANES and seq % SLC_CHUNK == 0

    slopes_a, slopes_n = _alibi_slopes(n_heads)
    slopes_a = (slopes_a * np.float32(LOG2E)).astype(np.float32)
    slopes_n = (slopes_n * np.float32(LOG2E)).astype(np.float32)
    qfeat_a = jnp.asarray(_query_feats(slopes_a), F32)
    qfeat_n = jnp.asarray(_query_feats(slopes_n), F32)
    pos = np.arange(seq)
    kfeat_swa = jnp.asarray(_key_feats(pos, SWA_WINDOW // BAND_CHUNK, BAND_CHUNK, False), MXU_DT)
    kfeat_win = jnp.asarray(_key_feats(pos, NSA_WINDOW // BAND_CHUNK, BAND_CHUNK, False), MXU_DT)
    kfeat_slc = jnp.asarray(_key_feats(pos, 0, SLC_CHUNK, True), MXU_DT)
    n_cmp = (seq - NSA_CMP_LEN) // NSA_CMP_STRIDE + 1
    n_cmp_pad = seq // NSA_CMP_STRIDE
    cmp_end = np.arange(n_cmp_pad) * NSA_CMP_STRIDE + NSA_CMP_LEN - 1
    kfeat_cmp = jnp.asarray(_key_feats(cmp_end, 0, n_cmp_pad, False), MXU_DT)
    n_slc = seq // NSA_SLC_LEN
    cs = np.arange(n_cmp_pad)[None, :] * NSA_CMP_STRIDE
    ss = np.arange(LANES)[:, None] * NSA_SLC_LEN
    ovt = ((cs < ss + NSA_SLC_LEN) & (cs + NSA_CMP_LEN - 1 >= ss)
           & (np.arange(n_cmp_pad)[None, :] < n_cmp) & (np.arange(LANES)[:, None] < n_slc))
    ovt = jnp.asarray(ovt.astype(np.float32), MXU_DT)
    slopes_a_s = jnp.asarray(slopes_a, F32)

    o_qa, o_ka = 0, q_w
    o_qn = q_w + 2 * kv_w
    o_kvn = o_qn + q_w
    o_gn = o_kvn + 6 * kv_w
    gate_w = 3 * HEADS_PER_GROUP

    def seg(a, lo, hi):
        return a[..., lo:hi]

    def gate_cols(a):
        pad = [(0, 0)] * (a.ndim - 1) + [(0, LANES - gate_w)]
        return jnp.concatenate([jnp.pad(seg(a, o_gn + g * gate_w, o_gn + (g + 1) * gate_w), pad)
                                for g in range(N_KV_GROUPS)], axis=-1)

    def regroup(a):
        return jnp.concatenate([seg(a, o_qa, o_qa + q_w) * scale, seg(a, o_qn, o_qn + q_w) * scale,
                                seg(a, o_ka, o_ka + 2 * kv_w), seg(a, o_kvn, o_kvn + 6 * kv_w),
                                gate_cols(a)], axis=-1)

    w_r = regroup(w_in).astype(MXU_DT)
    b_r = regroup(b_in)[:, None, :]
    flat = NSA_CMP_LEN * HEAD_DIM
    pe = jnp.broadcast_to(jnp.stack([cmp_pe_k, cmp_pe_v], axis=1).reshape(depth, 2, 1, flat),
                          (depth, 2, 8, flat))
    w1 = jnp.stack([cmp_w1_k, cmp_w1_v], axis=1).astype(MXU_DT)
    w2 = jnp.stack([cmp_w2_k, cmp_w2_v], axis=1).astype(MXU_DT)
    n_route_pad = LANES - N_GROUPS - N_EXPERTS
    wr = jnp.concatenate([w_group, w_expert, jnp.zeros((depth, d_model, n_route_pad), F32)], axis=2)
    br = jnp.concatenate([b_group, b_expert, jnp.zeros((depth, n_route_pad), F32)], axis=1)[:, None, :]
    wr_hi = wr.astype(MXU_DT)
    wr = jnp.concatenate([wr_hi, (wr - wr_hi.astype(F32)).astype(MXU_DT)], axis=2)
    w_o = w_out.astype(MXU_DT)
    b_o, g1, be1, g2, be2 = (a[:, None, :] for a in (b_out, ln1_g, ln1_b, ln2_g, ln2_b))

    h = x.reshape(n_tok, d_model)
    for l in range(depth):
        qa, qn, kv, gates = _input_projection(h, w_r, b_r, l)
        qa = qa.reshape(bsz, seq, q_w)
        qn = qn.reshape(bsz, seq, q_w)
        kv = kv.reshape(bsz, seq, 8 * kv_w)
        gates = gates.reshape(bsz, seq, 2 * LANES)

        o_a = _banded_attention(qa, kv, 0, 1, qfeat_a, kfeat_swa, SWA_WINDOW,
                                sinks=swa_sinks, slopes=slopes_a_s, layer=l)

        kvc = kv[:, :, 2 * kv_w:4 * kv_w].reshape(bsz, n_cmp_pad, NSA_CMP_STRIDE, 2, N_KV_GROUPS, HEAD_DIM)
        kvc = kvc.transpose(0, 3, 4, 1, 2, 5).reshape(bsz, 2, N_KV_GROUPS, n_cmp_pad, NSA_CMP_STRIDE * HEAD_DIM)
        kvcmp = _compress(kvc, pe, w1, w2, l)

        o_cmp, maskfeat = _compressed_and_select(qn, kvcmp, qfeat_n, kfeat_cmp, ovt, gates)
        o_slc = _selected_attention(qn, kv, 4, 5, qfeat_n, maskfeat, kfeat_slc, gates)
        o_win = _banded_attention(qn, kv, 6, 7, qfeat_n, kfeat_win, NSA_WINDOW,
                                  gates=gates, gate_col=2)

        x1, route = _output_projection(
            o_a.reshape(n_tok, q_w), o_cmp.reshape(n_tok, q_w), o_slc.reshape(n_tok, q_w),
            o_win.reshape(n_tok, q_w), h, w_o, b_o, g1, be1, wr, br, alpha, l)

        dest_kmajor, row_tok, blk_e, n_used, first, wslot, next_e = _dispatch_plan(route, n_tok)
        ys = _expert_ffn(x1, row_tok, blk_e, n_used, first, wslot, next_e, we_gate, we_up, we_down, l)
        h = _combine(ys, dest_kmajor, route, x1, g2, be2, alpha, l)
    return h.reshape(bsz, seq, d_model)
```
